```python
import functools
import jax
import jax.numpy as jnp
from jax import lax
import numpy as np

D_MODEL = 1024
BATCH = 2
SEQ = 8192
DEPTH = 1
DEC_BATCH = 128
DEC_SEQ = 1
PAST_LEN = 16384
PAGE_SIZE = 128

FFN_DIM = 2816
RWKV_HEADS = 8
RWKV_HEAD = 64
RWKV_DIM = RWKV_HEADS * RWKV_HEAD
DECAY_LORA = 64
ICLR_LORA = 64
GATE_LORA = 128
RWKV_PROJ = 3 * RWKV_DIM + DECAY_LORA + ICLR_LORA + GATE_LORA
LNX_EPS = 64e-5
MLA_HEADS = 8
QK_NOPE = 64
QK_ROPE = 32
V_HEAD = 64
Q_LORA = 768
KV_LORA = 256
CACHE_W = KV_LORA + QK_ROPE
MLA_SCALE = (QK_NOPE + QK_ROPE) ** -0.5
ROPE_BASE = 10000.0
Q_BLOCK = 128
MEM_TOKENS = 256
MEM_HEADS = 4
MEM_HEAD_DIM = 128
MEM_DIM = MEM_HEADS * MEM_HEAD_DIM
N_BRANCH = 3
IN_PROJ = RWKV_PROJ + Q_LORA + KV_LORA + QK_ROPE + MEM_DIM + N_BRANCH * D_MODEL
IN_SPLITS = (RWKV_PROJ,
             RWKV_PROJ + Q_LORA,
             RWKV_PROJ + Q_LORA + KV_LORA,
             RWKV_PROJ + Q_LORA + KV_LORA + QK_ROPE,
             RWKV_PROJ + Q_LORA + KV_LORA + QK_ROPE + MEM_DIM)
RWKV_SPLITS = (RWKV_DIM, 2 * RWKV_DIM, 3 * RWKV_DIM,
               3 * RWKV_DIM + DECAY_LORA, 3 * RWKV_DIM + DECAY_LORA + ICLR_LORA)
RMS_EPS = 1e-6
PAGE_POOL_SPARE = 4

kernel_name = 'hybrid_rwkv7_mla_memory_macaron_step'

WEIGHT_NAMES = (
    'ffn1_pre', 'ffn1_post', 'ffn1_gate', 'ffn1_up', 'ffn1_down',
    'mix_pre', 'mix_post', 'w_in',
    'rwkv_mu', 'rwkv_w0', 'rwkv_w2', 'rwkv_a0', 'rwkv_a2', 'rwkv_g2',
    'rwkv_k_k', 'rwkv_k_a', 'rwkv_r_k', 'rwkv_lnx_g', 'rwkv_lnx_b', 'rwkv_w_o',
    'mla_q_norm', 'mla_w_qb', 'mla_kv_norm', 'mla_w_uk', 'mla_w_uv', 'mla_w_o',
    'mem_norm', 'mem_w_k', 'mem_w_v', 'mem_w_o',
    'w_out',
    'ffn2_pre', 'ffn2_post', 'ffn2_gate', 'ffn2_up', 'ffn2_down',
)


def rmsnorm(x, g):
    xf = x.astype(jnp.float32)
    xf = xf * lax.rsqrt(jnp.mean(xf * xf, axis=-1, keepdims=True) + RMS_EPS)
    return (xf * g.astype(jnp.float32)).astype(x.dtype)


def swiglu_half(x, pre, post, w_gate, w_up, w_down):
    h = rmsnorm(x, pre)
    y = (jax.nn.silu(h @ w_gate) * (h @ w_up)) @ w_down
    return x + 0.5 * rmsnorm(y, post)


def rope(x, pos):
    half = x.shape[-1] // 2
    freqs = ROPE_BASE ** (-jnp.arange(half, dtype=jnp.float32) / half)
    ang = pos.astype(jnp.float32)[:, None] * freqs
    ang = ang.reshape((ang.shape[0],) + (1,) * (x.ndim - 3) + (half,))
    cos, sin = jnp.cos(ang), jnp.sin(ang)
    x1 = x[..., :half].astype(jnp.float32)
    x2 = x[..., half:].astype(jnp.float32)
    return jnp.concatenate([x1 * cos - x2 * sin, x1 * sin + x2 * cos], axis=-1).astype(x.dtype)


def rwkv_scan(r, decay, k, v, a_vec, b_vec, s0):
    xs = tuple(jnp.moveaxis(t.astype(jnp.float32), 1, 0) for t in (r, decay, k, v, a_vec, b_vec))

    def step(s, inp):
        r_t, w_t, k_t, v_t, a_t, b_t = inp
        sa = jnp.einsum('bhij,bhj->bhi', s, a_t)
        s = s * w_t[:, :, None, :] + sa[..., None] * b_t[:, :, None, :] + v_t[..., None] * k_t[:, :, None, :]
        return s, jnp.einsum('bhij,bhj->bhi', s, r_t)

    s, o = lax.scan(step, s0.astype(jnp.float32), xs)
    return jnp.moveaxis(o, 0, 1), s


def rwkv_branch(p, shift0, s0, W):
    B, T, _ = p.shape
    prev = jnp.concatenate([shift0[:, None, :].astype(p.dtype), p[:, :-1]], axis=1)
    ps = p + (prev - p) * W['rwkv_mu']
    r, k, v, wl, al, gl = jnp.split(ps, RWKV_SPLITS, axis=-1)
    w_log = -jax.nn.softplus(-(W['rwkv_w0'] + jnp.tanh(wl) @ W['rwkv_w2']).astype(jnp.float32)) - 0.5
    decay = jnp.exp(-jnp.exp(w_log))
    a = jax.nn.sigmoid(W['rwkv_a0'] + al @ W['rwkv_a2'])
    g = jax.nn.sigmoid(gl) @ W['rwkv_g2']

    def heads(t):
        return t.reshape(B, T, RWKV_HEADS, RWKV_HEAD)

    kk = heads(k * W['rwkv_k_k']).astype(jnp.float32)
    kk = kk / jnp.maximum(jnp.sqrt(jnp.sum(kk * kk, axis=-1, keepdims=True)), 1e-12)
    k = k * (1.0 + (a - 1.0) * W['rwkv_k_a'])
    rh, kh, vh = heads(r), heads(k), heads(v)
    o, s = rwkv_scan(rh, heads(decay), kh, vh, -kk, kk * heads(a).astype(jnp.float32), s0)
    mean = jnp.mean(o, axis=-1, keepdims=True)
    var = jnp.mean(jnp.square(o - mean), axis=-1, keepdims=True)
    o = ((o - mean) * lax.rsqrt(var + LNX_EPS)).reshape(B, T, RWKV_DIM)
    o = o * W['rwkv_lnx_g'].astype(jnp.float32) + W['rwkv_lnx_b'].astype(jnp.float32)
    bonus = jnp.sum((rh * kh * W['rwkv_r_k']).astype(jnp.float32), axis=-1, keepdims=True) * vh.astype(jnp.float32)
    o = (o + bonus.reshape(B, T, RWKV_DIM)).astype(p.dtype)
    return (o * g) @ W['rwkv_w_o'], p[:, -1], s.astype(s0.dtype)


def mem_kv(mem, W):
    B, M, _ = mem.shape
    m = rmsnorm(mem, W['mem_norm'])
    k = (m @ W['mem_w_k']).reshape(B, M, MEM_HEADS, MEM_HEAD_DIM)
    v = (m @ W['mem_w_v']).reshape(B, M, MEM_HEADS, MEM_HEAD_DIM)
    return k, v


def mem_attend(q, mem_k, mem_v):
    B, T, _ = q.shape
    q = q.reshape(B, T, MEM_HEADS, MEM_HEAD_DIM)
    s = jnp.einsum('bqhd,bkhd->bhqk', q, mem_k).astype(jnp.float32) * (MEM_HEAD_DIM ** -0.5)
    prob = jax.nn.softmax(s, axis=-1).astype(mem_v.dtype)
    return jnp.einsum('bhqk,bkhd->bqhd', prob, mem_v).reshape(B, T, MEM_DIM)


def prompt_attend(q_lat, q_pe, c_kv, k_pe):
    B, T, H, L = q_lat.shape
    nb = T // Q_BLOCK
    ql = jnp.moveaxis(q_lat.reshape(B, nb, Q_BLOCK, H, L), 1, 0)
    qp = jnp.moveaxis(q_pe.reshape(B, nb, Q_BLOCK, H, QK_ROPE), 1, 0)
    key_pos = jnp.arange(T)

    def block(args):
        i, ql_b, qp_b = args
        q_pos = i * Q_BLOCK + jnp.arange(Q_BLOCK)
        mask = key_pos[None, :] <= q_pos[:, None]
        s = (jnp.einsum('bqhl,bkl->bhqk', ql_b, c_kv)
             + jnp.einsum('bqhr,bkr->bhqk', qp_b, k_pe)).astype(jnp.float32) * MLA_SCALE
        s = jnp.where(mask, s, -jnp.inf)
        prob = jax.nn.softmax(s, axis=-1).astype(c_kv.dtype)
        return jnp.einsum('bhqk,bkl->bqhl', prob, c_kv)

    ctx = lax.map(block, (jnp.arange(nb), ql, qp))
    return jnp.moveaxis(ctx, 0, 1).reshape(B, T, H, L)


def sample_attend(q_lat, q_pe, c_kv, k_pe, pool, page_table):
    DB, T = q_lat.shape[0], q_lat.shape[1]
    past = pool[page_table].reshape(DB, -1, CACHE_W)
    p_lat, p_pe = past[..., :KV_LORA], past[..., KV_LORA:]
    s_past = jnp.einsum('bqhl,bkl->bhqk', q_lat, p_lat) + jnp.einsum('bqhr,bkr->bhqk', q_pe, p_pe)
    s_new = jnp.einsum('bqhl,bkl->bhqk', q_lat, c_kv) + jnp.einsum('bqhr,bkr->bhqk', q_pe, k_pe)
    causal = jnp.tril(jnp.ones((T, T), dtype=bool))
    s_new = jnp.where(causal, s_new.astype(jnp.float32), -jnp.inf)
    s = jnp.concatenate([s_past.astype(jnp.float32), s_new], axis=-1) * MLA_SCALE
    prob = jax.nn.softmax(s, axis=-1).astype(c_kv.dtype)
    n_past = p_lat.shape[1]
    return (jnp.einsum('bhqk,bkl->bqhl', prob[..., :n_past], p_lat)
            + jnp.einsum('bhqk,bkl->bqhl', prob[..., n_past:], c_kv))


def layer(x, pos, shift0, wkv0, mem_k, mem_v, attend, W):
    B, T, _ = x.shape
    h = swiglu_half(x, W['ffn1_pre'], W['ffn1_post'], W['ffn1_gate'], W['ffn1_up'], W['ffn1_down'])
    u = rmsnorm(h, W['mix_pre'])
    p = u @ W['w_in']
    p_rwkv, c_q, c_kv, k_pe, q_mem, gate_logits = jnp.split(p, IN_SPLITS, axis=-1)
    o_rwkv, shift_new, wkv_new = rwkv_branch(p_rwkv, shift0, wkv0, W)
    q = (rmsnorm(c_q, W['mla_q_norm']) @ W['mla_w_qb']).reshape(B, T, MLA_HEADS, QK_NOPE + QK_ROPE)
    q_pe = rope(q[..., QK_NOPE:], pos)
    q_lat = jnp.einsum('bthn,lhn->bthl', q[..., :QK_NOPE], W['mla_w_uk'])
    c_kv = rmsnorm(c_kv, W['mla_kv_norm'])
    k_pe = rope(k_pe, pos)
    ctx_lat = attend(q_lat, q_pe, c_kv, k_pe)
    o_mla = jnp.einsum('bthl,lhv->bthv', ctx_lat, W['mla_w_uv']).reshape(B, T, MLA_HEADS * V_HEAD) @ W['mla_w_o']
    o_mem = mem_attend(q_mem, mem_k, mem_v) @ W['mem_w_o']
    g = jax.nn.sigmoid(gate_logits).reshape(B, T, N_BRANCH, D_MODEL)
    merged = g[:, :, 0] * o_rwkv + g[:, :, 1] * o_mla + g[:, :, 2] * o_mem
    h = h + rmsnorm(merged @ W['w_out'], W['mix_post'])
    y = swiglu_half(h, W['ffn2_pre'], W['ffn2_post'], W['ffn2_gate'], W['ffn2_up'], W['ffn2_down'])
    rows = jnp.concatenate([c_kv, k_pe], axis=-1)
    return y, rows, shift_new, wkv_new


def setup_inputs(seed: int = 0) -> dict:
    key = jax.random.key(seed)
    keys = iter(jax.random.split(key, 64))
    f32 = jnp.float32

    def normal(shape, scale):
        return scale * jax.random.normal(next(keys), (DEPTH,) + shape, f32)

    def gain(n):
        return 1.0 + normal((n,), 0.05)

    def uniform(shape, lo, hi):
        return jax.random.uniform(next(keys), (DEPTH,) + shape, f32, lo, hi)

    n_pages = PAST_LEN // PAGE_SIZE
    n_used = DEC_BATCH * n_pages
    n_phys = n_used + n_used // PAGE_POOL_SPARE
    page_table = jax.random.permutation(next(keys), n_phys)[:n_used].reshape(DEC_BATCH, n_pages).astype(jnp.int32)
    return {
        'x_prompt': jax.random.normal(next(keys), (BATCH, SEQ, D_MODEL), f32),
        'x_sample': jax.random.normal(next(keys), (DEC_BATCH, DEC_SEQ, D_MODEL), f32),
        'cache_mla': normal((n_phys, PAGE_SIZE, CACHE_W), 1.0),
        'state_rwkv': normal((DEC_BATCH, RWKV_HEADS, RWKV_HEAD, RWKV_HEAD), 0.5),
        'state_shift': normal((DEC_BATCH, RWKV_PROJ), 1.0),
        'cache_mem_k': normal((DEC_BATCH, MEM_TOKENS, MEM_HEADS, MEM_HEAD_DIM), 1.0),
        'cache_mem_v': normal((DEC_BATCH, MEM_TOKENS, MEM_HEADS, MEM_HEAD_DIM), 1.0),
        'page_table': page_table,
        'mem_prompt': jax.random.normal(next(keys), (BATCH, MEM_TOKENS, D_MODEL), f32),
        'ffn1_pre': gain(D_MODEL),
        'ffn1_post': gain(D_MODEL),
        'ffn1_gate': normal((D_MODEL, FFN_DIM), D_MODEL ** -0.5),
        'ffn1_up': normal((D_MODEL, FFN_DIM), D_MODEL ** -0.5),
        'ffn1_down': normal((FFN_DIM, D_MODEL), FFN_DIM ** -0.5),
        'mix_pre': gain(D_MODEL),
        'mix_post': gain(D_MODEL),
        'w_in': normal((D_MODEL, IN_PROJ), D_MODEL ** -0.5),
        'rwkv_mu': uniform((RWKV_PROJ,), 0.0, 1.0),
        'rwkv_w0': uniform((RWKV_DIM,), -5.0, -0.5),
        'rwkv_w2': normal((DECAY_LORA, RWKV_DIM), 0.1 * DECAY_LORA ** -0.5),
        'rwkv_a0': normal((RWKV_DIM,), 0.1),
        'rwkv_a2': normal((ICLR_LORA, RWKV_DIM), ICLR_LORA ** -0.5),
        'rwkv_g2': normal((GATE_LORA, RWKV_DIM), GATE_LORA ** -0.5),
        'rwkv_k_k': 0.85 + normal((RWKV_DIM,), 0.05),
        'rwkv_k_a': 1.0 + normal((RWKV_DIM,), 0.05),
        'rwkv_r_k': normal((RWKV_HEADS, RWKV_HEAD), 0.1),
        'rwkv_lnx_g': gain(RWKV_DIM),
        'rwkv_lnx_b': normal((RWKV_DIM,), 0.01),
        'rwkv_w_o': normal((RWKV_DIM, D_MODEL), RWKV_DIM ** -0.5),
        'mla_q_norm': gain(Q_LORA),
        'mla_w_qb': normal((Q_LORA, MLA_HEADS * (QK_NOPE + QK_ROPE)), Q_LORA ** -0.5),
        'mla_kv_norm': gain(KV_LORA),
        'mla_w_uk': normal((KV_LORA, MLA_HEADS, QK_NOPE), KV_LORA ** -0.5),
        'mla_w_uv': normal((KV_LORA, MLA_HEADS, V_HEAD), KV_LORA ** -0.5),
        'mla_w_o': normal((MLA_HEADS * V_HEAD, D_MODEL), (MLA_HEADS * V_HEAD) ** -0.5),
        'mem_norm': gain(D_MODEL),
        'mem_w_k': normal((D_MODEL, MEM_DIM), D_MODEL ** -0.5),
        'mem_w_v': normal((D_MODEL, MEM_DIM), D_MODEL ** -0.5),
        'mem_w_o': normal((MEM_DIM, D_MODEL), MEM_DIM ** -0.5),
        'w_out': normal((D_MODEL, D_MODEL), D_MODEL ** -0.5),
        'ffn2_pre': gain(D_MODEL),
        'ffn2_post': gain(D_MODEL),
        'ffn2_gate': normal((D_MODEL, FFN_DIM), D_MODEL ** -0.5),
        'ffn2_up': normal((D_MODEL, FFN_DIM), D_MODEL ** -0.5),
        'ffn2_down': normal((FFN_DIM, D_MODEL), FFN_DIM ** -0.5),
    }


def reference(x_prompt, x_sample, cache_mla, state_rwkv, state_shift, cache_mem_k, cache_mem_v,
              page_table, mem_prompt,
              ffn1_pre, ffn1_post, ffn1_gate, ffn1_up, ffn1_down,
              mix_pre, mix_post, w_in,
              rwkv_mu, rwkv_w0, rwkv_w2, rwkv_a0, rwkv_a2, rwkv_g2,
              rwkv_k_k, rwkv_k_a, rwkv_r_k, rwkv_lnx_g, rwkv_lnx_b, rwkv_w_o,
              mla_q_norm, mla_w_qb, mla_kv_norm, mla_w_uk, mla_w_uv, mla_w_o,
              mem_norm, mem_w_k, mem_w_v, mem_w_o,
              w_out,
              ffn2_pre, ffn2_post, ffn2_gate, ffn2_up, ffn2_down):
    stacked = (ffn1_pre, ffn1_post, ffn1_gate, ffn1_up, ffn1_down,
               mix_pre, mix_post, w_in,
               rwkv_mu, rwkv_w0, rwkv_w2, rwkv_a0, rwkv_a2, rwkv_g2,
               rwkv_k_k, rwkv_k_a, rwkv_r_k, rwkv_lnx_g, rwkv_lnx_b, rwkv_w_o,
               mla_q_norm, mla_w_qb, mla_kv_norm, mla_w_uk, mla_w_uv, mla_w_o,
               mem_norm, mem_w_k, mem_w_v, mem_w_o,
               w_out,
               ffn2_pre, ffn2_post, ffn2_gate, ffn2_up, ffn2_down)
    B, S, _ = x_prompt.shape
    DB, T, _ = x_sample.shape
    pos_p = jnp.arange(S)
    pos_s = PAST_LEN + jnp.arange(T)
    shift_zero = jnp.zeros((B, RWKV_PROJ), x_prompt.dtype)
    wkv_zero = jnp.zeros((B, RWKV_HEADS, RWKV_HEAD, RWKV_HEAD), state_rwkv.dtype)
    x_p, x_s = x_prompt, x_sample
    rows_p, rows_s, wkv_p, wkv_s, sh_p, sh_s, mk_p, mv_p = [], [], [], [], [], [], [], []
    for l in range(DEPTH):
        W = {name: w[l] for name, w in zip(WEIGHT_NAMES, stacked)}
        mk, mv = mem_kv(mem_prompt, W)
        x_p, r_p, s_p, st_p = layer(x_p, pos_p, shift_zero, wkv_zero, mk, mv, prompt_attend, W)
        attend_s = functools.partial(sample_attend, pool=cache_mla[l], page_table=page_table)
        x_s, r_s, s_s, st_s = layer(x_s, pos_s, state_shift[l], state_rwkv[l],
                                    cache_mem_k[l], cache_mem_v[l], attend_s, W)
        rows_p.append(r_p)
        rows_s.append(r_s)
        wkv_p.append(st_p)
        wkv_s.append(st_s)
        sh_p.append(s_p)
        sh_s.append(s_s)
        mk_p.append(mk)
        mv_p.append(mv)
    mla_rows_prompt = jnp.stack(rows_p)
    mla_rows_sample = jnp.stack(rows_s)
    rwkv_prompt = jnp.stack(wkv_p)
    rwkv_sample = jnp.stack(wkv_s)
    shift_prompt = jnp.stack(sh_p)
    shift_sample = jnp.stack(sh_s)
    mem_k_prompt = jnp.stack(mk_p)
    mem_v_prompt = jnp.stack(mv_p)
    return (x_p, x_s, mla_rows_prompt, mla_rows_sample, rwkv_prompt, rwkv_sample,
            shift_prompt, shift_sample, mem_k_prompt, mem_v_prompt)
```

```python
import functools

import jax
import jax.numpy as jnp
from jax import lax
from jax.experimental import pallas as pl
from jax.experimental.pallas import tpu as pltpu

F32, BF16 = jnp.float32, jnp.bfloat16
RMS_EPS = 1e-6
LNX_EPS = 64e-5
ROPE_BASE = 10000.0
LANES = 128
VMEM_LIMIT = 52 * 1024 * 1024
RWKV_CHUNK = 64
ATT_TQ = 128
ATT_TK = 512
PAGES_PER_STEP = 16


def _cparams(*sem):
    return pltpu.CompilerParams(dimension_semantics=sem, vmem_limit_bytes=VMEM_LIMIT)


def _resident(shape):
    nd = len(shape)
    return pl.BlockSpec(shape, lambda *_: (0,) * nd, pipeline_mode=pl.Buffered(1))


def _rms(x, g):
    return x * lax.rsqrt(jnp.mean(x * x, axis=-1, keepdims=True) + RMS_EPS) * g


def _sigmoid(x):
    return 1.0 / (1.0 + jnp.exp(-x))


def _mm(a, b):
    return jnp.dot(a.astype(BF16), b.astype(BF16), preferred_element_type=F32)


def _mm_nt(a, b):
    return lax.dot_general(a.astype(BF16), b.astype(BF16), (((1,), (1,)), ((), ())), preferred_element_type=F32)


def _ffn_body(x_ref, pre_ref, post_ref, wg_ref, wu_ref, wd_ref, o_ref):
    x = x_ref[...]
    h = _rms(x, pre_ref[...]).astype(BF16)
    g = jnp.dot(h, wg_ref[...], preferred_element_type=F32)
    u = jnp.dot(h, wu_ref[...], preferred_element_type=F32)
    act = (g * _sigmoid(g)) * u
    y = jnp.dot(act.astype(BF16), wd_ref[...], preferred_element_type=F32)
    o_ref[...] = x + 0.5 * _rms(y, post_ref[...])


def _ffn(x, pre, post, wg, wu, wd, tm):
    n, d = x.shape
    f = wg.shape[1]
    return pl.pallas_call(
        _ffn_body,
        grid=(n // tm,),
        in_specs=[pl.BlockSpec((tm, d), lambda i: (i, 0)), _resident((1, d)), _resident((1, d)),
                  _resident((d, f)), _resident((d, f)), _resident((f, d))],
        out_specs=pl.BlockSpec((tm, d), lambda i: (i, 0)),
        out_shape=jax.ShapeDtypeStruct((n, d), F32),
        compiler_params=_cparams("parallel"),
        name="ffn",
    )(x, pre, post, wg, wu, wd)


def _inproj_body(dims, h_ref, pre_ref, win_ref, qn_ref, wqb_ref, wuk_ref, kvn_ref, cos_ref, sin_ref,
                 prw_ref, q_ref, rows_ref, kbf_ref, qmem_ref, gates_ref, u_ref=None):
    rp, ql, kl, rope, md, gd, nh, nope = dims
    u = _rms(h_ref[...], pre_ref[...]).astype(BF16)
    p = jnp.dot(u, win_ref[...], preferred_element_type=F32)
    o = 0
    prw_ref[...] = p[:, o:o + rp]; o += rp
    cq = p[:, o:o + ql]; o += ql
    ckv = p[:, o:o + kl]; o += kl
    qmem_ref[...] = p[:, o:o + md].astype(BF16); o += md
    gates_ref[...] = _sigmoid(p[:, o:o + gd]); o += gd
    kpe = p[:, o:o + rope]; o += rope
    kpe_sw = p[:, o:o + rope]
    cos = cos_ref[...]
    sin = sin_ref[...]
    q = jnp.dot(_rms(cq, qn_ref[...]).astype(BF16), wqb_ref[...], preferred_element_type=F32)
    nn = nh * nope
    nr = nh * rope
    qpe = (q[:, nn:nn + nr] * cos + q[:, nn + nr:nn + 2 * nr] * sin).astype(BF16)
    qn = q[:, :nn].astype(BF16)
    for pr in range(nh // 2):
        qlat = jnp.dot(qn[:, LANES * pr:LANES * (pr + 1)], wuk_ref[pr], preferred_element_type=F32).astype(BF16)
        for e in range(2):
            hh = 2 * pr + e
            q_ref[0, hh, :, 0:kl] = qlat[:, kl * e:kl * (e + 1)]
            q_ref[0, hh, :, kl:kl + rope] = qpe[:, rope * hh:rope * (hh + 1)]
    ckvn = _rms(ckv, kvn_ref[...])
    kper = kpe * cos[:, :rope] + kpe_sw * sin[:, :rope]
    rows_ref[:, 0:kl] = ckvn
    rows_ref[:, kl:kl + rope] = kper
    kbf_ref[:, 0:kl] = ckvn.astype(BF16)
    kbf_ref[:, kl:kl + rope] = kper.astype(BF16)


def _inproj(h, nbatch, pre, win, qn, wqb, wuk, kvn, cos, sin, dims, tm):
    n, d = h.shape
    rp, ql, kl, rope, md, gd, nh, nope = dims
    t = n // nbatch
    nb = t // tm
    cw = win.shape[1]
    row = lambda i: (i, 0)
    tab = lambda i: (i % nb, 0)
    return pl.pallas_call(
        functools.partial(_inproj_body, dims),
        grid=(n // tm,),
        in_specs=[pl.BlockSpec((tm, d), row), _resident((1, d)), _resident((d, cw)), _resident((1, ql)),
                  _resident(wqb.shape), _resident(wuk.shape), _resident((1, kl)),
                  pl.BlockSpec((tm, nh * rope), tab), pl.BlockSpec((tm, nh * rope), tab)],
        out_specs=[pl.BlockSpec((tm, rp), row),
                   pl.BlockSpec((1, nh, tm, kl + rope), lambda i: (i // nb, 0, i % nb, 0)),
                   pl.BlockSpec((tm, kl + rope), row), pl.BlockSpec((tm, kl + rope), row),
                   pl.BlockSpec((tm, md), row), pl.BlockSpec((tm, gd), row)],
        out_shape=[jax.ShapeDtypeStruct((n, rp), F32),
                   jax.ShapeDtypeStruct((nbatch, nh, t, kl + rope), BF16),
                   jax.ShapeDtypeStruct((n, kl + rope), F32),
                   jax.ShapeDtypeStruct((n, kl + rope), BF16),
                   jax.ShapeDtypeStruct((n, md), BF16),
                   jax.ShapeDtypeStruct((n, gd), F32)],
        compiler_params=_cparams("parallel"),
        name="inproj",
    )(h, pre, win, qn, wqb, wuk, kvn, cos, sin)


def _segsum(x, e):
    hi = x.astype(BF16)
    lo = (x - hi.astype(F32)).astype(BF16)
    return jnp.dot(hi, e, preferred_element_type=F32) + jnp.dot(lo, e, preferred_element_type=F32)


def _rwkv_prep(p, prev, mu, w0, w2a2, a0, g2, k_k, k_a, e, rd, lora):
    ps = p + (prev - p) * mu
    r = ps[:, 0:rd]
    k = ps[:, rd:2 * rd]
    v = ps[:, 2 * rd:3 * rd]
    wa = ps[:, 3 * rd:3 * rd + 2 * lora]
    gl = ps[:, 3 * rd + 2 * lora:]
    lane = lax.broadcasted_iota(jnp.int32, wa.shape, 1)
    wa = jnp.where(lane < lora, jnp.tanh(wa), wa)
    wa2 = _mm(wa, w2a2)
    x = -(w0 + wa2[:, :rd])
    softplus = jnp.maximum(x, 0.0) + jnp.log(1.0 + jnp.exp(-jnp.abs(x)))
    logdec = -jnp.exp(-softplus - 0.5)
    a = _sigmoid(a0 + wa2[:, rd:])
    g = _mm(_sigmoid(gl), g2)
    kk = k * k_k
    kk = kk / jnp.maximum(jnp.sqrt(_segsum(kk * kk, e)), 1e-12)
    k = k * (1.0 + (a - 1.0) * k_a)
    return r, k, v, logdec, -kk, kk * a, g


def _rwkv_post(o, r, k, v, g, r_k, lng, lnb, e, hd):
    mean = _segsum(o, e) * (1.0 / hd)
    oc = o - mean
    var = _segsum(oc * oc, e) * (1.0 / hd)
    o = oc * lax.rsqrt(var + LNX_EPS) * lng + lnb
    bonus = _segsum(r * k * r_k, e) * v
    return (o + bonus) * g


def _pair_rows(y):
    lo = (lax.broadcasted_iota(jnp.int32, y.shape, 1) % LANES) < (LANES // 2)
    z = jnp.zeros_like(y)
    return jnp.concatenate([jnp.where(lo, y, z), jnp.where(lo, z, y)], axis=0)


def _rwkv_chunk_pair(r, k, v, ld, cum, a, b, s):
    L = r.shape[0]
    cum_l = cum[L - 1:L, :]
    e_neg = jnp.exp(-cum)
    e_pos = jnp.exp(cum)
    e_exc = jnp.exp(cum - ld)
    e_end = jnp.exp(cum_l - cum)
    kt, bt = (k * e_neg).astype(BF16), (b * e_neg).astype(BF16)
    at, rt = a * e_exc, (r * e_pos).astype(BF16)
    bh, kh = b * e_end, k * e_end
    vb = _pair_rows(v.astype(BF16))
    mm = _mm_nt(jnp.concatenate([at.astype(BF16), rt], axis=0),
                jnp.concatenate([_pair_rows(bt), _pair_rows(kt)], axis=0))
    ti = lax.broadcasted_iota(jnp.int32, (L, 2 * L), 0)
    si = lax.broadcasted_iota(jnp.int32, (L, 2 * L), 1) % L
    strict, incl = si < ti, si <= ti
    m_ba = jnp.where(strict, mm[:L, :2 * L], 0.0)
    m_ka = jnp.where(strict, mm[:L, 2 * L:], 0.0)
    m_br = jnp.where(incl, mm[L:, :2 * L], 0.0)
    m_kr = jnp.where(incl, mm[L:, 2 * L:], 0.0)
    x = jnp.concatenate([at, _mm(m_ka, vb)], axis=1)
    pw = m_ba
    span = 1
    while span < L:
        x = x + _mm(pw, _pair_rows(x.astype(BF16)))
        span *= 2
        if span < L:
            pw = _mm(pw, _pair_rows(pw.astype(BF16)))
    w1, uloc = x[:, :LANES], x[:, LANES:]
    sb = s.astype(BF16)
    ur = _mm_nt(w1, sb) + uloc
    o = _mm_nt(rt, sb) + _mm(jnp.concatenate([m_br, m_kr], axis=1),
                             jnp.concatenate([_pair_rows(ur.astype(BF16)), vb], axis=0))
    upd = _mm(jnp.concatenate([ur, v], axis=0).T, jnp.concatenate([bh, kh], axis=0))
    ri = lax.broadcasted_iota(jnp.int32, s.shape, 0) < (LANES // 2)
    ci = lax.broadcasted_iota(jnp.int32, s.shape, 1) < (LANES // 2)
    s_new = s * jnp.exp(cum_l) + jnp.where(ri == ci, upd, 0.0)
    return o, s_new


def _rwkv_prompt_body(rd, lora, hd, p_ref, mu_ref, w0_ref, w2a2_ref, a0_ref, g2_ref, kk_ref, ka_ref, rk_ref,
                      lng_ref, lnb_ref, e_ref, og_ref, st_ref, prev_ref, s_ref):
    c = pl.program_id(1)

    @pl.when(c == 0)
    def _():
        prev_ref[...] = jnp.zeros_like(prev_ref)
        s_ref[...] = jnp.zeros_like(s_ref)

    p = p_ref[0]
    L = p.shape[0]
    rowi = lax.broadcasted_iota(jnp.int32, p.shape, 0)
    prev = jnp.where(rowi == 0, prev_ref[...], pltpu.roll(p, 1, axis=0))
    prev_ref[...] = p[L - 1:L, :]
    e = e_ref[...]
    r, k, v, ld, a, b, g = _rwkv_prep(p, prev, mu_ref[...], w0_ref[...], w2a2_ref[...], a0_ref[...], g2_ref[...],
                                      kk_ref[...], ka_ref[...], e, rd, lora)
    tri = (lax.broadcasted_iota(jnp.int32, (L, L), 1) <= lax.broadcasted_iota(jnp.int32, (L, L), 0)).astype(BF16)
    hi = ld.astype(BF16)
    r1 = ld - hi.astype(F32)
    mid = r1.astype(BF16)
    lo = (r1 - mid.astype(F32)).astype(BF16)
    cum = (jnp.dot(tri, hi, preferred_element_type=F32) + jnp.dot(tri, mid, preferred_element_type=F32)
           + jnp.dot(tri, lo, preferred_element_type=F32))
    outs = []
    for pr in range(rd // LANES):
        sl = slice(LANES * pr, LANES * (pr + 1))
        o, s_new = _rwkv_chunk_pair(r[:, sl], k[:, sl], v[:, sl], ld[:, sl], cum[:, sl], a[:, sl], b[:, sl], s_ref[pr])
        s_ref[pr] = s_new
        outs.append(o)
    o = jnp.concatenate(outs, axis=1)
    og_ref[0] = _rwkv_post(o, r, k, v, g, rk_ref[...], lng_ref[...], lnb_ref[...], e, hd).astype(BF16)

    @pl.when(c == pl.num_programs(1) - 1)
    def _():
        for pr in range(rd // LANES):
            s = s_ref[pr]
            st_ref[0, 2 * pr] = s[:hd, :hd]
            st_ref[0, 2 * pr + 1] = s[hd:, hd:]


def _rwkv_prompt(prw, rw, nh, hd, lora):
    b, t, pw = prw.shape
    rd = nh * hd
    L = RWKV_CHUNK
    vec = [_resident(rw[k].shape) for k in ("mu", "w0", "w2a2", "a0", "g2", "k_k", "k_a", "r_k", "lnx_g", "lnx_b", "e")]
    return pl.pallas_call(
        functools.partial(_rwkv_prompt_body, rd, lora, hd),
        grid=(b, t // L),
        in_specs=[pl.BlockSpec((1, L, pw), lambda i, c: (i, c, 0))] + vec,
        out_specs=[pl.BlockSpec((1, L, rd), lambda i, c: (i, c, 0)),
                   pl.BlockSpec((1, nh, hd, hd), lambda i, c: (i, 0, 0, 0))],
        out_shape=[jax.ShapeDtypeStruct((b, t, rd), BF16), jax.ShapeDtypeStruct((b, nh, hd, hd), F32)],
        scratch_shapes=[pltpu.VMEM((1, pw), F32), pltpu.VMEM((rd // LANES, LANES, LANES), F32)],
        compiler_params=_cparams("parallel", "arbitrary"),
        name="rwkv_prompt",
    )(prw, *[rw[k] for k in ("mu", "w0", "w2a2", "a0", "g2", "k_k", "k_a", "r_k", "lnx_g", "lnx_b", "e")])


def _rwkv_prep_body(rd, lora, p_ref, prev_ref, mu_ref, w0_ref, w2a2_ref, a0_ref, g2_ref, kk_ref, ka_ref, e_ref,
                    r_ref, k_ref, v_ref, w_ref, a_ref, b_ref, g_ref):
    r, k, v, ld, a, b, g = _rwkv_prep(p_ref[...], prev_ref[...], mu_ref[...], w0_ref[...], w2a2_ref[...], a0_ref[...],
                                      g2_ref[...], kk_ref[...], ka_ref[...], e_ref[...], rd, lora)
    r_ref[...] = r
    k_ref[...] = k
    v_ref[...] = v
    w_ref[...] = jnp.exp(ld)
    a_ref[...] = a
    b_ref[...] = b
    g_ref[...] = g


def _rwkv_step_body(nh, s_ref, r_ref, k_ref, vt_ref, w_ref, a_ref, b_ref, so_ref, o_ref):
    hd = s_ref.shape[-1]
    for g in range(s_ref.shape[0]):
        for h in range(nh):
            s = s_ref[g, h]
            row = lambda ref: ref[g, h:h + 1, :]
            sa = jnp.sum(s * row(a_ref), axis=-1, keepdims=True)
            s = s * row(w_ref) + sa * row(b_ref) + vt_ref[g, :, h:h + 1] * row(k_ref)
            so_ref[g, h] = s
            o_ref[g, h * hd:(h + 1) * hd, :] = jnp.sum(s * row(r_ref), axis=-1, keepdims=True)


def _rwkv_post_body(hd, o_ref, r_ref, k_ref, v_ref, g_ref, rk_ref, lng_ref, lnb_ref, e_ref, og_ref):
    og_ref[...] = _rwkv_post(o_ref[...], r_ref[...], k_ref[...], v_ref[...], g_ref[...], rk_ref[...], lng_ref[...],
                             lnb_ref[...], e_ref[...], hd).astype(BF16)


def _rwkv_sample(prw, shift, state, rw, nh, hd, lora, gr):
    n, pw = prw.shape
    rd = nh * hd
    names = ("mu", "w0", "w2a2", "a0", "g2", "k_k", "k_a", "e")
    full = lambda s: pl.BlockSpec(s, lambda: (0,) * len(s))
    vecs = pl.pallas_call(
        functools.partial(_rwkv_prep_body, rd, lora),
        in_specs=[full((n, pw)), full((n, pw))] + [full(rw[k].shape) for k in names],
        out_specs=[full((n, rd))] * 7,
        out_shape=[jax.ShapeDtypeStruct((n, rd), F32)] * 7,
        name="rwkv_prep",
    )(prw, shift, *[rw[k] for k in names])
    r, k, v, w, a, b, g = vecs
    hv = lambda x: x.reshape(n, nh, hd)
    vt = jnp.swapaxes(hv(v), 1, 2)
    vspec = pl.BlockSpec((gr, nh, hd), lambda i: (i, 0, 0))
    sspec = pl.BlockSpec((gr, nh, hd, hd), lambda i: (i, 0, 0, 0))
    s_new, o = pl.pallas_call(
        functools.partial(_rwkv_step_body, nh),
        grid=(n // gr,),
        in_specs=[sspec, vspec, vspec, pl.BlockSpec((gr, hd, nh), lambda i: (i, 0, 0)), vspec, vspec, vspec],
        out_specs=[sspec, pl.BlockSpec((gr, rd, 1), lambda i: (i, 0, 0))],
        out_shape=[jax.ShapeDtypeStruct(state.shape, F32), jax.ShapeDtypeStruct((n, rd, 1), F32)],
        compiler_params=_cparams("parallel"),
        name="rwkv_step",
    )(state, hv(r), hv(k), vt, hv(w), hv(a), hv(b))
    pnames = ("r_k", "lnx_g", "lnx_b", "e")
    og = pl.pallas_call(
        functools.partial(_rwkv_post_body, hd),
        in_specs=[full((n, rd))] * 5 + [full(rw[k].shape) for k in pnames],
        out_specs=full((n, rd)),
        out_shape=jax.ShapeDtypeStruct((n, rd), BF16),
        name="rwkv_post",
    )(o.reshape(n, rd), r, k, v, g, *[rw[k] for k in pnames])
    return og, s_new


def _attn_prompt_body(scale, kl, tk, q_ref, k_ref, o_ref, m_ref, l_ref, acc_ref):
    i = pl.program_id(1)
    nh, tq, dk = q_ref.shape[1:]
    q = q_ref[0].reshape(nh * tq, dk)
    m_ref[...] = jnp.full_like(m_ref, -jnp.inf)
    l_ref[...] = jnp.zeros_like(l_ref)
    acc_ref[...] = jnp.zeros_like(acc_ref)

    def step(j, masked):
        k = k_ref[0, pl.ds(pl.multiple_of(j * tk, tk), tk), :]
        s = lax.dot_general(q, k, (((1,), (1,)), ((), ())), preferred_element_type=F32) * scale
        if masked:
            qpos = i * tq + lax.broadcasted_iota(jnp.int32, s.shape, 0) % tq
            kpos = j * tk + lax.broadcasted_iota(jnp.int32, s.shape, 1)
            s = jnp.where(kpos <= qpos, s, -jnp.inf)
        m_old = m_ref[...]
        m_new = jnp.maximum(m_old, jnp.max(s, axis=-1, keepdims=True))
        alpha = jnp.exp(m_old - m_new)
        p = jnp.exp(s - m_new)
        l_ref[...] = alpha * l_ref[...] + jnp.sum(p, axis=-1, keepdims=True)
        acc_ref[...] = alpha * acc_ref[...] + jnp.dot(p.astype(BF16), k[:, :kl], preferred_element_type=F32)
        m_ref[...] = m_new

    last = (i * tq) // tk
    lax.fori_loop(0, last, lambda j, c: (step(j, False), c)[1], 0)
    step(last, True)
    o_ref[0] = (acc_ref[...] / l_ref[...]).reshape(nh, tq, kl).astype(o_ref.dtype)


def _attn_prompt(q, kbf, scale, kl):
    b, nh, t, dk = q.shape
    tq, tk = _tile(t, ATT_TQ), _tile(t, ATT_TK)
    assert tk % tq == 0
    return pl.pallas_call(
        functools.partial(_attn_prompt_body, scale, kl, tk),
        grid=(b, t // tq),
        in_specs=[pl.BlockSpec((1, nh, tq, dk), lambda bi, i: (bi, 0, i, 0)),
                  pl.BlockSpec((1, t, dk), lambda bi, i: (bi, 0, 0))],
        out_specs=pl.BlockSpec((1, nh, tq, kl), lambda bi, i: (bi, 0, i, 0)),
        out_shape=jax.ShapeDtypeStruct((b, nh, t, kl), BF16),
        scratch_shapes=[pltpu.VMEM((nh * tq, 1), F32), pltpu.VMEM((nh * tq, 1), F32), pltpu.VMEM((nh * tq, kl), F32)],
        compiler_params=_cparams("parallel", "arbitrary"),
        name="attn_prompt",
    )(q, kbf)


def _attn_sample_body(scale, kl, npg, pt_ref, q_ref, kself_ref, *rest):
    pages, (o_ref, m_ref, l_ref, acc_ref) = rest[:npg], rest[npg:]
    j = pl.program_id(1)
    q = q_ref[0]

    @pl.when(j == 0)
    def _():
        ks = kself_ref[0]
        m_ref[...] = jnp.sum(q.astype(F32) * ks.astype(F32), axis=-1, keepdims=True) * scale
        l_ref[...] = jnp.ones_like(l_ref)
        acc_ref[...] = jnp.broadcast_to(ks[:, :kl].astype(F32), acc_ref.shape)

    ks = [pg[0].astype(BF16) for pg in pages]
    s = jnp.concatenate([lax.dot_general(q, k, (((1,), (1,)), ((), ())), preferred_element_type=F32) for k in ks],
                        axis=1) * scale
    m_old = m_ref[...]
    m_new = jnp.maximum(m_old, jnp.max(s, axis=-1, keepdims=True))
    alpha = jnp.exp(m_old - m_new)
    p = jnp.exp(s - m_new)
    l_ref[...] = alpha * l_ref[...] + jnp.sum(p, axis=-1, keepdims=True)
    pb = p.astype(BF16)
    ps = ks[0].shape[0]
    pv = jnp.dot(pb[:, :ps], ks[0][:, :kl], preferred_element_type=F32)
    for n in range(1, npg):
        pv = pv + jnp.dot(pb[:, n * ps:(n + 1) * ps], ks[n][:, :kl], preferred_element_type=F32)
    acc_ref[...] = alpha * acc_ref[...] + pv
    m_ref[...] = m_new

    @pl.when(j == pl.num_programs(1) - 1)
    def _():
        o_ref[0] = (acc_ref[...] / l_ref[...]).astype(o_ref.dtype)


def _attn_sample(q, kself, pool, page_table, scale, kl):
    n, nh, dk = q.shape
    _, ps, _ = pool.shape
    npages = page_table.shape[1]
    npg = min(PAGES_PER_STEP, npages)

    def page_spec(kk):
        return pl.BlockSpec((1, ps, dk), lambda bi, j, pt: (pt[bi, j * npg + kk], 0, 0))

    grid_spec = pltpu.PrefetchScalarGridSpec(
        num_scalar_prefetch=1,
        grid=(n, npages // npg),
        in_specs=[pl.BlockSpec((1, nh, dk), lambda bi, j, pt: (bi, 0, 0)),
                  pl.BlockSpec((1, 1, dk), lambda bi, j, pt: (bi, 0, 0))] + [page_spec(kk) for kk in range(npg)],
        out_specs=pl.BlockSpec((1, nh, kl), lambda bi, j, pt: (bi, 0, 0)),
        scratch_shapes=[pltpu.VMEM((nh, 1), F32), pltpu.VMEM((nh, 1), F32), pltpu.VMEM((nh, kl), F32)],
    )
    return pl.pallas_call(
        functools.partial(_attn_sample_body, scale, kl, npg),
        grid_spec=grid_spec,
        out_shape=jax.ShapeDtypeStruct((n, nh, kl), BF16),
        compiler_params=_cparams("parallel", "arbitrary"),
        name="attn_sample",
    )(page_table, q, kself, *([pool] * npg))


def _memkv_body(m_ref, g_ref, w_ref, o_ref):
    o_ref[...] = jnp.dot(_rms(m_ref[...], g_ref[...]).astype(BF16), w_ref[...], preferred_element_type=F32)


def _memkv(mem, g, wkv):
    n, d = mem.shape
    full = lambda s: pl.BlockSpec(s, lambda: (0,) * len(s))
    return pl.pallas_call(
        _memkv_body,
        in_specs=[full((n, d)), full((1, d)), full(wkv.shape)],
        out_specs=full((n, wkv.shape[1])),
        out_shape=jax.ShapeDtypeStruct((n, wkv.shape[1]), F32),
        name="mem_kv",
    )(mem, g, wkv)


def _mem_heads(q, k, v, nh, scale):
    hd = q.shape[1] // nh
    outs = []
    for h in range(nh):
        sl = slice(h * hd, (h + 1) * hd)
        s = _mm_nt(q[:, sl], k[:, sl]) * scale
        p = jnp.exp(s - jnp.max(s, axis=-1, keepdims=True))
        p = p / jnp.sum(p, axis=-1, keepdims=True)
        outs.append(_mm(p, v[:, sl]))
    return jnp.concatenate(outs, axis=1)


def _memattn_prompt_body(nh, scale, q_ref, k_ref, v_ref, o_ref):
    o_ref[...] = _mem_heads(q_ref[...], k_ref[0], v_ref[0], nh, scale).astype(o_ref.dtype)


def _memattn_prompt(q, mk, mv, nh, scale, tm):
    n, md = q.shape
    b, m, _ = mk.shape
    nb = (n // b) // tm
    kv = pl.BlockSpec((1, m, md), lambda i: (i // nb, 0, 0))
    return pl.pallas_call(
        functools.partial(_memattn_prompt_body, nh, scale),
        grid=(n // tm,),
        in_specs=[pl.BlockSpec((tm, md), lambda i: (i, 0)), kv, kv],
        out_specs=pl.BlockSpec((tm, md), lambda i: (i, 0)),
        out_shape=jax.ShapeDtypeStruct((n, md), BF16),
        compiler_params=_cparams("parallel"),
        name="memattn_prompt",
    )(q, mk, mv)


def _memattn_sample_body(nh, scale, q_ref, k_ref, v_ref, o_ref):
    for g in range(k_ref.shape[0]):
        o_ref[g] = _mem_heads(q_ref[g], k_ref[g], v_ref[g], nh, scale).astype(o_ref.dtype)


def _memattn_sample(q, mk, mv, nh, scale, gr):
    n, md = q.shape
    m = mk.shape[1]
    kv = pl.BlockSpec((gr, m, md), lambda i: (i, 0, 0))
    qs = pl.BlockSpec((gr, 1, md), lambda i: (i, 0, 0))
    return pl.pallas_call(
        functools.partial(_memattn_sample_body, nh, scale),
        grid=(n // gr,),
        in_specs=[qs, kv, kv],
        out_specs=qs,
        out_shape=jax.ShapeDtypeStruct((n, 1, md), BF16),
        compiler_params=_cparams("parallel"),
        name="memattn_sample",
    )(q.reshape(n, 1, md), mk, mv).reshape(n, md)


def _merge_body(nh, h_ref, og_ref, ctx_ref, om_ref, gates_ref, wo_ref, wuv_ref, mwo_ref, memwo_ref, wout_ref, post_ref,
                o_ref):
    d = h_ref.shape[1]
    o_rwkv = jnp.dot(og_ref[...], wo_ref[...], preferred_element_type=F32)
    vs = []
    for pr in range(nh // 2):
        vp = (jnp.dot(ctx_ref[0, 2 * pr], wuv_ref[2 * pr], preferred_element_type=F32)
              + jnp.dot(ctx_ref[0, 2 * pr + 1], wuv_ref[2 * pr + 1], preferred_element_type=F32))
        vs.append(vp.astype(BF16))
    o_mla = jnp.dot(jnp.concatenate(vs, axis=1), mwo_ref[...], preferred_element_type=F32)
    o_mem = jnp.dot(om_ref[...], memwo_ref[...], preferred_element_type=F32)
    merged = gates_ref[:, 0:d] * o_rwkv + gates_ref[:, d:2 * d] * o_mla + gates_ref[:, 2 * d:3 * d] * o_mem
    y = jnp.dot(merged.astype(BF16), wout_ref[...], preferred_element_type=F32)
    o_ref[...] = h_ref[...] + _rms(y, post_ref[...])


def _merge(h, nbatch, og, ctx, om, gates, wo, wuv, mwo, memwo, wout, post, nh, tm):
    n, d = h.shape
    nb = (n // nbatch) // tm
    row = lambda i: (i, 0)
    kl = ctx.shape[-1]
    return pl.pallas_call(
        functools.partial(_merge_body, nh),
        grid=(n // tm,),
        in_specs=[pl.BlockSpec((tm, d), row), pl.BlockSpec((tm, og.shape[1]), row),
                  pl.BlockSpec((1, nh, tm, kl), lambda i: (i // nb, 0, i % nb, 0)),
                  pl.BlockSpec((tm, om.shape[1]), row), pl.BlockSpec((tm, 3 * d), row),
                  _resident(wo.shape), _resident(wuv.shape), _resident(mwo.shape), _resident(memwo.shape),
                  _resident(wout.shape), _resident((1, d))],
        out_specs=pl.BlockSpec((tm, d), row),
        out_shape=jax.ShapeDtypeStruct((n, d), F32),
        compiler_params=_cparams("parallel"),
        name="merge",
    )(h, og, ctx, om, gates, wo, wuv, mwo, memwo, wout, post)


def _rope_tables(pos, rope, nh):
    half = rope // 2
    freqs = ROPE_BASE ** (-jnp.arange(half, dtype=F32) / half)
    ang = pos.astype(F32)[:, None] * freqs
    cos, sin = jnp.cos(ang), jnp.sin(ang)
    return jnp.tile(jnp.concatenate([cos, cos], axis=1), (1, nh)), jnp.tile(jnp.concatenate([-sin, sin], axis=1), (1, nh))


def _prep_weights(W, d):
    nh, hd = W["rwkv_r_k"].shape
    rd = nh * hd
    lora = W["rwkv_w2"].shape[0]
    glora = W["rwkv_g2"].shape[0]
    rp = 3 * rd + 2 * lora + glora
    ql = W["mla_q_norm"].shape[0]
    kl, mh, vh = W["mla_w_uv"].shape
    nope = W["mla_w_uk"].shape[2]
    rope = W["mla_w_qb"].shape[1] // mh - nope
    md = W["mem_w_k"].shape[1]
    half = rope // 2
    row = lambda x: x.reshape(1, -1)
    w_in = W["w_in"]
    o_cq, o_kv, o_pe, o_mem, o_g = rp, rp + ql, rp + ql + kl, rp + ql + kl + rope, rp + ql + kl + rope + md
    cols = [w_in[:, :o_pe], w_in[:, o_mem:], w_in[:, o_pe:o_mem],
            w_in[:, o_pe + half:o_mem], w_in[:, o_pe:o_pe + half]]
    width = sum(c.shape[1] for c in cols)
    pad = (-width) % LANES
    win = jnp.concatenate(cols + [jnp.zeros((d, pad), F32)], axis=1).astype(BF16)
    wqb = W["mla_w_qb"].reshape(ql, mh, nope + rope)
    wqb = jnp.concatenate([wqb[:, :, :nope].reshape(ql, mh * nope),
                           wqb[:, :, nope:].reshape(ql, mh * rope),
                           jnp.concatenate([wqb[:, :, nope + half:], wqb[:, :, nope:nope + half]], axis=2).reshape(ql, mh * rope)],
                          axis=1).astype(BF16)
    ukt = jnp.transpose(W["mla_w_uk"], (1, 2, 0))
    z = jnp.zeros_like(ukt[0])
    wuk = jnp.stack([jnp.concatenate([jnp.concatenate([ukt[2 * p], z], axis=1),
                                      jnp.concatenate([z, ukt[2 * p + 1]], axis=1)], axis=0)
                     for p in range(mh // 2)]).astype(BF16)
    uv = jnp.transpose(W["mla_w_uv"], (1, 0, 2))
    zv = jnp.zeros_like(uv[0])
    wuv = jnp.stack([jnp.concatenate([uv[h], zv] if h % 2 == 0 else [zv, uv[h]], axis=1)
                     for h in range(mh)]).astype(BF16)
    zl = jnp.zeros((lora, rd), F32)
    w2a2 = jnp.concatenate([jnp.concatenate([W["rwkv_w2"], zl], axis=1),
                            jnp.concatenate([zl, W["rwkv_a2"]], axis=1)], axis=0).astype(BF16)
    hid = jnp.arange(rd) // hd
    rw = dict(mu=row(W["rwkv_mu"]), w0=row(W["rwkv_w0"]), w2a2=w2a2, a0=row(W["rwkv_a0"]), g2=W["rwkv_g2"].astype(BF16),
              k_k=row(W["rwkv_k_k"]), k_a=row(W["rwkv_k_a"]), r_k=row(W["rwkv_r_k"]), lnx_g=row(W["rwkv_lnx_g"]),
              lnx_b=row(W["rwkv_lnx_b"]), e=(hid[:, None] == hid[None, :]).astype(BF16))
    dims = (rp, ql, kl, rope, md, 3 * d, mh, nope)
    return dict(
        dims=dims, nh=nh, hd=hd, lora=lora, rw=rw, win=win, wqb=wqb, wuk=wuk, wuv=wuv,
        ffn1=(row(W["ffn1_pre"]), row(W["ffn1_post"]), W["ffn1_gate"].astype(BF16), W["ffn1_up"].astype(BF16),
              W["ffn1_down"].astype(BF16)),
        ffn2=(row(W["ffn2_pre"]), row(W["ffn2_post"]), W["ffn2_gate"].astype(BF16), W["ffn2_up"].astype(BF16),
              W["ffn2_down"].astype(BF16)),
        mix_pre=row(W["mix_pre"]), mix_post=row(W["mix_post"]), q_norm=row(W["mla_q_norm"]), kv_norm=row(W["mla_kv_norm"]),
        mem_norm=row(W["mem_norm"]), mem_wkv=jnp.concatenate([W["mem_w_k"], W["mem_w_v"]], axis=1).astype(BF16),
        rwkv_wo=W["rwkv_w_o"].astype(BF16), mla_wo=W["mla_w_o"].astype(BF16), mem_wo=W["mem_w_o"].astype(BF16),
        w_out=W["w_out"].astype(BF16), mla_scale=float(nope + rope) ** -0.5, mem_heads=None,
    )


def _tile(n, pref):
    t = min(pref, n)
    assert n % t == 0, (n, t)
    return t


def kernel(x_prompt, x_sample, cache_mla, state_rwkv, state_shift, cache_mem_k, cache_mem_v, page_table, mem_prompt, ffn1_pre, ffn1_post, ffn1_gate, ffn1_up, ffn1_down, mix_pre, mix_post, w_in, rwkv_mu, rwkv_w0, rwkv_w2, rwkv_a0, rwkv_a2, rwkv_g2, rwkv_k_k, rwkv_k_a, rwkv_r_k, rwkv_lnx_g, rwkv_lnx_b, rwkv_w_o, mla_q_norm, mla_w_qb, mla_kv_norm, mla_w_uk, mla_w_uv, mla_w_o, mem_norm, mem_w_k, mem_w_v, mem_w_o, w_out, ffn2_pre, ffn2_post, ffn2_gate, ffn2_up, ffn2_down):
    names = ("ffn1_pre", "ffn1_post", "ffn1_gate", "ffn1_up", "ffn1_down", "mix_pre", "mix_post", "w_in",
             "rwkv_mu", "rwkv_w0", "rwkv_w2", "rwkv_a0", "rwkv_a2", "rwkv_g2", "rwkv_k_k", "rwkv_k_a", "rwkv_r_k",
             "rwkv_lnx_g", "rwkv_lnx_b", "rwkv_w_o", "mla_q_norm", "mla_w_qb", "mla_kv_norm", "mla_w_uk", "mla_w_uv",
             "mla_w_o", "mem_norm", "mem_w_k", "mem_w_v", "mem_w_o", "w_out", "ffn2_pre", "ffn2_post", "ffn2_gate",
             "ffn2_up", "ffn2_down")
    stacked = (ffn1_pre, ffn1_post, ffn1_gate, ffn1_up, ffn1_down, mix_pre, mix_post, w_in,
               rwkv_mu, rwkv_w0, rwkv_w2, rwkv_a0, rwkv_a2, rwkv_g2, rwkv_k_k, rwkv_k_a, rwkv_r_k,
               rwkv_lnx_g, rwkv_lnx_b, rwkv_w_o, mla_q_norm, mla_w_qb, mla_kv_norm, mla_w_uk, mla_w_uv,
               mla_w_o, mem_norm, mem_w_k, mem_w_v, mem_w_o, w_out, ffn2_pre, ffn2_post, ffn2_gate,
               ffn2_up, ffn2_down)
    B, S, D = x_prompt.shape
    DB, T, _ = x_sample.shape
    assert T == 1, "decode groups carry one new token per request"
    depth = ffn1_pre.shape[0]
    page = cache_mla.shape[2]
    past_len = page_table.shape[1] * page
    mem_tokens, mem_heads, mem_hd = cache_mem_k.shape[2:]
    mem_scale = float(mem_hd) ** -0.5

    xp = x_prompt.reshape(B * S, D)
    xs = x_sample.reshape(DB, D)
    outs = [[] for _ in range(8)]
    for l in range(depth):
        P = _prep_weights({n: w[l] for n, w in zip(names, stacked)}, D)
        rp, ql, kl, rope, md, gd, mh, nope = P["dims"]
        nh, hd, lora, rw = P["nh"], P["hd"], P["lora"], P["rw"]
        cos_p, sin_p = _rope_tables(jnp.arange(S), rope, mh)
        cos_s, sin_s = _rope_tables(jnp.full((DB,), past_len), rope, mh)
        tm_p, tm_s = _tile(S, 256), DB

        mkv = _memkv(mem_prompt.reshape(B * mem_tokens, D), P["mem_norm"], P["mem_wkv"])
        mk_p, mv_p = mkv[:, :md].reshape(B, mem_tokens, md), mkv[:, md:].reshape(B, mem_tokens, md)

        h = _ffn(xp, *P["ffn1"], tm_p)
        prw, q, rows, kbf, qmem, gates = _inproj(h, B, P["mix_pre"], P["win"], P["q_norm"], P["wqb"], P["wuk"],
                                                 P["kv_norm"], cos_p, sin_p, P["dims"], tm_p)
        og, wkv_p = _rwkv_prompt(prw.reshape(B, S, rp), rw, nh, hd, lora)
        ctx = _attn_prompt(q, kbf.reshape(B, S, kl + rope), P["mla_scale"], kl)
        om = _memattn_prompt(qmem, mk_p, mv_p, mem_heads, mem_scale, tm_p)
        h = _merge(h, B, og.reshape(B * S, nh * hd), ctx, om, gates, P["rwkv_wo"], P["wuv"], P["mla_wo"], P["mem_wo"],
                   P["w_out"], P["mix_post"], mh, tm_p)
        xp = _ffn(h, *P["ffn2"], tm_p)
        rows_p, shift_p = rows.reshape(B, S, kl + rope), prw.reshape(B, S, rp)[:, -1]

        h = _ffn(xs, *P["ffn1"], tm_s)
        prw, q, rows, kbf, qmem, gates = _inproj(h, 1, P["mix_pre"], P["win"], P["q_norm"], P["wqb"], P["wuk"],
                                                 P["kv_norm"], cos_s, sin_s, P["dims"], tm_s)
        og, wkv_s = _rwkv_sample(prw, state_shift[l], state_rwkv[l], rw, nh, hd, lora, _tile(DB, 8))
        ctx = _attn_sample(jnp.swapaxes(q[0], 0, 1), kbf.reshape(DB, 1, kl + rope), cache_mla[l], page_table,
                           P["mla_scale"], kl)
        om = _memattn_sample(qmem, cache_mem_k[l].reshape(DB, mem_tokens, md), cache_mem_v[l].reshape(DB, mem_tokens, md),
                             mem_heads, mem_scale, _tile(DB, 4))
        h = _merge(h, 1, og, jnp.swapaxes(ctx, 0, 1)[None], om, gates, P["rwkv_wo"], P["wuv"], P["mla_wo"], P["mem_wo"],
                   P["w_out"], P["mix_post"], mh, tm_s)
        xs = _ffn(h, *P["ffn2"], tm_s)

        for lst, val in zip(outs, (rows_p, rows.reshape(DB, T, kl + rope), wkv_p, wkv_s, shift_p, prw,
                                   mk_p.reshape(B, mem_tokens, mem_heads, mem_hd),
                                   mv_p.reshape(B, mem_tokens, mem_heads, mem_hd))):
            lst.append(val)
    return (xp.reshape(B, S, D), xs.reshape(DB, T, D)) + tuple(jnp.stack(o) for o in outs)
```

```python
import functools

import jax
import jax.numpy as jnp
from jax import lax
from jax.experimental import pallas as pl
from jax.experimental.pallas import tpu as pltpu

F32, BF16 = jnp.float32, jnp.bfloat16
RMS_EPS = 1e-6
LNX_EPS = 64e-5
ROPE_BASE = 10000.0
LANES = 128
VMEM_LIMIT = 52 * 1024 * 1024
RWKV_CHUNK = 64
RWKV_CHUNKS_PER_STEP = 2
ATT_TQ = 128
ATT_TK = 1024
ATT_ROW_GROUP = 256
PAGES_PER_STEP = 32
LOG2E = 1.4426950408889634


def _cparams(*sem):
    return pltpu.CompilerParams(dimension_semantics=sem, vmem_limit_bytes=VMEM_LIMIT)


def _resident(shape):
    nd = len(shape)
    return pl.BlockSpec(shape, lambda *_: (0,) * nd, pipeline_mode=pl.Buffered(1))


def _rms(x, g):
    return x * lax.rsqrt(jnp.mean(x * x, axis=-1, keepdims=True) + RMS_EPS) * g


def _sigmoid(x):
    return 1.0 / (1.0 + jnp.exp(-x))


def _mm(a, b):
    return jnp.dot(a.astype(BF16), b.astype(BF16), preferred_element_type=F32)


def _mm_nt(a, b):
    return lax.dot_general(a.astype(BF16), b.astype(BF16), (((1,), (1,)), ((), ())), preferred_element_type=F32)


def _ffn_body(x_ref, pre_ref, post_ref, wg_ref, wu_ref, wd_ref, o_ref):
    x = x_ref[...]
    h = _rms(x, pre_ref[...]).astype(BF16)
    g = jnp.dot(h, wg_ref[...], preferred_element_type=F32)
    u = jnp.dot(h, wu_ref[...], preferred_element_type=F32)
    act = (g * _sigmoid(g)) * u
    y = jnp.dot(act.astype(BF16), wd_ref[...], preferred_element_type=F32)
    o_ref[...] = x + 0.5 * _rms(y, post_ref[...])


def _ffn(x, pre, post, wg, wu, wd, tm):
    n, d = x.shape
    f = wg.shape[1]
    return pl.pallas_call(
        _ffn_body,
        grid=(n // tm,),
        in_specs=[pl.BlockSpec((tm, d), lambda i: (i, 0)), _resident((1, d)), _resident((1, d)),
                  _resident((d, f)), _resident((d, f)), _resident((f, d))],
        out_specs=pl.BlockSpec((tm, d), lambda i: (i, 0)),
        out_shape=jax.ShapeDtypeStruct((n, d), F32),
        compiler_params=_cparams("parallel"),
        name="ffn",
    )(x, pre, post, wg, wu, wd)


def _inproj_body(dims, qscale, h_ref, pre_ref, win_ref, qn_ref, wqb_ref, wuk_ref, kvn_ref, cos_ref, sin_ref,
                 prw_ref, q_ref, rows_ref, kbf_ref, qmem_ref, gates_ref):
    rp, ql, kl, rope, md, gd, nh, nope = dims
    u = _rms(h_ref[...], pre_ref[...]).astype(BF16)
    p = jnp.dot(u, win_ref[...], preferred_element_type=F32)
    o = 0
    prw_ref[...] = p[:, o:o + rp]; o += rp
    cq = p[:, o:o + ql]; o += ql
    ckv = p[:, o:o + kl]; o += kl
    qmem_ref[...] = p[:, o:o + md].astype(BF16); o += md
    gates_ref[...] = _sigmoid(p[:, o:o + gd]); o += gd
    kpe = p[:, o:o + rope]; o += rope
    kpe_sw = p[:, o:o + rope]
    cos = cos_ref[...]
    sin = sin_ref[...]
    q = jnp.dot(_rms(cq, qn_ref[...]).astype(BF16), wqb_ref[...], preferred_element_type=F32)
    nn = nh * nope
    nr = nh * rope
    qpe = ((q[:, nn:nn + nr] * cos + q[:, nn + nr:nn + 2 * nr] * sin) * qscale).astype(BF16)
    qn = q[:, :nn].astype(BF16)
    for pr in range(nh // 2):
        qlat = (jnp.dot(qn[:, LANES * pr:LANES * (pr + 1)], wuk_ref[pr], preferred_element_type=F32) * qscale).astype(BF16)
        for e in range(2):
            hh = 2 * pr + e
            q_ref[0, hh, :, 0:kl] = qlat[:, kl * e:kl * (e + 1)]
            q_ref[0, hh, :, kl:kl + rope] = qpe[:, rope * hh:rope * (hh + 1)]
    ckvn = _rms(ckv, kvn_ref[...])
    kper = kpe * cos[:, :rope] + kpe_sw * sin[:, :rope]
    rows_ref[:, 0:kl] = ckvn
    rows_ref[:, kl:kl + rope] = kper
    kbf_ref[:, 0:kl] = ckvn.astype(BF16)
    kbf_ref[:, kl:kl + rope] = kper.astype(BF16)


def _inproj(h, nbatch, pre, win, qn, wqb, wuk, kvn, cos, sin, dims, qscale, tm):
    n, d = h.shape
    rp, ql, kl, rope, md, gd, nh, nope = dims
    t = n // nbatch
    nb = t // tm
    cw = win.shape[1]
    row = lambda i: (i, 0)
    tab = lambda i: (i % nb, 0)
    return pl.pallas_call(
        functools.partial(_inproj_body, dims, qscale),
        grid=(n // tm,),
        in_specs=[pl.BlockSpec((tm, d), row), _resident((1, d)), _resident((d, cw)), _resident((1, ql)),
                  _resident(wqb.shape), _resident(wuk.shape), _resident((1, kl)),
                  pl.BlockSpec((tm, nh * rope), tab), pl.BlockSpec((tm, nh * rope), tab)],
        out_specs=[pl.BlockSpec((tm, rp), row),
                   pl.BlockSpec((1, nh, tm, kl + rope), lambda i: (i // nb, 0, i % nb, 0)),
                   pl.BlockSpec((tm, kl + rope), row), pl.BlockSpec((tm, kl + rope), row),
                   pl.BlockSpec((tm, md), row), pl.BlockSpec((tm, gd), row)],
        out_shape=[jax.ShapeDtypeStruct((n, rp), F32),
                   jax.ShapeDtypeStruct((nbatch, nh, t, kl + rope), BF16),
                   jax.ShapeDtypeStruct((n, kl + rope), F32),
                   jax.ShapeDtypeStruct((n, kl + rope), BF16),
                   jax.ShapeDtypeStruct((n, md), BF16),
                   jax.ShapeDtypeStruct((n, gd), F32)],
        compiler_params=_cparams("parallel"),
        name="inproj",
    )(h, pre, win, qn, wqb, wuk, kvn, cos, sin)


def _segsum(x, e):
    hi = x.astype(BF16)
    lo = (x - hi.astype(F32)).astype(BF16)
    return jnp.dot(hi, e, preferred_element_type=F32) + jnp.dot(lo, e, preferred_element_type=F32)


def _rwkv_prep(p, prev, mu, w0, w2a2, a0, g2, k_k, k_a, e, rd, lora):
    ps = p + (prev - p) * mu
    r = ps[:, 0:rd]
    k = ps[:, rd:2 * rd]
    v = ps[:, 2 * rd:3 * rd]
    wa = ps[:, 3 * rd:3 * rd + 2 * lora]
    gl = ps[:, 3 * rd + 2 * lora:]
    lane = lax.broadcasted_iota(jnp.int32, wa.shape, 1)
    wa = jnp.where(lane < lora, jnp.tanh(wa), wa)
    wa2 = _mm(wa, w2a2)
    x = -(w0 + wa2[:, :rd])
    softplus = jnp.maximum(x, 0.0) + jnp.log(1.0 + jnp.exp(-jnp.abs(x)))
    logdec = -jnp.exp(-softplus - 0.5)
    a = _sigmoid(a0 + wa2[:, rd:])
    g = _mm(_sigmoid(gl), g2)
    kk = k * k_k
    kk = kk / jnp.maximum(jnp.sqrt(_segsum(kk * kk, e)), 1e-12)
    k = k * (1.0 + (a - 1.0) * k_a)
    return r, k, v, logdec, -kk, kk * a, g


def _rwkv_post(o, r, k, v, g, r_k, lng, lnb, e, hd):
    mean = _segsum(o, e) * (1.0 / hd)
    oc = o - mean
    var = _segsum(oc * oc, e) * (1.0 / hd)
    o = oc * lax.rsqrt(var + LNX_EPS) * lng + lnb
    bonus = _segsum(r * k * r_k, e) * v
    return (o + bonus) * g


def _pair_rows(y):
    lo = (lax.broadcasted_iota(jnp.int32, y.shape, 1) % LANES) < (LANES // 2)
    z = jnp.zeros_like(y)
    return jnp.concatenate([jnp.where(lo, y, z), jnp.where(lo, z, y)], axis=0)


def _rwkv_chunk_local(r, k, v, ld, cum, a, b, strict, incl):
    L = r.shape[0]
    cum_l = cum[L - 1:L, :]
    e_neg = jnp.exp(-cum)
    e_pos = jnp.exp(cum)
    e_exc = jnp.exp(cum - ld)
    e_end = jnp.exp(cum_l - cum)
    kt, bt = (k * e_neg).astype(BF16), (b * e_neg).astype(BF16)
    at, rt = a * e_exc, (r * e_pos).astype(BF16)
    vb = _pair_rows(v.astype(BF16))
    mm = _mm_nt(jnp.concatenate([at.astype(BF16), rt], axis=0),
                jnp.concatenate([_pair_rows(bt), _pair_rows(kt)], axis=0))
    m_ba = jnp.where(strict, mm[:L, :2 * L], 0.0)
    m_ka = jnp.where(strict, mm[:L, 2 * L:], 0.0)
    m_r = jnp.concatenate([jnp.where(incl, mm[L:, :2 * L], 0.0), jnp.where(incl, mm[L:, 2 * L:], 0.0)],
                          axis=1).astype(BF16)
    x = jnp.concatenate([at, _mm(m_ka, vb)], axis=1)
    pw = m_ba
    span = 1
    while span < L:
        x = x + _mm(pw, _pair_rows(x.astype(BF16)))
        span *= 2
        if span < L:
            pw = _mm(pw, _pair_rows(pw.astype(BF16)))
    bkh = jnp.concatenate([b * e_end, k * e_end], axis=0).astype(BF16)
    return dict(w1=x[:, :LANES].astype(BF16), uloc=x[:, LANES:], rt=rt, m_r=m_r, vb=vb, v=v, bkh=bkh, dl=jnp.exp(cum_l))


def _rwkv_chunk_apply(c, s, diag):
    sb = s.astype(BF16)
    ur = _mm_nt(c["w1"], sb) + c["uloc"]
    o = _mm_nt(c["rt"], sb) + _mm(c["m_r"], jnp.concatenate([_pair_rows(ur.astype(BF16)), c["vb"]], axis=0))
    upd = _mm(jnp.concatenate([ur, c["v"]], axis=0).T, c["bkh"])
    return o, s * c["dl"] + jnp.where(diag, upd, 0.0)


def _rwkv_prompt_body(rd, lora, hd, L, p_ref, mu_ref, w0_ref, w2a2_ref, a0_ref, g2_ref, kk_ref, ka_ref, rk_ref,
                      lng_ref, lnb_ref, e_ref, og_ref, st_ref, prev_ref, s_ref):
    step = pl.program_id(0)
    nb, rows, _ = p_ref.shape
    npair = rd // LANES

    @pl.when(step == 0)
    def _():
        prev_ref[...] = jnp.zeros_like(prev_ref)
        s_ref[...] = jnp.zeros_like(s_ref)

    e = e_ref[...]
    ti = lax.broadcasted_iota(jnp.int32, (L, 2 * L), 0)
    si = lax.broadcasted_iota(jnp.int32, (L, 2 * L), 1) % L
    strict, incl = si < ti, si <= ti
    half = LANES // 2
    diag = ((lax.broadcasted_iota(jnp.int32, (LANES, LANES), 0) < half)
            == (lax.broadcasted_iota(jnp.int32, (LANES, LANES), 1) < half))
    tr = lax.broadcasted_iota(jnp.int32, (rows, rows), 0)
    tc = lax.broadcasted_iota(jnp.int32, (rows, rows), 1)
    tri = ((tc <= tr) & (tc // L == tr // L)).astype(BF16)
    rowi = lax.broadcasted_iota(jnp.int32, (rows, p_ref.shape[2]), 0)

    local, vecs = {}, []
    for bi in range(nb):
        p = p_ref[bi]
        prev = jnp.where(rowi == 0, prev_ref[bi], pltpu.roll(p, 1, axis=0))
        prev_ref[bi] = p[rows - 1:rows, :]
        r, k, v, ld, a, b, g = _rwkv_prep(p, prev, mu_ref[...], w0_ref[...], w2a2_ref[...], a0_ref[...], g2_ref[...],
                                          kk_ref[...], ka_ref[...], e, rd, lora)
        hi = ld.astype(BF16)
        r1 = ld - hi.astype(F32)
        mid = r1.astype(BF16)
        lo = (r1 - mid.astype(F32)).astype(BF16)
        cum = (jnp.dot(tri, hi, preferred_element_type=F32) + jnp.dot(tri, mid, preferred_element_type=F32)
               + jnp.dot(tri, lo, preferred_element_type=F32))
        vecs.append((r, k, v, g))
        for cc in range(rows // L):
            for pr in range(npair):
                sl = (slice(L * cc, L * (cc + 1)), slice(LANES * pr, LANES * (pr + 1)))
                local[bi, cc, pr] = _rwkv_chunk_local(r[sl], k[sl], v[sl], ld[sl], cum[sl], a[sl], b[sl], strict, incl)
    for bi in range(nb):
        cols = []
        for pr in range(npair):
            s = s_ref[bi * npair + pr]
            outs = []
            for cc in range(rows // L):
                o, s = _rwkv_chunk_apply(local[bi, cc, pr], s, diag)
                outs.append(o)
            s_ref[bi * npair + pr] = s
            cols.append(jnp.concatenate(outs, axis=0))
        r, k, v, g = vecs[bi]
        og_ref[bi] = _rwkv_post(jnp.concatenate(cols, axis=1), r, k, v, g, rk_ref[...], lng_ref[...], lnb_ref[...], e,
                                hd).astype(BF16)

    @pl.when(step == pl.num_programs(0) - 1)
    def _():
        for bi in range(nb):
            for pr in range(npair):
                s = s_ref[bi * npair + pr]
                st_ref[bi, 2 * pr] = s[:hd, :hd]
                st_ref[bi, 2 * pr + 1] = s[hd:, hd:]


def _rwkv_prompt(prw, rw, nh, hd, lora):
    b, t, pw = prw.shape
    rd = nh * hd
    rows = _tile(t, RWKV_CHUNK * RWKV_CHUNKS_PER_STEP)
    names = ("mu", "w0", "w2a2", "a0", "g2", "k_k", "k_a", "r_k", "lnx_g", "lnx_b", "e")
    return pl.pallas_call(
        functools.partial(_rwkv_prompt_body, rd, lora, hd, RWKV_CHUNK),
        grid=(t // rows,),
        in_specs=[pl.BlockSpec((b, rows, pw), lambda c: (0, c, 0))] + [_resident(rw[k].shape) for k in names],
        out_specs=[pl.BlockSpec((b, rows, rd), lambda c: (0, c, 0)),
                   pl.BlockSpec((b, nh, hd, hd), lambda c: (0, 0, 0, 0))],
        out_shape=[jax.ShapeDtypeStruct((b, t, rd), BF16), jax.ShapeDtypeStruct((b, nh, hd, hd), F32)],
        scratch_shapes=[pltpu.VMEM((b, 1, pw), F32), pltpu.VMEM((b * (rd // LANES), LANES, LANES), F32)],
        compiler_params=_cparams("arbitrary"),
        name="rwkv_prompt",
    )(prw, *[rw[k] for k in names])


def _rwkv_prep_body(rd, lora, p_ref, prev_ref, mu_ref, w0_ref, w2a2_ref, a0_ref, g2_ref, kk_ref, ka_ref, e_ref,
                    r_ref, k_ref, v_ref, w_ref, a_ref, b_ref, g_ref):
    r, k, v, ld, a, b, g = _rwkv_prep(p_ref[...], prev_ref[...], mu_ref[...], w0_ref[...], w2a2_ref[...], a0_ref[...],
                                      g2_ref[...], kk_ref[...], ka_ref[...], e_ref[...], rd, lora)
    r_ref[...] = r
    k_ref[...] = k
    v_ref[...] = v
    w_ref[...] = jnp.exp(ld)
    a_ref[...] = a
    b_ref[...] = b
    g_ref[...] = g


def _rwkv_step_body(nh, s_ref, r_ref, k_ref, vt_ref, w_ref, a_ref, b_ref, so_ref, o_ref):
    hd = s_ref.shape[-1]
    for g in range(s_ref.shape[0]):
        for h in range(nh):
            s = s_ref[g, h]
            row = lambda ref: ref[g, h:h + 1, :]
            sa = jnp.sum(s * row(a_ref), axis=-1, keepdims=True)
            s = s * row(w_ref) + sa * row(b_ref) + vt_ref[g, :, h:h + 1] * row(k_ref)
            so_ref[g, h] = s
            o_ref[g, h * hd:(h + 1) * hd, :] = jnp.sum(s * row(r_ref), axis=-1, keepdims=True)


def _rwkv_post_body(hd, o_ref, r_ref, k_ref, v_ref, g_ref, rk_ref, lng_ref, lnb_ref, e_ref, og_ref):
    og_ref[...] = _rwkv_post(o_ref[...], r_ref[...], k_ref[...], v_ref[...], g_ref[...], rk_ref[...], lng_ref[...],
                             lnb_ref[...], e_ref[...], hd).astype(BF16)


def _rwkv_sample(prw, shift, state, layer, rw, nh, hd, lora, gr):
    n, pw = prw.shape
    rd = nh * hd
    names = ("mu", "w0", "w2a2", "a0", "g2", "k_k", "k_a", "e")
    full = lambda s: pl.BlockSpec(s, lambda: (0,) * len(s))
    vecs = pl.pallas_call(
        functools.partial(_rwkv_prep_body, rd, lora),
        in_specs=[full((n, pw)), full((n, pw))] + [full(rw[k].shape) for k in names],
        out_specs=[full((n, rd))] * 7,
        out_shape=[jax.ShapeDtypeStruct((n, rd), F32)] * 7,
        name="rwkv_prep",
    )(prw, shift, *[rw[k] for k in names])
    r, k, v, w, a, b, g = vecs
    hv = lambda x: x.reshape(n, nh, hd)
    vt = jnp.swapaxes(hv(v), 1, 2)
    vspec = pl.BlockSpec((gr, nh, hd), lambda i: (i, 0, 0))
    sspec = pl.BlockSpec((gr, nh, hd, hd), lambda i: (i, 0, 0, 0))
    s_new, o = pl.pallas_call(
        functools.partial(_rwkv_step_body, nh),
        grid=(n // gr,),
        in_specs=[pl.BlockSpec((None, gr, nh, hd, hd), lambda i: (layer, i, 0, 0, 0)), vspec, vspec,
                  pl.BlockSpec((gr, hd, nh), lambda i: (i, 0, 0)), vspec, vspec, vspec],
        out_specs=[sspec, pl.BlockSpec((gr, rd, 1), lambda i: (i, 0, 0))],
        out_shape=[jax.ShapeDtypeStruct(state.shape[1:], F32), jax.ShapeDtypeStruct((n, rd, 1), F32)],
        compiler_params=_cparams("parallel"),
        name="rwkv_step",
    )(state, hv(r), hv(k), vt, hv(w), hv(a), hv(b))
    pnames = ("r_k", "lnx_g", "lnx_b", "e")
    og = pl.pallas_call(
        functools.partial(_rwkv_post_body, hd),
        in_specs=[full((n, rd))] * 5 + [full(rw[k].shape) for k in pnames],
        out_specs=full((n, rd)),
        out_shape=jax.ShapeDtypeStruct((n, rd), BF16),
        name="rwkv_post",
    )(o.reshape(n, rd), r, k, v, g, *[rw[k] for k in pnames])
    return og, s_new


def _lanes(x, n):
    return x if n == LANES else jnp.concatenate([x] * (n // LANES), axis=1)


def _attn_prompt_body(kl, tk, rg, q_ref, k_ref, o_ref, m_ref, l_ref, acc_ref):
    i = pl.program_id(1)
    nh, tq, dk = q_ref.shape[1:]
    rows = nh * tq
    q = q_ref[0].reshape(rows, dk)
    m_ref[...] = jnp.full_like(m_ref, -jnp.inf)
    l_ref[...] = jnp.zeros_like(l_ref)
    acc_ref[...] = jnp.zeros_like(acc_ref)

    def step(j, masked):
        k = k_ref[0, pl.ds(pl.multiple_of(j * tk, tk), tk), :]
        v = k[:, :kl]
        for g in range(rows // rg):
            r = slice(g * rg, (g + 1) * rg)
            s = lax.dot_general(q[r], k, (((1,), (1,)), ((), ())), preferred_element_type=F32)
            if masked:
                qpos = i * tq + (g * rg + lax.broadcasted_iota(jnp.int32, s.shape, 0)) % tq
                kpos = j * tk + lax.broadcasted_iota(jnp.int32, s.shape, 1)
                s = jnp.where(kpos <= qpos, s, -jnp.inf)
            m_old = m_ref[r]
            m_new = jnp.maximum(m_old, jnp.max(s, axis=-1, keepdims=True))
            alpha = jnp.exp2(m_old - m_new)
            p = jnp.exp2(s - _lanes(m_new, tk))
            l_ref[r] = alpha * l_ref[r] + jnp.sum(p, axis=-1, keepdims=True)
            acc_ref[r] = _lanes(alpha, kl) * acc_ref[r] + jnp.dot(p.astype(BF16), v, preferred_element_type=F32)
            m_ref[r] = m_new

    last = (i * tq) // tk
    lax.fori_loop(0, last, lambda j, c: (step(j, False), c)[1], 0)
    step(last, True)
    o_ref[0] = (acc_ref[...] / _lanes(l_ref[...], kl)).reshape(nh, tq, kl).astype(o_ref.dtype)


def _attn_prompt(q, kbf, kl):
    b, nh, t, dk = q.shape
    tq, tk = _tile(t, ATT_TQ), _tile(t, ATT_TK)
    rows = nh * tq
    rg = _tile(rows, ATT_ROW_GROUP)
    assert tk % tq == 0 and rg % tq == 0
    return pl.pallas_call(
        functools.partial(_attn_prompt_body, kl, tk, rg),
        grid=(b, t // tq),
        in_specs=[pl.BlockSpec((1, nh, tq, dk), lambda bi, i: (bi, 0, i, 0)),
                  pl.BlockSpec((1, t, dk), lambda bi, i: (bi, 0, 0))],
        out_specs=pl.BlockSpec((1, nh, tq, kl), lambda bi, i: (bi, 0, i, 0)),
        out_shape=jax.ShapeDtypeStruct((b, nh, t, kl), BF16),
        scratch_shapes=[pltpu.VMEM((rows, LANES), F32), pltpu.VMEM((rows, LANES), F32), pltpu.VMEM((rows, kl), F32)],
        compiler_params=_cparams("parallel", "arbitrary"),
        name="attn_prompt",
    )(q, kbf)


def _attn_sample_body(kl, npg, pt_ref, q_ref, kself_ref, *rest):
    pages, (o_ref, m_ref, l_ref, acc_ref) = rest[:npg], rest[npg:]
    j = pl.program_id(1)
    q = q_ref[0]

    @pl.when(j == 0)
    def _():
        ks = kself_ref[0]
        m_ref[...] = jnp.sum(q.astype(F32) * ks.astype(F32), axis=-1, keepdims=True)
        l_ref[...] = jnp.ones_like(l_ref)
        acc_ref[...] = jnp.broadcast_to(ks[:, :kl].astype(F32), acc_ref.shape)

    grp = 2 if npg % 2 == 0 else 1
    ks = [jnp.concatenate([pages[n + e][0] for e in range(grp)], axis=0).astype(BF16) for n in range(0, npg, grp)]
    s = jnp.concatenate([lax.dot_general(q, k, (((1,), (1,)), ((), ())), preferred_element_type=F32) for k in ks],
                        axis=1)
    m_old = m_ref[...]
    m_new = jnp.maximum(m_old, jnp.max(s, axis=-1, keepdims=True))
    alpha = jnp.exp2(m_old - m_new)
    p = jnp.exp2(s - m_new)
    l_ref[...] = alpha * l_ref[...] + jnp.sum(p, axis=-1, keepdims=True)
    pb = p.astype(BF16)
    w = ks[0].shape[0]
    pv = jnp.dot(pb[:, :w], ks[0][:, :kl], preferred_element_type=F32)
    for n in range(1, len(ks)):
        pv = pv + jnp.dot(pb[:, n * w:(n + 1) * w], ks[n][:, :kl], preferred_element_type=F32)
    acc_ref[...] = alpha * acc_ref[...] + pv
    m_ref[...] = m_new

    @pl.when(j == pl.num_programs(1) - 1)
    def _():
        o_ref[0] = (acc_ref[...] / l_ref[...]).astype(o_ref.dtype)


def _attn_sample(q, kself, cache, layer, page_table, kl):
    n, nh, dk = q.shape
    ps = cache.shape[2]
    npages = page_table.shape[1]
    npg = _tile(npages, PAGES_PER_STEP)

    def page_spec(kk):
        return pl.BlockSpec((None, 1, ps, dk), lambda bi, j, pt: (layer, pt[bi, j * npg + kk], 0, 0))

    grid_spec = pltpu.PrefetchScalarGridSpec(
        num_scalar_prefetch=1,
        grid=(n, npages // npg),
        in_specs=[pl.BlockSpec((1, nh, dk), lambda bi, j, pt: (bi, 0, 0)),
                  pl.BlockSpec((1, 1, dk), lambda bi, j, pt: (bi, 0, 0))] + [page_spec(kk) for kk in range(npg)],
        out_specs=pl.BlockSpec((1, nh, kl), lambda bi, j, pt: (bi, 0, 0)),
        scratch_shapes=[pltpu.VMEM((nh, 1), F32), pltpu.VMEM((nh, 1), F32), pltpu.VMEM((nh, kl), F32)],
    )
    return pl.pallas_call(
        functools.partial(_attn_sample_body, kl, npg),
        grid_spec=grid_spec,
        out_shape=jax.ShapeDtypeStruct((n, nh, kl), BF16),
        compiler_params=_cparams("parallel", "arbitrary"),
        name="attn_sample",
    )(page_table, q, kself, *([cache] * npg))


def _memkv_body(m_ref, g_ref, w_ref, o_ref):
    o_ref[...] = jnp.dot(_rms(m_ref[...], g_ref[...]).astype(BF16), w_ref[...], preferred_element_type=F32)


def _memkv(mem, g, wkv):
    n, d = mem.shape
    full = lambda s: pl.BlockSpec(s, lambda: (0,) * len(s))
    return pl.pallas_call(
        _memkv_body,
        in_specs=[full((n, d)), full((1, d)), full(wkv.shape)],
        out_specs=full((n, wkv.shape[1])),
        out_shape=jax.ShapeDtypeStruct((n, wkv.shape[1]), F32),
        name="mem_kv",
    )(mem, g, wkv)


def _mem_attend(q, k, v, scale):
    s = _mm_nt(q, k) * scale
    p = jnp.exp(s - jnp.max(s, axis=-1, keepdims=True))
    p = p / jnp.sum(p, axis=-1, keepdims=True)
    return _mm(p, v)


def _memattn_prompt_body(nh, scale, q_ref, k_ref, v_ref, o_ref):
    hd = q_ref.shape[1] // nh
    q, k, v = q_ref[...], k_ref[0], v_ref[0]
    heads = [slice(h * hd, (h + 1) * hd) for h in range(nh)]
    o_ref[...] = jnp.concatenate([_mem_attend(q[:, sl], k[:, sl], v[:, sl], scale) for sl in heads],
                                 axis=1).astype(o_ref.dtype)


def _memattn_prompt(q, mk, mv, nh, scale, tm):
    n, md = q.shape
    b, m, _ = mk.shape
    nb = (n // b) // tm
    kv = pl.BlockSpec((1, m, md), lambda i: (i // nb, 0, 0))
    return pl.pallas_call(
        functools.partial(_memattn_prompt_body, nh, scale),
        grid=(n // tm,),
        in_specs=[pl.BlockSpec((tm, md), lambda i: (i, 0)), kv, kv],
        out_specs=pl.BlockSpec((tm, md), lambda i: (i, 0)),
        out_shape=jax.ShapeDtypeStruct((n, md), BF16),
        compiler_params=_cparams("parallel"),
        name="memattn_prompt",
    )(q, mk, mv)


def _memattn_sample_body(scale, q_ref, k_ref, v_ref, o_ref):
    gr, _, nh, hd = k_ref.shape
    for g in range(gr):
        q = q_ref[g]
        o_ref[g] = jnp.concatenate([_mem_attend(q[:, h * hd:(h + 1) * hd], k_ref[g, :, h, :], v_ref[g, :, h, :], scale)
                                    for h in range(nh)], axis=1).astype(o_ref.dtype)


def _memattn_sample(q, cache_k, cache_v, layer, scale, gr):
    n, md = q.shape
    _, _, m, nh, hd = cache_k.shape
    kv = pl.BlockSpec((None, gr, m, nh, hd), lambda i: (layer, i, 0, 0, 0))
    qs = pl.BlockSpec((gr, 1, md), lambda i: (i, 0, 0))
    return pl.pallas_call(
        functools.partial(_memattn_sample_body, scale),
        grid=(n // gr,),
        in_specs=[qs, kv, kv],
        out_specs=qs,
        out_shape=jax.ShapeDtypeStruct((n, 1, md), BF16),
        compiler_params=_cparams("parallel"),
        name="memattn_sample",
    )(q.reshape(n, 1, md), cache_k, cache_v).reshape(n, md)


def _merge_body(nh, h_ref, og_ref, ctx_ref, om_ref, gates_ref, wo_ref, wuv_ref, mwo_ref, memwo_ref, wout_ref, post_ref,
                o_ref):
    d = h_ref.shape[1]
    o_rwkv = jnp.dot(og_ref[...], wo_ref[...], preferred_element_type=F32)
    vs = []
    for pr in range(nh // 2):
        vp = (jnp.dot(ctx_ref[0, 2 * pr], wuv_ref[2 * pr], preferred_element_type=F32)
              + jnp.dot(ctx_ref[0, 2 * pr + 1], wuv_ref[2 * pr + 1], preferred_element_type=F32))
        vs.append(vp.astype(BF16))
    o_mla = jnp.dot(jnp.concatenate(vs, axis=1), mwo_ref[...], preferred_element_type=F32)
    o_mem = jnp.dot(om_ref[...], memwo_ref[...], preferred_element_type=F32)
    merged = gates_ref[:, 0:d] * o_rwkv + gates_ref[:, d:2 * d] * o_mla + gates_ref[:, 2 * d:3 * d] * o_mem
    y = jnp.dot(merged.astype(BF16), wout_ref[...], preferred_element_type=F32)
    o_ref[...] = h_ref[...] + _rms(y, post_ref[...])


def _merge(h, nbatch, og, ctx, om, gates, wo, wuv, mwo, memwo, wout, post, nh, tm):
    n, d = h.shape
    nb = (n // nbatch) // tm
    row = lambda i: (i, 0)
    kl = ctx.shape[-1]
    return pl.pallas_call(
        functools.partial(_merge_body, nh),
        grid=(n // tm,),
        in_specs=[pl.BlockSpec((tm, d), row), pl.BlockSpec((tm, og.shape[1]), row),
                  pl.BlockSpec((1, nh, tm, kl), lambda i: (i // nb, 0, i % nb, 0)),
                  pl.BlockSpec((tm, om.shape[1]), row), pl.BlockSpec((tm, 3 * d), row),
                  _resident(wo.shape), _resident(wuv.shape), _resident(mwo.shape), _resident(memwo.shape),
                  _resident(wout.shape), _resident((1, d))],
        out_specs=pl.BlockSpec((tm, d), row),
        out_shape=jax.ShapeDtypeStruct((n, d), F32),
        compiler_params=_cparams("parallel"),
        name="merge",
    )(h, og, ctx, om, gates, wo, wuv, mwo, memwo, wout, post)


def _rope_tables(pos, rope, nh):
    half = rope // 2
    freqs = ROPE_BASE ** (-jnp.arange(half, dtype=F32) / half)
    ang = pos.astype(F32)[:, None] * freqs
    cos, sin = jnp.cos(ang), jnp.sin(ang)
    return jnp.tile(jnp.concatenate([cos, cos], axis=1), (1, nh)), jnp.tile(jnp.concatenate([-sin, sin], axis=1), (1, nh))


def _prep_weights(W, d):
    nh, hd = W["rwkv_r_k"].shape
    rd = nh * hd
    lora = W["rwkv_w2"].shape[0]
    glora = W["rwkv_g2"].shape[0]
    rp = 3 * rd + 2 * lora + glora
    ql = W["mla_q_norm"].shape[0]
    kl, mh, vh = W["mla_w_uv"].shape
    nope = W["mla_w_uk"].shape[2]
    rope = W["mla_w_qb"].shape[1] // mh - nope
    md = W["mem_w_k"].shape[1]
    half = rope // 2
    row = lambda x: x.reshape(1, -1)
    w_in = W["w_in"]
    o_cq, o_kv, o_pe, o_mem, o_g = rp, rp + ql, rp + ql + kl, rp + ql + kl + rope, rp + ql + kl + rope + md
    cols = [w_in[:, :o_pe], w_in[:, o_mem:], w_in[:, o_pe:o_mem],
            w_in[:, o_pe + half:o_mem], w_in[:, o_pe:o_pe + half]]
    width = sum(c.shape[1] for c in cols)
    pad = (-width) % LANES
    win = jnp.concatenate(cols + [jnp.zeros((d, pad), F32)], axis=1).astype(BF16)
    wqb = W["mla_w_qb"].reshape(ql, mh, nope + rope)
    wqb = jnp.concatenate([wqb[:, :, :nope].reshape(ql, mh * nope),
                           wqb[:, :, nope:].reshape(ql, mh * rope),
                           jnp.concatenate([wqb[:, :, nope + half:], wqb[:, :, nope:nope + half]], axis=2).reshape(ql, mh * rope)],
                          axis=1).astype(BF16)
    ukt = jnp.transpose(W["mla_w_uk"], (1, 2, 0))
    z = jnp.zeros_like(ukt[0])
    wuk = jnp.stack([jnp.concatenate([jnp.concatenate([ukt[2 * p], z], axis=1),
                                      jnp.concatenate([z, ukt[2 * p + 1]], axis=1)], axis=0)
                     for p in range(mh // 2)]).astype(BF16)
    uv = jnp.transpose(W["mla_w_uv"], (1, 0, 2))
    zv = jnp.zeros_like(uv[0])
    wuv = jnp.stack([jnp.concatenate([uv[h], zv] if h % 2 == 0 else [zv, uv[h]], axis=1)
                     for h in range(mh)]).astype(BF16)
    zl = jnp.zeros((lora, rd), F32)
    w2a2 = jnp.concatenate([jnp.concatenate([W["rwkv_w2"], zl], axis=1),
                            jnp.concatenate([zl, W["rwkv_a2"]], axis=1)], axis=0).astype(BF16)
    hid = jnp.arange(rd) // hd
    rw = dict(mu=row(W["rwkv_mu"]), w0=row(W["rwkv_w0"]), w2a2=w2a2, a0=row(W["rwkv_a0"]), g2=W["rwkv_g2"].astype(BF16),
              k_k=row(W["rwkv_k_k"]), k_a=row(W["rwkv_k_a"]), r_k=row(W["rwkv_r_k"]), lnx_g=row(W["rwkv_lnx_g"]),
              lnx_b=row(W["rwkv_lnx_b"]), e=(hid[:, None] == hid[None, :]).astype(BF16))
    dims = (rp, ql, kl, rope, md, 3 * d, mh, nope)
    return dict(
        dims=dims, nh=nh, hd=hd, lora=lora, rw=rw, win=win, wqb=wqb, wuk=wuk, wuv=wuv,
        ffn1=(row(W["ffn1_pre"]), row(W["ffn1_post"]), W["ffn1_gate"].astype(BF16), W["ffn1_up"].astype(BF16),
              W["ffn1_down"].astype(BF16)),
        ffn2=(row(W["ffn2_pre"]), row(W["ffn2_post"]), W["ffn2_gate"].astype(BF16), W["ffn2_up"].astype(BF16),
              W["ffn2_down"].astype(BF16)),
        mix_pre=row(W["mix_pre"]), mix_post=row(W["mix_post"]), q_norm=row(W["mla_q_norm"]), kv_norm=row(W["mla_kv_norm"]),
        mem_norm=row(W["mem_norm"]), mem_wkv=jnp.concatenate([W["mem_w_k"], W["mem_w_v"]], axis=1).astype(BF16),
        rwkv_wo=W["rwkv_w_o"].astype(BF16), mla_wo=W["mla_w_o"].astype(BF16), mem_wo=W["mem_w_o"].astype(BF16),
        w_out=W["w_out"].astype(BF16), qscale=float(nope + rope) ** -0.5 * LOG2E,
    )


def _tile(n, pref):
    t = min(pref, n)
    assert n % t == 0, (n, t)
    return t


def kernel(x_prompt, x_sample, cache_mla, state_rwkv, state_shift, cache_mem_k, cache_mem_v, page_table, mem_prompt, ffn1_pre, ffn1_post, ffn1_gate, ffn1_up, ffn1_down, mix_pre, mix_post, w_in, rwkv_mu, rwkv_w0, rwkv_w2, rwkv_a0, rwkv_a2, rwkv_g2, rwkv_k_k, rwkv_k_a, rwkv_r_k, rwkv_lnx_g, rwkv_lnx_b, rwkv_w_o, mla_q_norm, mla_w_qb, mla_kv_norm, mla_w_uk, mla_w_uv, mla_w_o, mem_norm, mem_w_k, mem_w_v, mem_w_o, w_out, ffn2_pre, ffn2_post, ffn2_gate, ffn2_up, ffn2_down):
    names = ("ffn1_pre", "ffn1_post", "ffn1_gate", "ffn1_up", "ffn1_down", "mix_pre", "mix_post", "w_in",
             "rwkv_mu", "rwkv_w0", "rwkv_w2", "rwkv_a0", "rwkv_a2", "rwkv_g2", "rwkv_k_k", "rwkv_k_a", "rwkv_r_k",
             "rwkv_lnx_g", "rwkv_lnx_b", "rwkv_w_o", "mla_q_norm", "mla_w_qb", "mla_kv_norm", "mla_w_uk", "mla_w_uv",
             "mla_w_o", "mem_norm", "mem_w_k", "mem_w_v", "mem_w_o", "w_out", "ffn2_pre", "ffn2_post", "ffn2_gate",
             "ffn2_up", "ffn2_down")
    stacked = (ffn1_pre, ffn1_post, ffn1_gate, ffn1_up, ffn1_down, mix_pre, mix_post, w_in,
               rwkv_mu, rwkv_w0, rwkv_w2, rwkv_a0, rwkv_a2, rwkv_g2, rwkv_k_k, rwkv_k_a, rwkv_r_k,
               rwkv_lnx_g, rwkv_lnx_b, rwkv_w_o, mla_q_norm, mla_w_qb, mla_kv_norm, mla_w_uk, mla_w_uv,
               mla_w_o, mem_norm, mem_w_k, mem_w_v, mem_w_o, w_out, ffn2_pre, ffn2_post, ffn2_gate,
               ffn2_up, ffn2_down)
    B, S, D = x_prompt.shape
    DB, T, _ = x_sample.shape
    assert T == 1, "decode groups carry one new token per request"
    depth = ffn1_pre.shape[0]
    page = cache_mla.shape[2]
    past_len = page_table.shape[1] * page
    mem_tokens, mem_heads, mem_hd = cache_mem_k.shape[2:]
    mem_scale = float(mem_hd) ** -0.5

    xp = x_prompt.reshape(B * S, D)
    xs = x_sample.reshape(DB, D)
    outs = [[] for _ in range(8)]
    for l in range(depth):
        P = _prep_weights({n: w[l] for n, w in zip(names, stacked)}, D)
        rp, ql, kl, rope, md, gd, mh, nope = P["dims"]
        nh, hd, lora, rw = P["nh"], P["hd"], P["lora"], P["rw"]
        cos_p, sin_p = _rope_tables(jnp.arange(S), rope, mh)
        cos_s, sin_s = _rope_tables(jnp.full((DB,), past_len), rope, mh)
        tm_p, tm_s = _tile(S, 256), DB

        mkv = _memkv(mem_prompt.reshape(B * mem_tokens, D), P["mem_norm"], P["mem_wkv"])
        mk_p, mv_p = mkv[:, :md].reshape(B, mem_tokens, md), mkv[:, md:].reshape(B, mem_tokens, md)

        h = _ffn(xp, *P["ffn1"], tm_p)
        prw, q, rows, kbf, qmem, gates = _inproj(h, B, P["mix_pre"], P["win"], P["q_norm"], P["wqb"], P["wuk"],
                                                 P["kv_norm"], cos_p, sin_p, P["dims"], P["qscale"], tm_p)
        og, wkv_p = _rwkv_prompt(prw.reshape(B, S, rp), rw, nh, hd, lora)
        ctx = _attn_prompt(q, kbf.reshape(B, S, kl + rope), kl)
        om = _memattn_prompt(qmem, mk_p, mv_p, mem_heads, mem_scale, tm_p)
        h = _merge(h, B, og.reshape(B * S, nh * hd), ctx, om, gates, P["rwkv_wo"], P["wuv"], P["mla_wo"], P["mem_wo"],
                   P["w_out"], P["mix_post"], mh, tm_p)
        xp = _ffn(h, *P["ffn2"], tm_p)
        rows_p, shift_p = rows.reshape(B, S, kl + rope), prw.reshape(B, S, rp)[:, -1]

        h = _ffn(xs, *P["ffn1"], tm_s)
        prw, q, rows, kbf, qmem, gates = _inproj(h, 1, P["mix_pre"], P["win"], P["q_norm"], P["wqb"], P["wuk"],
                                                 P["kv_norm"], cos_s, sin_s, P["dims"], P["qscale"], tm_s)
        og, wkv_s = _rwkv_sample(prw, state_shift[l], state_rwkv, l, rw, nh, hd, lora, _tile(DB, 8))
        ctx = _attn_sample(jnp.swapaxes(q[0], 0, 1), kbf.reshape(DB, 1, kl + rope), cache_mla, l, page_table, kl)
        om = _memattn_sample(qmem, cache_mem_k, cache_mem_v, l, mem_scale, _tile(DB, 4))
        h = _merge(h, 1, og, jnp.swapaxes(ctx, 0, 1)[None], om, gates, P["rwkv_wo"], P["wuv"], P["mla_wo"], P["mem_wo"],
                   P["w_out"], P["mix_post"], mh, tm_s)
        xs = _ffn(h, *P["ffn2"], tm_s)

        for lst, val in zip(outs, (rows_p, rows.reshape(DB, T, kl + rope), wkv_p, wkv_s, shift_p, prw,
                                   mk_p.reshape(B, mem_tokens, mem_heads, mem_hd),
                                   mv_p.reshape(B, mem_tokens, mem_heads, mem_hd))):
            lst.append(val)
    return (xp.reshape(B, S, D), xs.reshape(DB, T, D)) + tuple(jnp.stack(o) for o in outs)
```

```python
import functools

import jax
import jax.numpy as jnp
from jax import lax
from jax.experimental import pallas as pl
from jax.experimental.pallas import tpu as pltpu

F32, BF16 = jnp.float32, jnp.bfloat16
RMS_EPS = 1e-6
LNX_EPS = 64e-5
ROPE_BASE = 10000.0
LANES = 128
VMEM_LIMIT = 52 * 1024 * 1024
RWKV_CHUNK = 64
RWKV_CHUNKS_PER_STEP = 2
ATT_TQ = 128
ATT_TK = 1024
ATT_ROW_GROUP = 256
PAGES_PER_STEP = 32
LOG2E = 1.4426950408889634


def _cparams(*sem):
    return pltpu.CompilerParams(dimension_semantics=sem, vmem_limit_bytes=VMEM_LIMIT)


def _resident(shape):
    nd = len(shape)
    return pl.BlockSpec(shape, lambda *_: (0,) * nd, pipeline_mode=pl.Buffered(1))


def _rms(x, g):
    return x * lax.rsqrt(jnp.mean(x * x, axis=-1, keepdims=True) + RMS_EPS) * g


def _sigmoid(x):
    return 1.0 / (1.0 + jnp.exp(-x))


def _mm(a, b):
    return jnp.dot(a.astype(BF16), b.astype(BF16), preferred_element_type=F32)


def _mm_nt(a, b):
    return lax.dot_general(a.astype(BF16), b.astype(BF16), (((1,), (1,)), ((), ())), preferred_element_type=F32)


def _ffn_body(x_ref, pre_ref, post_ref, wg_ref, wu_ref, wd_ref, o_ref):
    x = x_ref[...]
    h = _rms(x, pre_ref[...]).astype(BF16)
    g = jnp.dot(h, wg_ref[...], preferred_element_type=F32)
    u = jnp.dot(h, wu_ref[...], preferred_element_type=F32)
    act = (g * _sigmoid(g)) * u
    y = jnp.dot(act.astype(BF16), wd_ref[...], preferred_element_type=F32)
    o_ref[...] = x + 0.5 * _rms(y, post_ref[...])


def _ffn(x, pre, post, wg, wu, wd, tm):
    n, d = x.shape
    f = wg.shape[1]
    return pl.pallas_call(
        _ffn_body,
        grid=(n // tm,),
        in_specs=[pl.BlockSpec((tm, d), lambda i: (i, 0)), _resident((1, d)), _resident((1, d)),
                  _resident((d, f)), _resident((d, f)), _resident((f, d))],
        out_specs=pl.BlockSpec((tm, d), lambda i: (i, 0)),
        out_shape=jax.ShapeDtypeStruct((n, d), F32),
        compiler_params=_cparams("parallel"),
        name="ffn",
    )(x, pre, post, wg, wu, wd)


def _inproj_body(dims, qscale, h_ref, pre_ref, win_ref, qn_ref, wqb_ref, wuk_ref, kvn_ref, cos_ref, sin_ref,
                 prw_ref, q_ref, rows_ref, kbf_ref, qmem_ref, gates_ref):
    rp, ql, kl, rope, md, gd, nh, nope = dims
    u = _rms(h_ref[...], pre_ref[...]).astype(BF16)
    p = jnp.dot(u, win_ref[...], preferred_element_type=F32)
    o = 0
    prw_ref[...] = p[:, o:o + rp]; o += rp
    cq = p[:, o:o + ql]; o += ql
    ckv = p[:, o:o + kl]; o += kl
    qmem_ref[...] = p[:, o:o + md].astype(BF16); o += md
    gates_ref[...] = _sigmoid(p[:, o:o + gd]); o += gd
    kpe = p[:, o:o + rope]; o += rope
    kpe_sw = p[:, o:o + rope]
    cos = cos_ref[...]
    sin = sin_ref[...]
    q = jnp.dot(_rms(cq, qn_ref[...]).astype(BF16), wqb_ref[...], preferred_element_type=F32)
    nn = nh * nope
    nr = nh * rope
    qpe = ((q[:, nn:nn + nr] * cos + q[:, nn + nr:nn + 2 * nr] * sin) * qscale).astype(BF16)
    qn = q[:, :nn].astype(BF16)
    for pr in range(nh // 2):
        qlat = (jnp.dot(qn[:, LANES * pr:LANES * (pr + 1)], wuk_ref[pr], preferred_element_type=F32) * qscale).astype(BF16)
        for e in range(2):
            hh = 2 * pr + e
            q_ref[0, hh, :, 0:kl] = qlat[:, kl * e:kl * (e + 1)]
            q_ref[0, hh, :, kl:kl + rope] = qpe[:, rope * hh:rope * (hh + 1)]
    ckvn = _rms(ckv, kvn_ref[...])
    kper = kpe * cos[:, :rope] + kpe_sw * sin[:, :rope]
    rows_ref[:, 0:kl] = ckvn
    rows_ref[:, kl:kl + rope] = kper
    kbf_ref[:, 0:kl] = ckvn.astype(BF16)
    kbf_ref[:, kl:kl + rope] = kper.astype(BF16)


def _inproj(h, nbatch, pre, win, qn, wqb, wuk, kvn, cos, sin, dims, qscale, tm):
    n, d = h.shape
    rp, ql, kl, rope, md, gd, nh, nope = dims
    t = n // nbatch
    nb = t // tm
    cw = win.shape[1]
    row = lambda i: (i, 0)
    tab = lambda i: (i % nb, 0)
    return pl.pallas_call(
        functools.partial(_inproj_body, dims, qscale),
        grid=(n // tm,),
        in_specs=[pl.BlockSpec((tm, d), row), _resident((1, d)), _resident((d, cw)), _resident((1, ql)),
                  _resident(wqb.shape), _resident(wuk.shape), _resident((1, kl)),
                  pl.BlockSpec((tm, nh * rope), tab), pl.BlockSpec((tm, nh * rope), tab)],
        out_specs=[pl.BlockSpec((tm, rp), row),
                   pl.BlockSpec((1, nh, tm, kl + rope), lambda i: (i // nb, 0, i % nb, 0)),
                   pl.BlockSpec((tm, kl + rope), row), pl.BlockSpec((tm, kl + rope), row),
                   pl.BlockSpec((tm, md), row), pl.BlockSpec((tm, gd), row)],
        out_shape=[jax.ShapeDtypeStruct((n, rp), F32),
                   jax.ShapeDtypeStruct((nbatch, nh, t, kl + rope), BF16),
                   jax.ShapeDtypeStruct((n, kl + rope), F32),
                   jax.ShapeDtypeStruct((n, kl + rope), BF16),
                   jax.ShapeDtypeStruct((n, md), BF16),
                   jax.ShapeDtypeStruct((n, gd), F32)],
        compiler_params=_cparams("parallel"),
        name="inproj",
    )(h, pre, win, qn, wqb, wuk, kvn, cos, sin)


def _segsum(x, e):
    hi = x.astype(BF16)
    lo = (x - hi.astype(F32)).astype(BF16)
    return jnp.dot(hi, e, preferred_element_type=F32) + jnp.dot(lo, e, preferred_element_type=F32)


def _rwkv_prep(p, prev, mu, w0, w2a2, a0, g2, k_k, k_a, e, rd, lora):
    ps = p + (prev - p) * mu
    r = ps[:, 0:rd]
    k = ps[:, rd:2 * rd]
    v = ps[:, 2 * rd:3 * rd]
    wa = ps[:, 3 * rd:3 * rd + 2 * lora]
    gl = ps[:, 3 * rd + 2 * lora:]
    lane = lax.broadcasted_iota(jnp.int32, wa.shape, 1)
    wa = jnp.where(lane < lora, jnp.tanh(wa), wa)
    wa2 = _mm(wa, w2a2)
    x = -(w0 + wa2[:, :rd])
    softplus = jnp.maximum(x, 0.0) + jnp.log(1.0 + jnp.exp(-jnp.abs(x)))
    logdec = -jnp.exp(-softplus - 0.5)
    a = _sigmoid(a0 + wa2[:, rd:])
    g = _mm(_sigmoid(gl), g2)
    kk = k * k_k
    kk = kk / jnp.maximum(jnp.sqrt(_segsum(kk * kk, e)), 1e-12)
    k = k * (1.0 + (a - 1.0) * k_a)
    return r, k, v, logdec, -kk, kk * a, g


def _rwkv_post(o, r, k, v, g, r_k, lng, lnb, e, hd):
    mean = _segsum(o, e) * (1.0 / hd)
    oc = o - mean
    var = _segsum(oc * oc, e) * (1.0 / hd)
    o = oc * lax.rsqrt(var + LNX_EPS) * lng + lnb
    bonus = _segsum(r * k * r_k, e) * v
    return (o + bonus) * g


def _pair_rows(y):
    lo = (lax.broadcasted_iota(jnp.int32, y.shape, 1) % LANES) < (LANES // 2)
    z = jnp.zeros_like(y)
    return jnp.concatenate([jnp.where(lo, y, z), jnp.where(lo, z, y)], axis=0)


def _rwkv_chunk_local(r, k, v, ld, cum, a, b, strict, incl):
    L = r.shape[0]
    cum_l = cum[L - 1:L, :]
    e_neg = jnp.exp(-cum)
    e_pos = jnp.exp(cum)
    e_exc = jnp.exp(cum - ld)
    e_end = jnp.exp(cum_l - cum)
    kt, bt = (k * e_neg).astype(BF16), (b * e_neg).astype(BF16)
    at, rt = a * e_exc, (r * e_pos).astype(BF16)
    vb = _pair_rows(v.astype(BF16))
    mm = _mm_nt(jnp.concatenate([at.astype(BF16), rt], axis=0),
                jnp.concatenate([_pair_rows(bt), _pair_rows(kt)], axis=0))
    m_ba = jnp.where(strict, mm[:L, :2 * L], 0.0)
    m_ka = jnp.where(strict, mm[:L, 2 * L:], 0.0)
    m_r = jnp.concatenate([jnp.where(incl, mm[L:, :2 * L], 0.0), jnp.where(incl, mm[L:, 2 * L:], 0.0)],
                          axis=1).astype(BF16)
    x = jnp.concatenate([at, _mm(m_ka, vb)], axis=1)
    pw = m_ba
    span = 1
    while span < L:
        x = x + _mm(pw, _pair_rows(x.astype(BF16)))
        span *= 2
        if span < L:
            pw = _mm(pw, _pair_rows(pw.astype(BF16)))
    bkh = jnp.concatenate([b * e_end, k * e_end], axis=0).astype(BF16)
    return dict(w1=x[:, :LANES].astype(BF16), uloc=x[:, LANES:], rt=rt, m_r=m_r, vb=vb, v=v, bkh=bkh, dl=jnp.exp(cum_l))


def _rwkv_chunk_apply(c, s, diag):
    sb = s.astype(BF16)
    ur = _mm_nt(c["w1"], sb) + c["uloc"]
    o = _mm_nt(c["rt"], sb) + _mm(c["m_r"], jnp.concatenate([_pair_rows(ur.astype(BF16)), c["vb"]], axis=0))
    upd = _mm(jnp.concatenate([ur, c["v"]], axis=0).T, c["bkh"])
    return o, s * c["dl"] + jnp.where(diag, upd, 0.0)


def _rwkv_prompt_body(rd, lora, hd, L, p_ref, mu_ref, w0_ref, w2a2_ref, a0_ref, g2_ref, kk_ref, ka_ref, rk_ref,
                      lng_ref, lnb_ref, e_ref, og_ref, st_ref, prev_ref, s_ref):
    step = pl.program_id(0)
    nb, rows, _ = p_ref.shape
    npair = rd // LANES

    @pl.when(step == 0)
    def _():
        prev_ref[...] = jnp.zeros_like(prev_ref)
        s_ref[...] = jnp.zeros_like(s_ref)

    e = e_ref[...]
    ti = lax.broadcasted_iota(jnp.int32, (L, 2 * L), 0)
    si = lax.broadcasted_iota(jnp.int32, (L, 2 * L), 1) % L
    strict, incl = si < ti, si <= ti
    half = LANES // 2
    diag = ((lax.broadcasted_iota(jnp.int32, (LANES, LANES), 0) < half)
            == (lax.broadcasted_iota(jnp.int32, (LANES, LANES), 1) < half))
    tr = lax.broadcasted_iota(jnp.int32, (rows, rows), 0)
    tc = lax.broadcasted_iota(jnp.int32, (rows, rows), 1)
    tri = ((tc <= tr) & (tc // L == tr // L)).astype(BF16)
    rowi = lax.broadcasted_iota(jnp.int32, (rows, p_ref.shape[2]), 0)

    local, vecs = {}, []
    for bi in range(nb):
        p = p_ref[bi]
        prev = jnp.where(rowi == 0, prev_ref[bi], pltpu.roll(p, 1, axis=0))
        prev_ref[bi] = p[rows - 1:rows, :]
        r, k, v, ld, a, b, g = _rwkv_prep(p, prev, mu_ref[...], w0_ref[...], w2a2_ref[...], a0_ref[...], g2_ref[...],
                                          kk_ref[...], ka_ref[...], e, rd, lora)
        hi = ld.astype(BF16)
        r1 = ld - hi.astype(F32)
        mid = r1.astype(BF16)
        lo = (r1 - mid.astype(F32)).astype(BF16)
        cum = (jnp.dot(tri, hi, preferred_element_type=F32) + jnp.dot(tri, mid, preferred_element_type=F32)
               + jnp.dot(tri, lo, preferred_element_type=F32))
        vecs.append((r, k, v, g))
        for cc in range(rows // L):
            for pr in range(npair):
                sl = (slice(L * cc, L * (cc + 1)), slice(LANES * pr, LANES * (pr + 1)))
                local[bi, cc, pr] = _rwkv_chunk_local(r[sl], k[sl], v[sl], ld[sl], cum[sl], a[sl], b[sl], strict, incl)
    for bi in range(nb):
        cols = []
        for pr in range(npair):
            s = s_ref[bi * npair + pr]
            outs = []
            for cc in range(rows // L):
                o, s = _rwkv_chunk_apply(local[bi, cc, pr], s, diag)
                outs.append(o)
            s_ref[bi * npair + pr] = s
            cols.append(jnp.concatenate(outs, axis=0))
        r, k, v, g = vecs[bi]
        og_ref[bi] = _rwkv_post(jnp.concatenate(cols, axis=1), r, k, v, g, rk_ref[...], lng_ref[...], lnb_ref[...], e,
                                hd).astype(BF16)

    @pl.when(step == pl.num_programs(0) - 1)
    def _():
        for bi in range(nb):
            for pr in range(npair):
                s = s_ref[bi * npair + pr]
                st_ref[bi, 2 * pr] = s[:hd, :hd]
                st_ref[bi, 2 * pr + 1] = s[hd:, hd:]


def _rwkv_prompt(prw, rw, nh, hd, lora):
    b, t, pw = prw.shape
    rd = nh * hd
    rows = _tile(t, RWKV_CHUNK * RWKV_CHUNKS_PER_STEP)
    names = ("mu", "w0", "w2a2", "a0", "g2", "k_k", "k_a", "r_k", "lnx_g", "lnx_b", "e")
    return pl.pallas_call(
        functools.partial(_rwkv_prompt_body, rd, lora, hd, RWKV_CHUNK),
        grid=(t // rows,),
        in_specs=[pl.BlockSpec((b, rows, pw), lambda c: (0, c, 0))] + [_resident(rw[k].shape) for k in names],
        out_specs=[pl.BlockSpec((b, rows, rd), lambda c: (0, c, 0)),
                   pl.BlockSpec((b, nh, hd, hd), lambda c: (0, 0, 0, 0))],
        out_shape=[jax.ShapeDtypeStruct((b, t, rd), BF16), jax.ShapeDtypeStruct((b, nh, hd, hd), F32)],
        scratch_shapes=[pltpu.VMEM((b, 1, pw), F32), pltpu.VMEM((b * (rd // LANES), LANES, LANES), F32)],
        compiler_params=_cparams("arbitrary"),
        name="rwkv_prompt",
    )(prw, *[rw[k] for k in names])


def _rwkv_prep_body(rd, lora, p_ref, prev_ref, mu_ref, w0_ref, w2a2_ref, a0_ref, g2_ref, kk_ref, ka_ref, e_ref,
                    r_ref, k_ref, v_ref, g_ref, *t_refs):
    r, k, v, ld, a, b, g = _rwkv_prep(p_ref[...], prev_ref[...], mu_ref[...], w0_ref[...], w2a2_ref[...], a0_ref[...],
                                      g2_ref[...], kk_ref[...], ka_ref[...], e_ref[...], rd, lora)
    r_ref[...] = r
    k_ref[...] = k
    v_ref[...] = v
    g_ref[...] = g
    for ref, x in zip(t_refs, (r, k, v, jnp.exp(ld), a, b)):
        ref[...] = x.T


def _rwkv_step_body(s_ref, r_ref, k_ref, v_ref, w_ref, a_ref, b_ref, so_ref, o_ref):
    for h in range(s_ref.shape[0]):
        s = s_ref[h]
        sa = jnp.sum(s * a_ref[h][None], axis=1)
        s = s * w_ref[h][None] + sa[:, None, :] * b_ref[h][None] + v_ref[h][:, None, :] * k_ref[h][None]
        so_ref[h] = s
        o_ref[h] = jnp.sum(s * r_ref[h][None], axis=1)


def _rwkv_post_body(hd, o_ref, r_ref, k_ref, v_ref, g_ref, rk_ref, lng_ref, lnb_ref, e_ref, og_ref):
    og_ref[...] = _rwkv_post(o_ref[...].T, r_ref[...], k_ref[...], v_ref[...], g_ref[...], rk_ref[...], lng_ref[...],
                             lnb_ref[...], e_ref[...], hd).astype(BF16)


def _rwkv_sample(prw, shift, state_t, layer, rw, nh, hd, lora, hb):
    n, pw = prw.shape
    rd = nh * hd
    names = ("mu", "w0", "w2a2", "a0", "g2", "k_k", "k_a", "e")
    full = lambda s: pl.BlockSpec(s, lambda: (0,) * len(s))
    vecs = pl.pallas_call(
        functools.partial(_rwkv_prep_body, rd, lora),
        in_specs=[full((n, pw)), full((n, pw))] + [full(rw[k].shape) for k in names],
        out_specs=[full((n, rd))] * 4 + [full((rd, n))] * 6,
        out_shape=[jax.ShapeDtypeStruct((n, rd), F32)] * 4 + [jax.ShapeDtypeStruct((rd, n), F32)] * 6,
        name="rwkv_prep",
    )(prw, shift, *[rw[k] for k in names])
    r, k, v, g = vecs[:4]
    vspec = pl.BlockSpec((hb, hd, n), lambda i: (i, 0, 0))
    sspec = pl.BlockSpec((hb, hd, hd, n), lambda i: (i, 0, 0, 0))
    s_new, o = pl.pallas_call(
        _rwkv_step_body,
        grid=(nh // hb,),
        in_specs=[pl.BlockSpec((None, hb, hd, hd, n), lambda i: (layer, i, 0, 0, 0))] + [vspec] * 6,
        out_specs=[sspec, vspec],
        out_shape=[jax.ShapeDtypeStruct(state_t.shape[1:], F32), jax.ShapeDtypeStruct((nh, hd, n), F32)],
        compiler_params=_cparams("parallel"),
        name="rwkv_step",
    )(state_t, *[x.reshape(nh, hd, n) for x in vecs[4:]])
    pnames = ("r_k", "lnx_g", "lnx_b", "e")
    og = pl.pallas_call(
        functools.partial(_rwkv_post_body, hd),
        in_specs=[full((rd, n))] + [full((n, rd))] * 4 + [full(rw[k].shape) for k in pnames],
        out_specs=full((n, rd)),
        out_shape=jax.ShapeDtypeStruct((n, rd), BF16),
        name="rwkv_post",
    )(o.reshape(rd, n), r, k, v, g, *[rw[k] for k in pnames])
    return og, s_new


def _lanes(x, n):
    return x if n == LANES else jnp.concatenate([x] * (n // LANES), axis=1)


def _attn_prompt_body(kl, tk, rg, q_ref, k_ref, o_ref, m_ref, l_ref, acc_ref):
    i = pl.program_id(1)
    nh, tq, dk = q_ref.shape[1:]
    rows = nh * tq
    q = q_ref[0].reshape(rows, dk)
    m_ref[...] = jnp.full_like(m_ref, -jnp.inf)
    l_ref[...] = jnp.zeros_like(l_ref)
    acc_ref[...] = jnp.zeros_like(acc_ref)

    def step(j, masked):
        k = k_ref[0, pl.ds(pl.multiple_of(j * tk, tk), tk), :]
        v = k[:, :kl]
        for g in range(rows // rg):
            r = slice(g * rg, (g + 1) * rg)
            s = lax.dot_general(q[r], k, (((1,), (1,)), ((), ())), preferred_element_type=F32)
            if masked:
                qpos = i * tq + (g * rg + lax.broadcasted_iota(jnp.int32, s.shape, 0)) % tq
                kpos = j * tk + lax.broadcasted_iota(jnp.int32, s.shape, 1)
                s = jnp.where(kpos <= qpos, s, -jnp.inf)
            m_old = m_ref[r]
            m_new = jnp.maximum(m_old, jnp.max(s, axis=-1, keepdims=True))
            alpha = jnp.exp2(m_old - m_new)
            p = jnp.exp2(s - _lanes(m_new, tk))
            l_ref[r] = alpha * l_ref[r] + jnp.sum(p, axis=-1, keepdims=True)
            acc_ref[r] = _lanes(alpha, kl) * acc_ref[r] + jnp.dot(p.astype(BF16), v, preferred_element_type=F32)
            m_ref[r] = m_new

    last = (i * tq) // tk
    lax.fori_loop(0, last, lambda j, c: (step(j, False), c)[1], 0)
    step(last, True)
    o_ref[0] = (acc_ref[...] / _lanes(l_ref[...], kl)).reshape(nh, tq, kl).astype(o_ref.dtype)


def _attn_prompt(q, kbf, kl):
    b, nh, t, dk = q.shape
    tq, tk = _tile(t, ATT_TQ), _tile(t, ATT_TK)
    rows = nh * tq
    rg = _tile(rows, ATT_ROW_GROUP)
    assert tk % tq == 0 and rg % tq == 0
    return pl.pallas_call(
        functools.partial(_attn_prompt_body, kl, tk, rg),
        grid=(b, t // tq),
        in_specs=[pl.BlockSpec((1, nh, tq, dk), lambda bi, i: (bi, 0, i, 0)),
                  pl.BlockSpec((1, t, dk), lambda bi, i: (bi, 0, 0))],
        out_specs=pl.BlockSpec((1, nh, tq, kl), lambda bi, i: (bi, 0, i, 0)),
        out_shape=jax.ShapeDtypeStruct((b, nh, t, kl), BF16),
        scratch_shapes=[pltpu.VMEM((rows, LANES), F32), pltpu.VMEM((rows, LANES), F32), pltpu.VMEM((rows, kl), F32)],
        compiler_params=_cparams("parallel", "arbitrary"),
        name="attn_prompt",
    )(q, kbf)


def _attn_sample_body(kl, npg, pt_ref, q_ref, kself_ref, *rest):
    pages, (o_ref, m_ref, l_ref, acc_ref) = rest[:npg], rest[npg:]
    j = pl.program_id(1)
    q = q_ref[0]

    @pl.when(j == 0)
    def _():
        ks = kself_ref[0]
        m_ref[...] = jnp.sum(q.astype(F32) * ks.astype(F32), axis=-1, keepdims=True)
        l_ref[...] = jnp.ones_like(l_ref)
        acc_ref[...] = jnp.broadcast_to(ks[:, :kl].astype(F32), acc_ref.shape)

    grp = 2 if npg % 2 == 0 else 1
    kts = [jnp.concatenate([pages[n + e][0] for e in range(grp)], axis=1).astype(BF16) for n in range(0, npg, grp)]
    s = jnp.concatenate([jnp.dot(q, kt, preferred_element_type=F32) for kt in kts], axis=1)
    m_old = m_ref[...]
    m_new = jnp.maximum(m_old, jnp.max(s, axis=-1, keepdims=True))
    alpha = jnp.exp2(m_old - m_new)
    p = jnp.exp2(s - m_new)
    l_ref[...] = alpha * l_ref[...] + jnp.sum(p, axis=-1, keepdims=True)
    pb = p.astype(BF16)
    w = kts[0].shape[1]
    pv = _mm_nt(pb[:, :w], kts[0][:kl, :])
    for n in range(1, len(kts)):
        pv = pv + _mm_nt(pb[:, n * w:(n + 1) * w], kts[n][:kl, :])
    acc_ref[...] = alpha * acc_ref[...] + pv
    m_ref[...] = m_new

    @pl.when(j == pl.num_programs(1) - 1)
    def _():
        o_ref[0] = (acc_ref[...] / l_ref[...]).astype(o_ref.dtype)


def _attn_sample(q, kself, cache_t, layer, page_table, kl):
    n, nh, dk = q.shape
    ps = cache_t.shape[3]
    npages = page_table.shape[1]
    npg = _tile(npages, PAGES_PER_STEP)

    def page_spec(kk):
        return pl.BlockSpec((None, 1, dk, ps), lambda bi, j, pt: (layer, pt[bi, j * npg + kk], 0, 0))

    grid_spec = pltpu.PrefetchScalarGridSpec(
        num_scalar_prefetch=1,
        grid=(n, npages // npg),
        in_specs=[pl.BlockSpec((1, nh, dk), lambda bi, j, pt: (bi, 0, 0)),
                  pl.BlockSpec((1, 1, dk), lambda bi, j, pt: (bi, 0, 0))] + [page_spec(kk) for kk in range(npg)],
        out_specs=pl.BlockSpec((1, nh, kl), lambda bi, j, pt: (bi, 0, 0)),
        scratch_shapes=[pltpu.VMEM((nh, 1), F32), pltpu.VMEM((nh, 1), F32), pltpu.VMEM((nh, kl), F32)],
    )
    return pl.pallas_call(
        functools.partial(_attn_sample_body, kl, npg),
        grid_spec=grid_spec,
        out_shape=jax.ShapeDtypeStruct((n, nh, kl), BF16),
        compiler_params=_cparams("parallel", "arbitrary"),
        name="attn_sample",
    )(page_table, q, kself, *([cache_t] * npg))


def _memkv_body(m_ref, g_ref, w_ref, o_ref):
    o_ref[...] = jnp.dot(_rms(m_ref[...], g_ref[...]).astype(BF16), w_ref[...], preferred_element_type=F32)


def _memkv(mem, g, wkv):
    n, d = mem.shape
    full = lambda s: pl.BlockSpec(s, lambda: (0,) * len(s))
    return pl.pallas_call(
        _memkv_body,
        in_specs=[full((n, d)), full((1, d)), full(wkv.shape)],
        out_specs=full((n, wkv.shape[1])),
        out_shape=jax.ShapeDtypeStruct((n, wkv.shape[1]), F32),
        name="mem_kv",
    )(mem, g, wkv)


def _mem_attend(q, k, v, scale):
    s = _mm_nt(q, k) * scale
    p = jnp.exp(s - jnp.max(s, axis=-1, keepdims=True))
    p = p / jnp.sum(p, axis=-1, keepdims=True)
    return _mm(p, v)


def _memattn_prompt_body(nh, scale, q_ref, k_ref, v_ref, o_ref):
    hd = q_ref.shape[1] // nh
    q, k, v = q_ref[...], k_ref[0], v_ref[0]
    heads = [slice(h * hd, (h + 1) * hd) for h in range(nh)]
    o_ref[...] = jnp.concatenate([_mem_attend(q[:, sl], k[:, sl], v[:, sl], scale) for sl in heads],
                                 axis=1).astype(o_ref.dtype)


def _memattn_prompt(q, mk, mv, nh, scale, tm):
    n, md = q.shape
    b, m, _ = mk.shape
    nb = (n // b) // tm
    kv = pl.BlockSpec((1, m, md), lambda i: (i // nb, 0, 0))
    return pl.pallas_call(
        functools.partial(_memattn_prompt_body, nh, scale),
        grid=(n // tm,),
        in_specs=[pl.BlockSpec((tm, md), lambda i: (i, 0)), kv, kv],
        out_specs=pl.BlockSpec((tm, md), lambda i: (i, 0)),
        out_shape=jax.ShapeDtypeStruct((n, md), BF16),
        compiler_params=_cparams("parallel"),
        name="memattn_prompt",
    )(q, mk, mv)


def _memattn_sample_body(scale, q_ref, k_ref, v_ref, o_ref):
    gr, _, nh, hd = k_ref.shape
    for g in range(gr):
        q = q_ref[g]
        o_ref[g] = jnp.concatenate([_mem_attend(q[:, h * hd:(h + 1) * hd], k_ref[g, :, h, :], v_ref[g, :, h, :], scale)
                                    for h in range(nh)], axis=1).astype(o_ref.dtype)


def _memattn_sample(q, cache_k, cache_v, layer, scale, gr):
    n, md = q.shape
    _, _, m, nh, hd = cache_k.shape
    kv = pl.BlockSpec((None, gr, m, nh, hd), lambda i: (layer, i, 0, 0, 0))
    qs = pl.BlockSpec((gr, 1, md), lambda i: (i, 0, 0))
    return pl.pallas_call(
        functools.partial(_memattn_sample_body, scale),
        grid=(n // gr,),
        in_specs=[qs, kv, kv],
        out_specs=qs,
        out_shape=jax.ShapeDtypeStruct((n, 1, md), BF16),
        compiler_params=_cparams("parallel"),
        name="memattn_sample",
    )(q.reshape(n, 1, md), cache_k, cache_v).reshape(n, md)


def _merge_body(nh, h_ref, og_ref, ctx_ref, om_ref, gates_ref, wo_ref, wuv_ref, mwo_ref, memwo_ref, wout_ref, post_ref,
                o_ref):
    d = h_ref.shape[1]
    o_rwkv = jnp.dot(og_ref[...], wo_ref[...], preferred_element_type=F32)
    vs = []
    for pr in range(nh // 2):
        vp = (jnp.dot(ctx_ref[0, 2 * pr], wuv_ref[2 * pr], preferred_element_type=F32)
              + jnp.dot(ctx_ref[0, 2 * pr + 1], wuv_ref[2 * pr + 1], preferred_element_type=F32))
        vs.append(vp.astype(BF16))
    o_mla = jnp.dot(jnp.concatenate(vs, axis=1), mwo_ref[...], preferred_element_type=F32)
    o_mem = jnp.dot(om_ref[...], memwo_ref[...], preferred_element_type=F32)
    merged = gates_ref[:, 0:d] * o_rwkv + gates_ref[:, d:2 * d] * o_mla + gates_ref[:, 2 * d:3 * d] * o_mem
    y = jnp.dot(merged.astype(BF16), wout_ref[...], preferred_element_type=F32)
    o_ref[...] = h_ref[...] + _rms(y, post_ref[...])


def _merge(h, nbatch, og, ctx, om, gates, wo, wuv, mwo, memwo, wout, post, nh, tm):
    n, d = h.shape
    nb = (n // nbatch) // tm
    row = lambda i: (i, 0)
    kl = ctx.shape[-1]
    return pl.pallas_call(
        functools.partial(_merge_body, nh),
        grid=(n // tm,),
        in_specs=[pl.BlockSpec((tm, d), row), pl.BlockSpec((tm, og.shape[1]), row),
                  pl.BlockSpec((1, nh, tm, kl), lambda i: (i // nb, 0, i % nb, 0)),
                  pl.BlockSpec((tm, om.shape[1]), row), pl.BlockSpec((tm, 3 * d), row),
                  _resident(wo.shape), _resident(wuv.shape), _resident(mwo.shape), _resident(memwo.shape),
                  _resident(wout.shape), _resident((1, d))],
        out_specs=pl.BlockSpec((tm, d), row),
        out_shape=jax.ShapeDtypeStruct((n, d), F32),
        compiler_params=_cparams("parallel"),
        name="merge",
    )(h, og, ctx, om, gates, wo, wuv, mwo, memwo, wout, post)


def _rope_tables(pos, rope, nh):
    half = rope // 2
    freqs = ROPE_BASE ** (-jnp.arange(half, dtype=F32) / half)
    ang = pos.astype(F32)[:, None] * freqs
    cos, sin = jnp.cos(ang), jnp.sin(ang)
    return jnp.tile(jnp.concatenate([cos, cos], axis=1), (1, nh)), jnp.tile(jnp.concatenate([-sin, sin], axis=1), (1, nh))


def _prep_weights(W, d):
    nh, hd = W["rwkv_r_k"].shape
    rd = nh * hd
    lora = W["rwkv_w2"].shape[0]
    glora = W["rwkv_g2"].shape[0]
    rp = 3 * rd + 2 * lora + glora
    ql = W["mla_q_norm"].shape[0]
    kl, mh, vh = W["mla_w_uv"].shape
    nope = W["mla_w_uk"].shape[2]
    rope = W["mla_w_qb"].shape[1] // mh - nope
    md = W["mem_w_k"].shape[1]
    half = rope // 2
    row = lambda x: x.reshape(1, -1)
    w_in = W["w_in"]
    o_cq, o_kv, o_pe, o_mem, o_g = rp, rp + ql, rp + ql + kl, rp + ql + kl + rope, rp + ql + kl + rope + md
    cols = [w_in[:, :o_pe], w_in[:, o_mem:], w_in[:, o_pe:o_mem],
            w_in[:, o_pe + half:o_mem], w_in[:, o_pe:o_pe + half]]
    width = sum(c.shape[1] for c in cols)
    pad = (-width) % LANES
    win = jnp.concatenate(cols + [jnp.zeros((d, pad), F32)], axis=1).astype(BF16)
    wqb = W["mla_w_qb"].reshape(ql, mh, nope + rope)
    wqb = jnp.concatenate([wqb[:, :, :nope].reshape(ql, mh * nope),
                           wqb[:, :, nope:].reshape(ql, mh * rope),
                           jnp.concatenate([wqb[:, :, nope + half:], wqb[:, :, nope:nope + half]], axis=2).reshape(ql, mh * rope)],
                          axis=1).astype(BF16)
    ukt = jnp.transpose(W["mla_w_uk"], (1, 2, 0))
    z = jnp.zeros_like(ukt[0])
    wuk = jnp.stack([jnp.concatenate([jnp.concatenate([ukt[2 * p], z], axis=1),
                                      jnp.concatenate([z, ukt[2 * p + 1]], axis=1)], axis=0)
                     for p in range(mh // 2)]).astype(BF16)
    uv = jnp.transpose(W["mla_w_uv"], (1, 0, 2))
    zv = jnp.zeros_like(uv[0])
    wuv = jnp.stack([jnp.concatenate([uv[h], zv] if h % 2 == 0 else [zv, uv[h]], axis=1)
                     for h in range(mh)]).astype(BF16)
    zl = jnp.zeros((lora, rd), F32)
    w2a2 = jnp.concatenate([jnp.concatenate([W["rwkv_w2"], zl], axis=1),
                            jnp.concatenate([zl, W["rwkv_a2"]], axis=1)], axis=0).astype(BF16)
    hid = jnp.arange(rd) // hd
    rw = dict(mu=row(W["rwkv_mu"]), w0=row(W["rwkv_w0"]), w2a2=w2a2, a0=row(W["rwkv_a0"]), g2=W["rwkv_g2"].astype(BF16),
              k_k=row(W["rwkv_k_k"]), k_a=row(W["rwkv_k_a"]), r_k=row(W["rwkv_r_k"]), lnx_g=row(W["rwkv_lnx_g"]),
              lnx_b=row(W["rwkv_lnx_b"]), e=(hid[:, None] == hid[None, :]).astype(BF16))
    dims = (rp, ql, kl, rope, md, 3 * d, mh, nope)
    return dict(
        dims=dims, nh=nh, hd=hd, lora=lora, rw=rw, win=win, wqb=wqb, wuk=wuk, wuv=wuv,
        ffn1=(row(W["ffn1_pre"]), row(W["ffn1_post"]), W["ffn1_gate"].astype(BF16), W["ffn1_up"].astype(BF16),
              W["ffn1_down"].astype(BF16)),
        ffn2=(row(W["ffn2_pre"]), row(W["ffn2_post"]), W["ffn2_gate"].astype(BF16), W["ffn2_up"].astype(BF16),
              W["ffn2_down"].astype(BF16)),
        mix_pre=row(W["mix_pre"]), mix_post=row(W["mix_post"]), q_norm=row(W["mla_q_norm"]), kv_norm=row(W["mla_kv_norm"]),
        mem_norm=row(W["mem_norm"]), mem_wkv=jnp.concatenate([W["mem_w_k"], W["mem_w_v"]], axis=1).astype(BF16),
        rwkv_wo=W["rwkv_w_o"].astype(BF16), mla_wo=W["mla_w_o"].astype(BF16), mem_wo=W["mem_w_o"].astype(BF16),
        w_out=W["w_out"].astype(BF16), qscale=float(nope + rope) ** -0.5 * LOG2E,
    )


def _tile(n, pref):
    t = min(pref, n)
    assert n % t == 0, (n, t)
    return t


def kernel(x_prompt, x_sample, cache_mla, state_rwkv, state_shift, cache_mem_k, cache_mem_v, page_table, mem_prompt, ffn1_pre, ffn1_post, ffn1_gate, ffn1_up, ffn1_down, mix_pre, mix_post, w_in, rwkv_mu, rwkv_w0, rwkv_w2, rwkv_a0, rwkv_a2, rwkv_g2, rwkv_k_k, rwkv_k_a, rwkv_r_k, rwkv_lnx_g, rwkv_lnx_b, rwkv_w_o, mla_q_norm, mla_w_qb, mla_kv_norm, mla_w_uk, mla_w_uv, mla_w_o, mem_norm, mem_w_k, mem_w_v, mem_w_o, w_out, ffn2_pre, ffn2_post, ffn2_gate, ffn2_up, ffn2_down):
    names = ("ffn1_pre", "ffn1_post", "ffn1_gate", "ffn1_up", "ffn1_down", "mix_pre", "mix_post", "w_in",
             "rwkv_mu", "rwkv_w0", "rwkv_w2", "rwkv_a0", "rwkv_a2", "rwkv_g2", "rwkv_k_k", "rwkv_k_a", "rwkv_r_k",
             "rwkv_lnx_g", "rwkv_lnx_b", "rwkv_w_o", "mla_q_norm", "mla_w_qb", "mla_kv_norm", "mla_w_uk", "mla_w_uv",
             "mla_w_o", "mem_norm", "mem_w_k", "mem_w_v", "mem_w_o", "w_out", "ffn2_pre", "ffn2_post", "ffn2_gate",
             "ffn2_up", "ffn2_down")
    stacked = (ffn1_pre, ffn1_post, ffn1_gate, ffn1_up, ffn1_down, mix_pre, mix_post, w_in,
               rwkv_mu, rwkv_w0, rwkv_w2, rwkv_a0, rwkv_a2, rwkv_g2, rwkv_k_k, rwkv_k_a, rwkv_r_k,
               rwkv_lnx_g, rwkv_lnx_b, rwkv_w_o, mla_q_norm, mla_w_qb, mla_kv_norm, mla_w_uk, mla_w_uv,
               mla_w_o, mem_norm, mem_w_k, mem_w_v, mem_w_o, w_out, ffn2_pre, ffn2_post, ffn2_gate,
               ffn2_up, ffn2_down)
    B, S, D = x_prompt.shape
    DB, T, _ = x_sample.shape
    assert T == 1, "decode groups carry one new token per request"
    depth = ffn1_pre.shape[0]
    page = cache_mla.shape[2]
    past_len = page_table.shape[1] * page
    mem_tokens, mem_heads, mem_hd = cache_mem_k.shape[2:]
    mem_scale = float(mem_hd) ** -0.5

    cache_t = jnp.swapaxes(cache_mla, 2, 3)
    state_t = jnp.transpose(state_rwkv, (0, 2, 3, 4, 1))
    xp = x_prompt.reshape(B * S, D)
    xs = x_sample.reshape(DB, D)
    outs = [[] for _ in range(8)]
    for l in range(depth):
        P = _prep_weights({n: w[l] for n, w in zip(names, stacked)}, D)
        rp, ql, kl, rope, md, gd, mh, nope = P["dims"]
        nh, hd, lora, rw = P["nh"], P["hd"], P["lora"], P["rw"]
        cos_p, sin_p = _rope_tables(jnp.arange(S), rope, mh)
        cos_s, sin_s = _rope_tables(jnp.full((DB,), past_len), rope, mh)
        tm_p, tm_s = _tile(S, 256), DB

        mkv = _memkv(mem_prompt.reshape(B * mem_tokens, D), P["mem_norm"], P["mem_wkv"])
        mk_p, mv_p = mkv[:, :md].reshape(B, mem_tokens, md), mkv[:, md:].reshape(B, mem_tokens, md)

        h = _ffn(xp, *P["ffn1"], tm_p)
        prw, q, rows, kbf, qmem, gates = _inproj(h, B, P["mix_pre"], P["win"], P["q_norm"], P["wqb"], P["wuk"],
                                                 P["kv_norm"], cos_p, sin_p, P["dims"], P["qscale"], tm_p)
        og, wkv_p = _rwkv_prompt(prw.reshape(B, S, rp), rw, nh, hd, lora)
        ctx = _attn_prompt(q, kbf.reshape(B, S, kl + rope), kl)
        om = _memattn_prompt(qmem, mk_p, mv_p, mem_heads, mem_scale, tm_p)
        h = _merge(h, B, og.reshape(B * S, nh * hd), ctx, om, gates, P["rwkv_wo"], P["wuv"], P["mla_wo"], P["mem_wo"],
                   P["w_out"], P["mix_post"], mh, tm_p)
        xp = _ffn(h, *P["ffn2"], tm_p)
        rows_p, shift_p = rows.reshape(B, S, kl + rope), prw.reshape(B, S, rp)[:, -1]

        h = _ffn(xs, *P["ffn1"], tm_s)
        prw, q, rows, kbf, qmem, gates = _inproj(h, 1, P["mix_pre"], P["win"], P["q_norm"], P["wqb"], P["wuk"],
                                                 P["kv_norm"], cos_s, sin_s, P["dims"], P["qscale"], tm_s)
        og, wkv_t = _rwkv_sample(prw, state_shift[l], state_t, l, rw, nh, hd, lora, _tile(nh, 2))
        wkv_s = jnp.transpose(wkv_t, (3, 0, 1, 2))
        ctx = _attn_sample(jnp.swapaxes(q[0], 0, 1), kbf.reshape(DB, 1, kl + rope), cache_t, l, page_table, kl)
        om = _memattn_sample(qmem, cache_mem_k, cache_mem_v, l, mem_scale, _tile(DB, 4))
        h = _merge(h, 1, og, jnp.swapaxes(ctx, 0, 1)[None], om, gates, P["rwkv_wo"], P["wuv"], P["mla_wo"], P["mem_wo"],
                   P["w_out"], P["mix_post"], mh, tm_s)
        xs = _ffn(h, *P["ffn2"], tm_s)

        for lst, val in zip(outs, (rows_p, rows.reshape(DB, T, kl + rope), wkv_p, wkv_s, shift_p, prw,
                                   mk_p.reshape(B, mem_tokens, mem_heads, mem_hd),
                                   mv_p.reshape(B, mem_tokens, mem_heads, mem_hd))):
            lst.append(val)
    return (xp.reshape(B, S, D), xs.reshape(DB, T, D)) + tuple(jnp.stack(o) for o in outs)
```

```python
import functools

import jax
import jax.numpy as jnp
from jax import lax
from jax.experimental import pallas as pl
from jax.experimental.pallas import tpu as pltpu

F32, BF16 = jnp.float32, jnp.bfloat16
RMS_EPS = 1e-6
LNX_EPS = 64e-5
ROPE_BASE = 10000.0
LANES = 128
VMEM_LIMIT = 52 * 1024 * 1024
RWKV_CHUNK = 64
RWKV_CHUNKS_PER_STEP = 2
ATT_TQ = 128
ATT_TK = 512
ATT_ROW_GROUP = 512
PAGES_PER_STEP = 32
LOG2E = 1.4426950408889634


def _cparams(*sem):
    return pltpu.CompilerParams(dimension_semantics=sem, vmem_limit_bytes=VMEM_LIMIT)


def _resident(shape):
    nd = len(shape)
    return pl.BlockSpec(shape, lambda *_: (0,) * nd, pipeline_mode=pl.Buffered(1))


def _rms(x, g):
    return x * lax.rsqrt(jnp.mean(x * x, axis=-1, keepdims=True) + RMS_EPS) * g


def _sigmoid(x):
    return 1.0 / (1.0 + jnp.exp(-x))


def _mm(a, b):
    return jnp.dot(a.astype(BF16), b.astype(BF16), preferred_element_type=F32)


def _mm_nt(a, b):
    return lax.dot_general(a.astype(BF16), b.astype(BF16), (((1,), (1,)), ((), ())), preferred_element_type=F32)


def _ffn_body(x_ref, pre_ref, post_ref, wg_ref, wu_ref, wd_ref, o_ref):
    x = x_ref[...]
    h = _rms(x, pre_ref[...]).astype(BF16)
    g = jnp.dot(h, wg_ref[...], preferred_element_type=F32)
    u = jnp.dot(h, wu_ref[...], preferred_element_type=F32)
    act = (g * _sigmoid(g)) * u
    y = jnp.dot(act.astype(BF16), wd_ref[...], preferred_element_type=F32)
    o_ref[...] = x + 0.5 * _rms(y, post_ref[...])


def _ffn(x, pre, post, wg, wu, wd, tm):
    n, d = x.shape
    f = wg.shape[1]
    return pl.pallas_call(
        _ffn_body,
        grid=(n // tm,),
        in_specs=[pl.BlockSpec((tm, d), lambda i: (i, 0)), _resident((1, d)), _resident((1, d)),
                  _resident((d, f)), _resident((d, f)), _resident((f, d))],
        out_specs=pl.BlockSpec((tm, d), lambda i: (i, 0)),
        out_shape=jax.ShapeDtypeStruct((n, d), F32),
        compiler_params=_cparams("parallel"),
        name="ffn",
    )(x, pre, post, wg, wu, wd)


def _inproj_body(dims, qscale, h_ref, pre_ref, win_ref, qn_ref, wqb_ref, wuk_ref, kvn_ref, cos_ref, sin_ref,
                 prw_ref, q_ref, rows_ref, kbf_ref, qmem_ref, gates_ref):
    rp, ql, kl, rope, md, gd, nh, nope = dims
    u = _rms(h_ref[...], pre_ref[...]).astype(BF16)
    p = jnp.dot(u, win_ref[...], preferred_element_type=F32)
    o = 0
    prw_ref[...] = p[:, o:o + rp]; o += rp
    cq = p[:, o:o + ql]; o += ql
    ckv = p[:, o:o + kl]; o += kl
    qmem_ref[...] = p[:, o:o + md].astype(BF16); o += md
    gates_ref[...] = _sigmoid(p[:, o:o + gd]); o += gd
    kpe = p[:, o:o + rope]; o += rope
    kpe_sw = p[:, o:o + rope]
    cos = cos_ref[...]
    sin = sin_ref[...]
    q = jnp.dot(_rms(cq, qn_ref[...]).astype(BF16), wqb_ref[...], preferred_element_type=F32)
    nn = nh * nope
    nr = nh * rope
    qpe = ((q[:, nn:nn + nr] * cos + q[:, nn + nr:nn + 2 * nr] * sin) * qscale).astype(BF16)
    qn = q[:, :nn].astype(BF16)
    for pr in range(nh // 2):
        qlat = (jnp.dot(qn[:, LANES * pr:LANES * (pr + 1)], wuk_ref[pr], preferred_element_type=F32) * qscale).astype(BF16)
        for e in range(2):
            hh = 2 * pr + e
            q_ref[0, hh, :, 0:kl] = qlat[:, kl * e:kl * (e + 1)]
            q_ref[0, hh, :, kl:kl + rope] = qpe[:, rope * hh:rope * (hh + 1)]
    ckvn = _rms(ckv, kvn_ref[...])
    kper = kpe * cos[:, :rope] + kpe_sw * sin[:, :rope]
    rows_ref[:, 0:kl] = ckvn
    rows_ref[:, kl:kl + rope] = kper
    kbf_ref[:, 0:kl] = ckvn.astype(BF16)
    kbf_ref[:, kl:kl + rope] = kper.astype(BF16)


def _inproj(h, nbatch, pre, win, qn, wqb, wuk, kvn, cos, sin, dims, qscale, tm):
    n, d = h.shape
    rp, ql, kl, rope, md, gd, nh, nope = dims
    t = n // nbatch
    nb = t // tm
    cw = win.shape[1]
    row = lambda i: (i, 0)
    tab = lambda i: (i % nb, 0)
    return pl.pallas_call(
        functools.partial(_inproj_body, dims, qscale),
        grid=(n // tm,),
        in_specs=[pl.BlockSpec((tm, d), row), _resident((1, d)), _resident((d, cw)), _resident((1, ql)),
                  _resident(wqb.shape), _resident(wuk.shape), _resident((1, kl)),
                  pl.BlockSpec((tm, nh * rope), tab), pl.BlockSpec((tm, nh * rope), tab)],
        out_specs=[pl.BlockSpec((tm, rp), row),
                   pl.BlockSpec((1, nh, tm, kl + rope), lambda i: (i // nb, 0, i % nb, 0)),
                   pl.BlockSpec((tm, kl + rope), row), pl.BlockSpec((tm, kl + rope), row),
                   pl.BlockSpec((tm, md), row), pl.BlockSpec((tm, gd), row)],
        out_shape=[jax.ShapeDtypeStruct((n, rp), F32),
                   jax.ShapeDtypeStruct((nbatch, nh, t, kl + rope), BF16),
                   jax.ShapeDtypeStruct((n, kl + rope), F32),
                   jax.ShapeDtypeStruct((n, kl + rope), BF16),
                   jax.ShapeDtypeStruct((n, md), BF16),
                   jax.ShapeDtypeStruct((n, gd), F32)],
        compiler_params=_cparams("parallel"),
        name="inproj",
    )(h, pre, win, qn, wqb, wuk, kvn, cos, sin)


def _segsum(x, e):
    hi = x.astype(BF16)
    lo = (x - hi.astype(F32)).astype(BF16)
    return jnp.dot(hi, e, preferred_element_type=F32) + jnp.dot(lo, e, preferred_element_type=F32)


def _rwkv_prep(p, prev, mu, w0, w2a2, a0, g2, k_k, k_a, e, rd, lora):
    ps = p + (prev - p) * mu
    r = ps[:, 0:rd]
    k = ps[:, rd:2 * rd]
    v = ps[:, 2 * rd:3 * rd]
    wa = ps[:, 3 * rd:3 * rd + 2 * lora]
    gl = ps[:, 3 * rd + 2 * lora:]
    lane = lax.broadcasted_iota(jnp.int32, wa.shape, 1)
    wa = jnp.where(lane < lora, jnp.tanh(wa), wa)
    wa2 = _mm(wa, w2a2)
    x = -(w0 + wa2[:, :rd])
    softplus = jnp.maximum(x, 0.0) + jnp.log(1.0 + jnp.exp(-jnp.abs(x)))
    logdec = -jnp.exp(-softplus - 0.5)
    a = _sigmoid(a0 + wa2[:, rd:])
    g = _mm(_sigmoid(gl), g2)
    kk = k * k_k
    kk = kk / jnp.maximum(jnp.sqrt(_segsum(kk * kk, e)), 1e-12)
    k = k * (1.0 + (a - 1.0) * k_a)
    return r, k, v, logdec, -kk, kk * a, g


def _rwkv_post(o, r, k, v, g, r_k, lng, lnb, e, hd):
    mean = _segsum(o, e) * (1.0 / hd)
    oc = o - mean
    var = _segsum(oc * oc, e) * (1.0 / hd)
    o = oc * lax.rsqrt(var + LNX_EPS) * lng + lnb
    bonus = _segsum(r * k * r_k, e) * v
    return (o + bonus) * g


def _pair_rows(y):
    lo = (lax.broadcasted_iota(jnp.int32, y.shape, 1) % LANES) < (LANES // 2)
    z = jnp.zeros_like(y)
    return jnp.concatenate([jnp.where(lo, y, z), jnp.where(lo, z, y)], axis=0)


def _rwkv_chunks_local(items, strict, incl):
    L = items[0][0].shape[0]
    cs = []
    for r, k, v, ld, cum, a, b in items:
        cum_l = cum[L - 1:L, :]
        e_neg = jnp.exp(-cum)
        e_end = jnp.exp(cum_l - cum)
        kt, bt = (k * e_neg).astype(BF16), (b * e_neg).astype(BF16)
        at, rt = a * jnp.exp(cum - ld), (r * jnp.exp(cum)).astype(BF16)
        cs.append(dict(at=at, rt=rt, vb=_pair_rows(v.astype(BF16)), v=v, dl=jnp.exp(cum_l),
                       bkh=jnp.concatenate([b * e_end, k * e_end], axis=0).astype(BF16),
                       lhs=jnp.concatenate([at.astype(BF16), rt], axis=0),
                       rhs=jnp.concatenate([_pair_rows(bt), _pair_rows(kt)], axis=0)))
    for c in cs:
        mm = _mm_nt(c.pop("lhs"), c.pop("rhs"))
        c["pw"] = jnp.where(strict, mm[:L, :2 * L], 0.0)
        c["m_ka"] = jnp.where(strict, mm[:L, 2 * L:], 0.0)
        c["m_r"] = jnp.concatenate([jnp.where(incl, mm[L:, :2 * L], 0.0), jnp.where(incl, mm[L:, 2 * L:], 0.0)],
                                   axis=1).astype(BF16)
    for c in cs:
        c["x"] = jnp.concatenate([c.pop("at"), _mm(c.pop("m_ka"), c["vb"])], axis=1)
    span = 1
    while span < L:
        for c in cs:
            c["x"] = c["x"] + _mm(c["pw"], _pair_rows(c["x"].astype(BF16)))
        span *= 2
        if span < L:
            for c in cs:
                c["pw"] = _mm(c["pw"], _pair_rows(c["pw"].astype(BF16)))
    for c in cs:
        x = c.pop("x")
        c["w1"], c["uloc"] = x[:, :LANES].astype(BF16), x[:, LANES:]
    return cs


def _rwkv_chunks_apply(cs, states, diag):
    sbs = [s.astype(BF16) for s in states]
    urs = [_mm_nt(c["w1"], sb) + c["uloc"] for c, sb in zip(cs, sbs)]
    o1 = [_mm_nt(c["rt"], sb) for c, sb in zip(cs, sbs)]
    upds = [_mm(jnp.concatenate([ur, c["v"]], axis=0).T, c["bkh"]) for c, ur in zip(cs, urs)]
    o2 = [_mm(c["m_r"], jnp.concatenate([_pair_rows(ur.astype(BF16)), c["vb"]], axis=0)) for c, ur in zip(cs, urs)]
    return [(a + b, s * c["dl"] + jnp.where(diag, u, 0.0)) for a, b, s, c, u in zip(o1, o2, states, cs, upds)]


def _rwkv_prompt_body(rd, lora, hd, L, p_ref, mu_ref, w0_ref, w2a2_ref, a0_ref, g2_ref, kk_ref, ka_ref, rk_ref,
                      lng_ref, lnb_ref, e_ref, og_ref, st_ref, prev_ref, s_ref):
    step = pl.program_id(0)
    nb, rows, _ = p_ref.shape
    npair = rd // LANES

    @pl.when(step == 0)
    def _():
        prev_ref[...] = jnp.zeros_like(prev_ref)
        s_ref[...] = jnp.zeros_like(s_ref)

    e = e_ref[...]
    ti = lax.broadcasted_iota(jnp.int32, (L, 2 * L), 0)
    si = lax.broadcasted_iota(jnp.int32, (L, 2 * L), 1) % L
    strict, incl = si < ti, si <= ti
    half = LANES // 2
    diag = ((lax.broadcasted_iota(jnp.int32, (LANES, LANES), 0) < half)
            == (lax.broadcasted_iota(jnp.int32, (LANES, LANES), 1) < half))
    tr = lax.broadcasted_iota(jnp.int32, (rows, rows), 0)
    tc = lax.broadcasted_iota(jnp.int32, (rows, rows), 1)
    tri = ((tc <= tr) & (tc // L == tr // L)).astype(BF16)
    rowi = lax.broadcasted_iota(jnp.int32, (rows, p_ref.shape[2]), 0)

    keys, items, vecs = [], [], []
    for bi in range(nb):
        p = p_ref[bi]
        prev = jnp.where(rowi == 0, prev_ref[bi], pltpu.roll(p, 1, axis=0))
        prev_ref[bi] = p[rows - 1:rows, :]
        r, k, v, ld, a, b, g = _rwkv_prep(p, prev, mu_ref[...], w0_ref[...], w2a2_ref[...], a0_ref[...], g2_ref[...],
                                          kk_ref[...], ka_ref[...], e, rd, lora)
        hi = ld.astype(BF16)
        r1 = ld - hi.astype(F32)
        mid = r1.astype(BF16)
        lo = (r1 - mid.astype(F32)).astype(BF16)
        cum = (jnp.dot(tri, hi, preferred_element_type=F32) + jnp.dot(tri, mid, preferred_element_type=F32)
               + jnp.dot(tri, lo, preferred_element_type=F32))
        vecs.append((r, k, v, g))
        for cc in range(rows // L):
            for pr in range(npair):
                sl = (slice(L * cc, L * (cc + 1)), slice(LANES * pr, LANES * (pr + 1)))
                keys.append((bi, cc, pr))
                items.append((r[sl], k[sl], v[sl], ld[sl], cum[sl], a[sl], b[sl]))
    local = dict(zip(keys, _rwkv_chunks_local(items, strict, incl)))
    chains = [(bi, pr) for bi in range(nb) for pr in range(npair)]
    state = {ch: s_ref[ch[0] * npair + ch[1]] for ch in chains}
    outs = {}
    for cc in range(rows // L):
        new = _rwkv_chunks_apply([local[bi, cc, pr] for bi, pr in chains], [state[ch] for ch in chains], diag)
        for ch, (o, s) in zip(chains, new):
            outs[ch, cc], state[ch] = o, s
    for bi in range(nb):
        for pr in range(npair):
            s_ref[bi * npair + pr] = state[bi, pr]
        o = jnp.concatenate([jnp.concatenate([outs[(bi, pr), cc] for cc in range(rows // L)], axis=0)
                             for pr in range(npair)], axis=1)
        r, k, v, g = vecs[bi]
        og_ref[bi] = _rwkv_post(o, r, k, v, g, rk_ref[...], lng_ref[...], lnb_ref[...], e, hd).astype(BF16)

    @pl.when(step == pl.num_programs(0) - 1)
    def _():
        for bi in range(nb):
            for pr in range(npair):
                s = s_ref[bi * npair + pr]
                st_ref[bi, 2 * pr] = s[:hd, :hd]
                st_ref[bi, 2 * pr + 1] = s[hd:, hd:]


def _rwkv_prompt(prw, rw, nh, hd, lora):
    b, t, pw = prw.shape
    rd = nh * hd
    rows = _tile(t, RWKV_CHUNK * RWKV_CHUNKS_PER_STEP)
    names = ("mu", "w0", "w2a2", "a0", "g2", "k_k", "k_a", "r_k", "lnx_g", "lnx_b", "e")
    return pl.pallas_call(
        functools.partial(_rwkv_prompt_body, rd, lora, hd, RWKV_CHUNK),
        grid=(t // rows,),
        in_specs=[pl.BlockSpec((b, rows, pw), lambda c: (0, c, 0))] + [_resident(rw[k].shape) for k in names],
        out_specs=[pl.BlockSpec((b, rows, rd), lambda c: (0, c, 0)),
                   pl.BlockSpec((b, nh, hd, hd), lambda c: (0, 0, 0, 0))],
        out_shape=[jax.ShapeDtypeStruct((b, t, rd), BF16), jax.ShapeDtypeStruct((b, nh, hd, hd), F32)],
        scratch_shapes=[pltpu.VMEM((b, 1, pw), F32), pltpu.VMEM((b * (rd // LANES), LANES, LANES), F32)],
        compiler_params=_cparams("arbitrary"),
        name="rwkv_prompt",
    )(prw, *[rw[k] for k in names])


def _rwkv_prep_body(rd, lora, p_ref, prev_ref, mu_ref, w0_ref, w2a2_ref, a0_ref, g2_ref, kk_ref, ka_ref, e_ref,
                    r_ref, k_ref, v_ref, g_ref, *t_refs):
    r, k, v, ld, a, b, g = _rwkv_prep(p_ref[...], prev_ref[...], mu_ref[...], w0_ref[...], w2a2_ref[...], a0_ref[...],
                                      g2_ref[...], kk_ref[...], ka_ref[...], e_ref[...], rd, lora)
    r_ref[...] = r
    k_ref[...] = k
    v_ref[...] = v
    g_ref[...] = g
    for ref, x in zip(t_refs, (r, k, v, jnp.exp(ld), a, b)):
        ref[...] = x.T


def _rwkv_step_body(s_ref, r_ref, k_ref, v_ref, w_ref, a_ref, b_ref, so_ref, o_ref):
    for h in range(s_ref.shape[0]):
        s = s_ref[h]
        sa = jnp.sum(s * a_ref[h][None], axis=1)
        s = s * w_ref[h][None] + sa[:, None, :] * b_ref[h][None] + v_ref[h][:, None, :] * k_ref[h][None]
        so_ref[h] = s
        o_ref[h] = jnp.sum(s * r_ref[h][None], axis=1)


def _rwkv_post_body(hd, o_ref, r_ref, k_ref, v_ref, g_ref, rk_ref, lng_ref, lnb_ref, e_ref, og_ref):
    og_ref[...] = _rwkv_post(o_ref[...].T, r_ref[...], k_ref[...], v_ref[...], g_ref[...], rk_ref[...], lng_ref[...],
                             lnb_ref[...], e_ref[...], hd).astype(BF16)


def _rwkv_sample(prw, shift, state_t, layer, rw, nh, hd, lora, hb):
    n, pw = prw.shape
    rd = nh * hd
    names = ("mu", "w0", "w2a2", "a0", "g2", "k_k", "k_a", "e")
    full = lambda s: pl.BlockSpec(s, lambda: (0,) * len(s))
    vecs = pl.pallas_call(
        functools.partial(_rwkv_prep_body, rd, lora),
        in_specs=[full((n, pw)), full((n, pw))] + [full(rw[k].shape) for k in names],
        out_specs=[full((n, rd))] * 4 + [full((rd, n))] * 6,
        out_shape=[jax.ShapeDtypeStruct((n, rd), F32)] * 4 + [jax.ShapeDtypeStruct((rd, n), F32)] * 6,
        name="rwkv_prep",
    )(prw, shift, *[rw[k] for k in names])
    r, k, v, g = vecs[:4]
    vspec = pl.BlockSpec((hb, hd, n), lambda i: (i, 0, 0))
    sspec = pl.BlockSpec((hb, hd, hd, n), lambda i: (i, 0, 0, 0))
    s_new, o = pl.pallas_call(
        _rwkv_step_body,
        grid=(nh // hb,),
        in_specs=[pl.BlockSpec((None, hb, hd, hd, n), lambda i: (layer, i, 0, 0, 0))] + [vspec] * 6,
        out_specs=[sspec, vspec],
        out_shape=[jax.ShapeDtypeStruct(state_t.shape[1:], F32), jax.ShapeDtypeStruct((nh, hd, n), F32)],
        compiler_params=_cparams("parallel"),
        name="rwkv_step",
    )(state_t, *[x.reshape(nh, hd, n) for x in vecs[4:]])
    pnames = ("r_k", "lnx_g", "lnx_b", "e")
    og = pl.pallas_call(
        functools.partial(_rwkv_post_body, hd),
        in_specs=[full((rd, n))] + [full((n, rd))] * 4 + [full(rw[k].shape) for k in pnames],
        out_specs=full((n, rd)),
        out_shape=jax.ShapeDtypeStruct((n, rd), BF16),
        name="rwkv_post",
    )(o.reshape(rd, n), r, k, v, g, *[rw[k] for k in pnames])
    return og, s_new


def _lanes(x, n):
    return x if n == LANES else jnp.concatenate([x] * (n // LANES), axis=1)


def _attn_prompt_body(kl, tk, rg, q_ref, k_ref, o_ref, m_ref, l_ref, acc_ref, s_ref):
    i = pl.program_id(1)
    nh, tq, dk = q_ref.shape[1:]
    rows = nh * tq
    q = q_ref[0].reshape(rows, dk)
    groups = [slice(g * rg, (g + 1) * rg) for g in range(rows // rg)]
    m_ref[...] = jnp.full_like(m_ref, -jnp.inf)
    l_ref[...] = jnp.zeros_like(l_ref)
    acc_ref[...] = jnp.zeros_like(acc_ref)

    def keys(j):
        return k_ref[0, pl.ds(pl.multiple_of(j * tk, tk), tk), :]

    def scores(r, k):
        return lax.dot_general(q[r], k, (((1,), (1,)), ((), ())), preferred_element_type=F32)

    def update(r, s, v):
        m_old = m_ref[r]
        m_new = jnp.maximum(m_old, jnp.max(s, axis=-1, keepdims=True))
        alpha = jnp.exp2(m_old - m_new)
        p = jnp.exp2(s - _lanes(m_new, tk))
        l_ref[r] = alpha * l_ref[r] + jnp.sum(p, axis=-1, keepdims=True)
        acc_ref[r] = _lanes(alpha, kl) * acc_ref[r] + jnp.dot(p.astype(BF16), v, preferred_element_type=F32)
        m_ref[r] = m_new

    k0 = keys(0)
    for r in groups:
        s_ref[r] = scores(r, k0)

    def body(j, carry):
        v = keys(j)[:, :kl]
        k_next = keys(j + 1)
        for r in groups:
            s = s_ref[r]
            s_ref[r] = scores(r, k_next)
            update(r, s, v)
        return carry

    last = (i * tq) // tk
    lax.fori_loop(0, last, body, 0)
    v = keys(last)[:, :kl]
    for g, r in enumerate(groups):
        s = s_ref[r]
        qpos = i * tq + (g * rg + lax.broadcasted_iota(jnp.int32, s.shape, 0)) % tq
        kpos = last * tk + lax.broadcasted_iota(jnp.int32, s.shape, 1)
        update(r, jnp.where(kpos <= qpos, s, -jnp.inf), v)
    o_ref[0] = (acc_ref[...] / _lanes(l_ref[...], kl)).reshape(nh, tq, kl).astype(o_ref.dtype)


def _attn_prompt(q, kbf, kl):
    b, nh, t, dk = q.shape
    tq, tk = _tile(t, ATT_TQ), _tile(t, ATT_TK)
    rows = nh * tq
    rg = _tile(rows, ATT_ROW_GROUP)
    assert tk % tq == 0 and rg % tq == 0
    return pl.pallas_call(
        functools.partial(_attn_prompt_body, kl, tk, rg),
        grid=(b, t // tq),
        in_specs=[pl.BlockSpec((1, nh, tq, dk), lambda bi, i: (bi, 0, i, 0)),
                  pl.BlockSpec((1, t, dk), lambda bi, i: (bi, 0, 0))],
        out_specs=pl.BlockSpec((1, nh, tq, kl), lambda bi, i: (bi, 0, i, 0)),
        out_shape=jax.ShapeDtypeStruct((b, nh, t, kl), BF16),
        scratch_shapes=[pltpu.VMEM((rows, LANES), F32), pltpu.VMEM((rows, LANES), F32), pltpu.VMEM((rows, kl), F32),
                        pltpu.VMEM((rows, tk), F32)],
        compiler_params=_cparams("parallel", "arbitrary"),
        name="attn_prompt",
    )(q, kbf)


def _attn_sample_body(kl, npg, pt_ref, q_ref, kself_ref, *rest):
    pages, (o_ref, m_ref, l_ref, acc_ref) = rest[:npg], rest[npg:]
    j = pl.program_id(1)
    q = q_ref[0]

    @pl.when(j == 0)
    def _():
        ks = kself_ref[0]
        m_ref[...] = jnp.sum(q.astype(F32) * ks.astype(F32), axis=-1, keepdims=True)
        l_ref[...] = jnp.ones_like(l_ref)
        acc_ref[...] = jnp.broadcast_to(ks[:, :kl].astype(F32), acc_ref.shape)

    grp = 2 if npg % 2 == 0 else 1
    kts = [jnp.concatenate([pages[n + e][0] for e in range(grp)], axis=1).astype(BF16) for n in range(0, npg, grp)]
    s = jnp.concatenate([jnp.dot(q, kt, preferred_element_type=F32) for kt in kts], axis=1)
    m_old = m_ref[...]
    m_new = jnp.maximum(m_old, jnp.max(s, axis=-1, keepdims=True))
    alpha = jnp.exp2(m_old - m_new)
    p = jnp.exp2(s - m_new)
    l_ref[...] = alpha * l_ref[...] + jnp.sum(p, axis=-1, keepdims=True)
    pb = p.astype(BF16)
    w = kts[0].shape[1]
    pv = _mm_nt(pb[:, :w], kts[0][:kl, :])
    for n in range(1, len(kts)):
        pv = pv + _mm_nt(pb[:, n * w:(n + 1) * w], kts[n][:kl, :])
    acc_ref[...] = alpha * acc_ref[...] + pv
    m_ref[...] = m_new

    @pl.when(j == pl.num_programs(1) - 1)
    def _():
        o_ref[0] = (acc_ref[...] / l_ref[...]).astype(o_ref.dtype)


def _attn_sample(q, kself, cache_t, layer, page_table, kl):
    n, nh, dk = q.shape
    ps = cache_t.shape[3]
    npages = page_table.shape[1]
    npg = _tile(npages, PAGES_PER_STEP)

    def page_spec(kk):
        return pl.BlockSpec((None, 1, dk, ps), lambda bi, j, pt: (layer, pt[bi, j * npg + kk], 0, 0))

    grid_spec = pltpu.PrefetchScalarGridSpec(
        num_scalar_prefetch=1,
        grid=(n, npages // npg),
        in_specs=[pl.BlockSpec((1, nh, dk), lambda bi, j, pt: (bi, 0, 0)),
                  pl.BlockSpec((1, 1, dk), lambda bi, j, pt: (bi, 0, 0))] + [page_spec(kk) for kk in range(npg)],
        out_specs=pl.BlockSpec((1, nh, kl), lambda bi, j, pt: (bi, 0, 0)),
        scratch_shapes=[pltpu.VMEM((nh, 1), F32), pltpu.VMEM((nh, 1), F32), pltpu.VMEM((nh, kl), F32)],
    )
    return pl.pallas_call(
        functools.partial(_attn_sample_body, kl, npg),
        grid_spec=grid_spec,
        out_shape=jax.ShapeDtypeStruct((n, nh, kl), BF16),
        compiler_params=_cparams("parallel", "arbitrary"),
        name="attn_sample",
    )(page_table, q, kself, *([cache_t] * npg))


def _memkv_body(m_ref, g_ref, w_ref, o_ref):
    o_ref[...] = jnp.dot(_rms(m_ref[...], g_ref[...]).astype(BF16), w_ref[...], preferred_element_type=F32)


def _memkv(mem, g, wkv):
    n, d = mem.shape
    full = lambda s: pl.BlockSpec(s, lambda: (0,) * len(s))
    return pl.pallas_call(
        _memkv_body,
        in_specs=[full((n, d)), full((1, d)), full(wkv.shape)],
        out_specs=full((n, wkv.shape[1])),
        out_shape=jax.ShapeDtypeStruct((n, wkv.shape[1]), F32),
        name="mem_kv",
    )(mem, g, wkv)


def _mem_attend(q, k, v, scale):
    s = _mm_nt(q, k) * scale
    p = jnp.exp(s - jnp.max(s, axis=-1, keepdims=True))
    p = p / jnp.sum(p, axis=-1, keepdims=True)
    return _mm(p, v)


def _memattn_prompt_body(nh, scale, q_ref, k_ref, v_ref, o_ref):
    hd = q_ref.shape[1] // nh
    q, k, v = q_ref[...], k_ref[0], v_ref[0]
    heads = [slice(h * hd, (h + 1) * hd) for h in range(nh)]
    o_ref[...] = jnp.concatenate([_mem_attend(q[:, sl], k[:, sl], v[:, sl], scale) for sl in heads],
                                 axis=1).astype(o_ref.dtype)


def _memattn_prompt(q, mk, mv, nh, scale, tm):
    n, md = q.shape
    b, m, _ = mk.shape
    nb = (n // b) // tm
    kv = pl.BlockSpec((1, m, md), lambda i: (i // nb, 0, 0))
    return pl.pallas_call(
        functools.partial(_memattn_prompt_body, nh, scale),
        grid=(n // tm,),
        in_specs=[pl.BlockSpec((tm, md), lambda i: (i, 0)), kv, kv],
        out_specs=pl.BlockSpec((tm, md), lambda i: (i, 0)),
        out_shape=jax.ShapeDtypeStruct((n, md), BF16),
        compiler_params=_cparams("parallel"),
        name="memattn_prompt",
    )(q, mk, mv)


def _memattn_sample_body(scale, q_ref, k_ref, v_ref, o_ref):
    gr, rows, _ = k_ref.shape
    nh = q_ref.shape[1]
    own = (lax.broadcasted_iota(jnp.int32, (nh, rows), 1) % nh) == lax.broadcasted_iota(jnp.int32, (nh, rows), 0)
    ss = [jnp.where(own, _mm_nt(q_ref[g], k_ref[g]) * scale, -jnp.inf) for g in range(gr)]
    ps = [jnp.exp(s - jnp.max(s, axis=-1, keepdims=True)) for s in ss]
    ps = [p / jnp.sum(p, axis=-1, keepdims=True) for p in ps]
    for g in range(gr):
        o_ref[g] = _mm(ps[g], v_ref[g]).astype(o_ref.dtype)


def _memattn_sample(q, cache_k, cache_v, layer, scale, gr):
    n, md = q.shape
    depth, _, m, nh, hd = cache_k.shape
    kv = pl.BlockSpec((None, gr, m * nh, hd), lambda i: (layer, i, 0, 0))
    qs = pl.BlockSpec((gr, nh, hd), lambda i: (i, 0, 0))
    return pl.pallas_call(
        functools.partial(_memattn_sample_body, scale),
        grid=(n // gr,),
        in_specs=[qs, kv, kv],
        out_specs=qs,
        out_shape=jax.ShapeDtypeStruct((n, nh, hd), BF16),
        compiler_params=_cparams("parallel"),
        name="memattn_sample",
    )(q.reshape(n, nh, hd), cache_k.reshape(depth, n, m * nh, hd), cache_v.reshape(depth, n, m * nh, hd)).reshape(n, md)


def _merge_body(nh, h_ref, og_ref, ctx_ref, om_ref, gates_ref, wo_ref, wuv_ref, mwo_ref, memwo_ref, wout_ref, post_ref,
                o_ref):
    d = h_ref.shape[1]
    o_rwkv = jnp.dot(og_ref[...], wo_ref[...], preferred_element_type=F32)
    vs = []
    for pr in range(nh // 2):
        vp = (jnp.dot(ctx_ref[0, 2 * pr], wuv_ref[2 * pr], preferred_element_type=F32)
              + jnp.dot(ctx_ref[0, 2 * pr + 1], wuv_ref[2 * pr + 1], preferred_element_type=F32))
        vs.append(vp.astype(BF16))
    o_mla = jnp.dot(jnp.concatenate(vs, axis=1), mwo_ref[...], preferred_element_type=F32)
    o_mem = jnp.dot(om_ref[...], memwo_ref[...], preferred_element_type=F32)
    merged = gates_ref[:, 0:d] * o_rwkv + gates_ref[:, d:2 * d] * o_mla + gates_ref[:, 2 * d:3 * d] * o_mem
    y = jnp.dot(merged.astype(BF16), wout_ref[...], preferred_element_type=F32)
    o_ref[...] = h_ref[...] + _rms(y, post_ref[...])


def _merge(h, nbatch, og, ctx, om, gates, wo, wuv, mwo, memwo, wout, post, nh, tm):
    n, d = h.shape
    nb = (n // nbatch) // tm
    row = lambda i: (i, 0)
    kl = ctx.shape[-1]
    return pl.pallas_call(
        functools.partial(_merge_body, nh),
        grid=(n // tm,),
        in_specs=[pl.BlockSpec((tm, d), row), pl.BlockSpec((tm, og.shape[1]), row),
                  pl.BlockSpec((1, nh, tm, kl), lambda i: (i // nb, 0, i % nb, 0)),
                  pl.BlockSpec((tm, om.shape[1]), row), pl.BlockSpec((tm, 3 * d), row),
                  _resident(wo.shape), _resident(wuv.shape), _resident(mwo.shape), _resident(memwo.shape),
                  _resident(wout.shape), _resident((1, d))],
        out_specs=pl.BlockSpec((tm, d), row),
        out_shape=jax.ShapeDtypeStruct((n, d), F32),
        compiler_params=_cparams("parallel"),
        name="merge",
    )(h, og, ctx, om, gates, wo, wuv, mwo, memwo, wout, post)


def _rope_tables(pos, rope, nh):
    half = rope // 2
    freqs = ROPE_BASE ** (-jnp.arange(half, dtype=F32) / half)
    ang = pos.astype(F32)[:, None] * freqs
    cos, sin = jnp.cos(ang), jnp.sin(ang)
    return jnp.tile(jnp.concatenate([cos, cos], axis=1), (1, nh)), jnp.tile(jnp.concatenate([-sin, sin], axis=1), (1, nh))


def _prep_weights(W, d):
    nh, hd = W["rwkv_r_k"].shape
    rd = nh * hd
    lora = W["rwkv_w2"].shape[0]
    glora = W["rwkv_g2"].shape[0]
    rp = 3 * rd + 2 * lora + glora
    ql = W["mla_q_norm"].shape[0]
    kl, mh, vh = W["mla_w_uv"].shape
    nope = W["mla_w_uk"].shape[2]
    rope = W["mla_w_qb"].shape[1] // mh - nope
    md = W["mem_w_k"].shape[1]
    half = rope // 2
    row = lambda x: x.reshape(1, -1)
    w_in = W["w_in"]
    o_cq, o_kv, o_pe, o_mem, o_g = rp, rp + ql, rp + ql + kl, rp + ql + kl + rope, rp + ql + kl + rope + md
    cols = [w_in[:, :o_pe], w_in[:, o_mem:], w_in[:, o_pe:o_mem],
            w_in[:, o_pe + half:o_mem], w_in[:, o_pe:o_pe + half]]
    width = sum(c.shape[1] for c in cols)
    pad = (-width) % LANES
    win = jnp.concatenate(cols + [jnp.zeros((d, pad), F32)], axis=1).astype(BF16)
    wqb = W["mla_w_qb"].reshape(ql, mh, nope + rope)
    wqb = jnp.concatenate([wqb[:, :, :nope].reshape(ql, mh * nope),
                           wqb[:, :, nope:].reshape(ql, mh * rope),
                           jnp.concatenate([wqb[:, :, nope + half:], wqb[:, :, nope:nope + half]], axis=2).reshape(ql, mh * rope)],
                          axis=1).astype(BF16)
    ukt = jnp.transpose(W["mla_w_uk"], (1, 2, 0))
    z = jnp.zeros_like(ukt[0])
    wuk = jnp.stack([jnp.concatenate([jnp.concatenate([ukt[2 * p], z], axis=1),
                                      jnp.concatenate([z, ukt[2 * p + 1]], axis=1)], axis=0)
                     for p in range(mh // 2)]).astype(BF16)
    uv = jnp.transpose(W["mla_w_uv"], (1, 0, 2))
    zv = jnp.zeros_like(uv[0])
    wuv = jnp.stack([jnp.concatenate([uv[h], zv] if h % 2 == 0 else [zv, uv[h]], axis=1)
                     for h in range(mh)]).astype(BF16)
    zl = jnp.zeros((lora, rd), F32)
    w2a2 = jnp.concatenate([jnp.concatenate([W["rwkv_w2"], zl], axis=1),
                            jnp.concatenate([zl, W["rwkv_a2"]], axis=1)], axis=0).astype(BF16)
    hid = jnp.arange(rd) // hd
    rw = dict(mu=row(W["rwkv_mu"]), w0=row(W["rwkv_w0"]), w2a2=w2a2, a0=row(W["rwkv_a0"]), g2=W["rwkv_g2"].astype(BF16),
              k_k=row(W["rwkv_k_k"]), k_a=row(W["rwkv_k_a"]), r_k=row(W["rwkv_r_k"]), lnx_g=row(W["rwkv_lnx_g"]),
              lnx_b=row(W["rwkv_lnx_b"]), e=(hid[:, None] == hid[None, :]).astype(BF16))
    dims = (rp, ql, kl, rope, md, 3 * d, mh, nope)
    return dict(
        dims=dims, nh=nh, hd=hd, lora=lora, rw=rw, win=win, wqb=wqb, wuk=wuk, wuv=wuv,
        ffn1=(row(W["ffn1_pre"]), row(W["ffn1_post"]), W["ffn1_gate"].astype(BF16), W["ffn1_up"].astype(BF16),
              W["ffn1_down"].astype(BF16)),
        ffn2=(row(W["ffn2_pre"]), row(W["ffn2_post"]), W["ffn2_gate"].astype(BF16), W["ffn2_up"].astype(BF16),
              W["ffn2_down"].astype(BF16)),
        mix_pre=row(W["mix_pre"]), mix_post=row(W["mix_post"]), q_norm=row(W["mla_q_norm"]), kv_norm=row(W["mla_kv_norm"]),
        mem_norm=row(W["mem_norm"]), mem_wkv=jnp.concatenate([W["mem_w_k"], W["mem_w_v"]], axis=1).astype(BF16),
        rwkv_wo=W["rwkv_w_o"].astype(BF16), mla_wo=W["mla_w_o"].astype(BF16), mem_wo=W["mem_w_o"].astype(BF16),
        w_out=W["w_out"].astype(BF16), qscale=float(nope + rope) ** -0.5 * LOG2E,
    )


def _tile(n, pref):
    t = min(pref, n)
    assert n % t == 0, (n, t)
    return t


def kernel(x_prompt, x_sample, cache_mla, state_rwkv, state_shift, cache_mem_k, cache_mem_v, page_table, mem_prompt, ffn1_pre, ffn1_post, ffn1_gate, ffn1_up, ffn1_down, mix_pre, mix_post, w_in, rwkv_mu, rwkv_w0, rwkv_w2, rwkv_a0, rwkv_a2, rwkv_g2, rwkv_k_k, rwkv_k_a, rwkv_r_k, rwkv_lnx_g, rwkv_lnx_b, rwkv_w_o, mla_q_norm, mla_w_qb, mla_kv_norm, mla_w_uk, mla_w_uv, mla_w_o, mem_norm, mem_w_k, mem_w_v, mem_w_o, w_out, ffn2_pre, ffn2_post, ffn2_gate, ffn2_up, ffn2_down):
    names = ("ffn1_pre", "ffn1_post", "ffn1_gate", "ffn1_up", "ffn1_down", "mix_pre", "mix_post", "w_in",
             "rwkv_mu", "rwkv_w0", "rwkv_w2", "rwkv_a0", "rwkv_a2", "rwkv_g2", "rwkv_k_k", "rwkv_k_a", "rwkv_r_k",
             "rwkv_lnx_g", "rwkv_lnx_b", "rwkv_w_o", "mla_q_norm", "mla_w_qb", "mla_kv_norm", "mla_w_uk", "mla_w_uv",
             "mla_w_o", "mem_norm", "mem_w_k", "mem_w_v", "mem_w_o", "w_out", "ffn2_pre", "ffn2_post", "ffn2_gate",
             "ffn2_up", "ffn2_down")
    stacked = (ffn1_pre, ffn1_post, ffn1_gate, ffn1_up, ffn1_down, mix_pre, mix_post, w_in,
               rwkv_mu, rwkv_w0, rwkv_w2, rwkv_a0, rwkv_a2, rwkv_g2, rwkv_k_k, rwkv_k_a, rwkv_r_k,
               rwkv_lnx_g, rwkv_lnx_b, rwkv_w_o, mla_q_norm, mla_w_qb, mla_kv_norm, mla_w_uk, mla_w_uv,
               mla_w_o, mem_norm, mem_w_k, mem_w_v, mem_w_o, w_out, ffn2_pre, ffn2_post, ffn2_gate,
               ffn2_up, ffn2_down)
    B, S, D = x_prompt.shape
    DB, T, _ = x_sample.shape
    assert T == 1, "decode groups carry one new token per request"
    depth = ffn1_pre.shape[0]
    page = cache_mla.shape[2]
    past_len = page_table.shape[1] * page
    mem_tokens, mem_heads, mem_hd = cache_mem_k.shape[2:]
    mem_scale = float(mem_hd) ** -0.5

    cache_t = jnp.swapaxes(cache_mla, 2, 3)
    state_t = jnp.transpose(state_rwkv, (0, 2, 3, 4, 1))
    xp = x_prompt.reshape(B * S, D)
    xs = x_sample.reshape(DB, D)
    outs = [[] for _ in range(8)]
    for l in range(depth):
        P = _prep_weights({n: w[l] for n, w in zip(names, stacked)}, D)
        rp, ql, kl, rope, md, gd, mh, nope = P["dims"]
        nh, hd, lora, rw = P["nh"], P["hd"], P["lora"], P["rw"]
        cos_p, sin_p = _rope_tables(jnp.arange(S), rope, mh)
        cos_s, sin_s = _rope_tables(jnp.full((DB,), past_len), rope, mh)
        tm_p, tm_s = _tile(S, 256), DB

        mkv = _memkv(mem_prompt.reshape(B * mem_tokens, D), P["mem_norm"], P["mem_wkv"])
        mk_p, mv_p = mkv[:, :md].reshape(B, mem_tokens, md), mkv[:, md:].reshape(B, mem_tokens, md)

        h = _ffn(xp, *P["ffn1"], tm_p)
        prw, q, rows, kbf, qmem, gates = _inproj(h, B, P["mix_pre"], P["win"], P["q_norm"], P["wqb"], P["wuk"],
                                                 P["kv_norm"], cos_p, sin_p, P["dims"], P["qscale"], tm_p)
        og, wkv_p = _rwkv_prompt(prw.reshape(B, S, rp), rw, nh, hd, lora)
        ctx = _attn_prompt(q, kbf.reshape(B, S, kl + rope), kl)
        om = _memattn_prompt(qmem, mk_p, mv_p, mem_heads, mem_scale, tm_p)
        h = _merge(h, B, og.reshape(B * S, nh * hd), ctx, om, gates, P["rwkv_wo"], P["wuv"], P["mla_wo"], P["mem_wo"],
                   P["w_out"], P["mix_post"], mh, tm_p)
        xp = _ffn(h, *P["ffn2"], tm_p)
        rows_p, shift_p = rows.reshape(B, S, kl + rope), prw.reshape(B, S, rp)[:, -1]

        h = _ffn(xs, *P["ffn1"], tm_s)
        prw, q, rows, kbf, qmem, gates = _inproj(h, 1, P["mix_pre"], P["win"], P["q_norm"], P["wqb"], P["wuk"],
                                                 P["kv_norm"], cos_s, sin_s, P["dims"], P["qscale"], tm_s)
        og, wkv_t = _rwkv_sample(prw, state_shift[l], state_t, l, rw, nh, hd, lora, _tile(nh, 2))
        wkv_s = jnp.transpose(wkv_t, (3, 0, 1, 2))
        ctx = _attn_sample(jnp.swapaxes(q[0], 0, 1), kbf.reshape(DB, 1, kl + rope), cache_t, l, page_table, kl)
        om = _memattn_sample(qmem, cache_mem_k, cache_mem_v, l, mem_scale, _tile(DB, 4))
        h = _merge(h, 1, og, jnp.swapaxes(ctx, 0, 1)[None], om, gates, P["rwkv_wo"], P["wuv"], P["mla_wo"], P["mem_wo"],
                   P["w_out"], P["mix_post"], mh, tm_s)
        xs = _ffn(h, *P["ffn2"], tm_s)

        for lst, val in zip(outs, (rows_p, rows.reshape(DB, T, kl + rope), wkv_p, wkv_s, shift_p, prw,
                                   mk_p.reshape(B, mem_tokens, mem_heads, mem_hd),
                                   mv_p.reshape(B, mem_tokens, mem_heads, mem_hd))):
            lst.append(val)
    return (xp.reshape(B, S, D), xs.reshape(DB, T, D)) + tuple(jnp.stack(o) for o in outs)
```

```python
import functools

import jax
import jax.numpy as jnp
from jax import lax
from jax.experimental import pallas as pl
from jax.experimental.pallas import tpu as pltpu

F32, BF16 = jnp.float32, jnp.bfloat16
RMS_EPS = 1e-6
LNX_EPS = 64e-5
ROPE_BASE = 10000.0
LANES = 128
VMEM_LIMIT = 52 * 1024 * 1024
RWKV_CHUNK = 64
RWKV_CHUNKS_PER_STEP = 2
ATT_TQ = 256
ATT_TK = 512
ATT_ROW_GROUP = 512
PAGES_PER_STEP = 64
LOG2E = 1.4426950408889634


def _cparams(*sem):
    return pltpu.CompilerParams(dimension_semantics=sem, vmem_limit_bytes=VMEM_LIMIT)


def _resident(shape):
    nd = len(shape)
    return pl.BlockSpec(shape, lambda *_: (0,) * nd, pipeline_mode=pl.Buffered(1))


def _rms(x, g):
    return x * lax.rsqrt(jnp.mean(x * x, axis=-1, keepdims=True) + RMS_EPS) * g


def _sigmoid(x):
    return 1.0 / (1.0 + jnp.exp(-x))


def _mm(a, b):
    return jnp.dot(a.astype(BF16), b.astype(BF16), preferred_element_type=F32)


def _mm_nt(a, b):
    return lax.dot_general(a.astype(BF16), b.astype(BF16), (((1,), (1,)), ((), ())), preferred_element_type=F32)


def _ffn_body(x_ref, pre_ref, post_ref, wg_ref, wu_ref, wd_ref, o_ref):
    x = x_ref[...]
    h = _rms(x, pre_ref[...]).astype(BF16)
    g = jnp.dot(h, wg_ref[...], preferred_element_type=F32)
    u = jnp.dot(h, wu_ref[...], preferred_element_type=F32)
    act = (g * _sigmoid(g)) * u
    y = jnp.dot(act.astype(BF16), wd_ref[...], preferred_element_type=F32)
    o_ref[...] = x + 0.5 * _rms(y, post_ref[...])


def _ffn(x, pre, post, wg, wu, wd, tm):
    n, d = x.shape
    f = wg.shape[1]
    return pl.pallas_call(
        _ffn_body,
        grid=(n // tm,),
        in_specs=[pl.BlockSpec((tm, d), lambda i: (i, 0)), _resident((1, d)), _resident((1, d)),
                  _resident((d, f)), _resident((d, f)), _resident((f, d))],
        out_specs=pl.BlockSpec((tm, d), lambda i: (i, 0)),
        out_shape=jax.ShapeDtypeStruct((n, d), F32),
        compiler_params=_cparams("parallel"),
        name="ffn",
    )(x, pre, post, wg, wu, wd)


def _inproj_body(dims, qscale, h_ref, pre_ref, win_ref, qn_ref, wqb_ref, wuk_ref, kvn_ref, cos_ref, sin_ref,
                 prw_ref, q_ref, rows_ref, kbf_ref, qmem_ref, gates_ref):
    rp, ql, kl, rope, md, gd, nh, nope = dims
    u = _rms(h_ref[...], pre_ref[...]).astype(BF16)
    p = jnp.dot(u, win_ref[...], preferred_element_type=F32)
    o = 0
    prw_ref[...] = p[:, o:o + rp]; o += rp
    cq = p[:, o:o + ql]; o += ql
    ckv = p[:, o:o + kl]; o += kl
    qmem_ref[...] = p[:, o:o + md].astype(BF16); o += md
    gates_ref[...] = _sigmoid(p[:, o:o + gd]); o += gd
    kpe = p[:, o:o + rope]; o += rope
    kpe_sw = p[:, o:o + rope]
    cos = cos_ref[...]
    sin = sin_ref[...]
    q = jnp.dot(_rms(cq, qn_ref[...]).astype(BF16), wqb_ref[...], preferred_element_type=F32)
    nn = nh * nope
    nr = nh * rope
    qpe = ((q[:, nn:nn + nr] * cos + q[:, nn + nr:nn + 2 * nr] * sin) * qscale).astype(BF16)
    qn = q[:, :nn].astype(BF16)
    for pr in range(nh // 2):
        qlat = (jnp.dot(qn[:, LANES * pr:LANES * (pr + 1)], wuk_ref[pr], preferred_element_type=F32) * qscale).astype(BF16)
        for e in range(2):
            hh = 2 * pr + e
            q_ref[0, hh, :, 0:kl] = qlat[:, kl * e:kl * (e + 1)]
            q_ref[0, hh, :, kl:kl + rope] = qpe[:, rope * hh:rope * (hh + 1)]
    ckvn = _rms(ckv, kvn_ref[...])
    kper = kpe * cos[:, :rope] + kpe_sw * sin[:, :rope]
    rows_ref[:, 0:kl] = ckvn
    rows_ref[:, kl:kl + rope] = kper
    kbf_ref[:, 0:kl] = ckvn.astype(BF16)
    kbf_ref[:, kl:kl + rope] = kper.astype(BF16)


def _inproj(h, nbatch, pre, win, qn, wqb, wuk, kvn, cos, sin, dims, qscale, tm):
    n, d = h.shape
    rp, ql, kl, rope, md, gd, nh, nope = dims
    t = n // nbatch
    nb = t // tm
    cw = win.shape[1]
    row = lambda i: (i, 0)
    tab = lambda i: (i % nb, 0)
    return pl.pallas_call(
        functools.partial(_inproj_body, dims, qscale),
        grid=(n // tm,),
        in_specs=[pl.BlockSpec((tm, d), row), _resident((1, d)), _resident((d, cw)), _resident((1, ql)),
                  _resident(wqb.shape), _resident(wuk.shape), _resident((1, kl)),
                  pl.BlockSpec((tm, nh * rope), tab), pl.BlockSpec((tm, nh * rope), tab)],
        out_specs=[pl.BlockSpec((tm, rp), row),
                   pl.BlockSpec((1, nh, tm, kl + rope), lambda i: (i // nb, 0, i % nb, 0)),
                   pl.BlockSpec((tm, kl + rope), row), pl.BlockSpec((tm, kl + rope), row),
                   pl.BlockSpec((tm, md), row), pl.BlockSpec((tm, gd), row)],
        out_shape=[jax.ShapeDtypeStruct((n, rp), F32),
                   jax.ShapeDtypeStruct((nbatch, nh, t, kl + rope), BF16),
                   jax.ShapeDtypeStruct((n, kl + rope), F32),
                   jax.ShapeDtypeStruct((n, kl + rope), BF16),
                   jax.ShapeDtypeStruct((n, md), BF16),
                   jax.ShapeDtypeStruct((n, gd), F32)],
        compiler_params=_cparams("parallel"),
        name="inproj",
    )(h, pre, win, qn, wqb, wuk, kvn, cos, sin)


def _segsum(x, e):
    hi = x.astype(BF16)
    lo = (x - hi.astype(F32)).astype(BF16)
    return jnp.dot(hi, e, preferred_element_type=F32) + jnp.dot(lo, e, preferred_element_type=F32)


def _rwkv_prep(p, prev, mu, w0, w2a2, a0, g2, k_k, k_a, e, rd, lora):
    ps = p + (prev - p) * mu
    r = ps[:, 0:rd]
    k = ps[:, rd:2 * rd]
    v = ps[:, 2 * rd:3 * rd]
    wa = ps[:, 3 * rd:3 * rd + 2 * lora]
    gl = ps[:, 3 * rd + 2 * lora:]
    lane = lax.broadcasted_iota(jnp.int32, wa.shape, 1)
    wa = jnp.where(lane < lora, jnp.tanh(wa), wa)
    wa2 = _mm(wa, w2a2)
    x = -(w0 + wa2[:, :rd])
    softplus = jnp.maximum(x, 0.0) + jnp.log(1.0 + jnp.exp(-jnp.abs(x)))
    logdec = -jnp.exp(-softplus - 0.5)
    a = _sigmoid(a0 + wa2[:, rd:])
    g = _mm(_sigmoid(gl), g2)
    kk = k * k_k
    kk = kk / jnp.maximum(jnp.sqrt(_segsum(kk * kk, e)), 1e-12)
    k = k * (1.0 + (a - 1.0) * k_a)
    return r, k, v, logdec, -kk, kk * a, g


def _rwkv_post(o, r, k, v, g, r_k, lng, lnb, e, hd):
    mean = _segsum(o, e) * (1.0 / hd)
    oc = o - mean
    var = _segsum(oc * oc, e) * (1.0 / hd)
    o = oc * lax.rsqrt(var + LNX_EPS) * lng + lnb
    bonus = _segsum(r * k * r_k, e) * v
    return (o + bonus) * g


def _pair_rows(y):
    lo = (lax.broadcasted_iota(jnp.int32, y.shape, 1) % LANES) < (LANES // 2)
    z = jnp.zeros_like(y)
    return jnp.concatenate([jnp.where(lo, y, z), jnp.where(lo, z, y)], axis=0)


def _rwkv_chunks_local(items, strict, incl):
    L = items[0][0].shape[0]
    cs = []
    for r, k, v, ld, cum, a, b in items:
        cum_l = cum[L - 1:L, :]
        e_neg = jnp.exp(-cum)
        e_end = jnp.exp(cum_l - cum)
        kt, bt = (k * e_neg).astype(BF16), (b * e_neg).astype(BF16)
        at, rt = a * jnp.exp(cum - ld), (r * jnp.exp(cum)).astype(BF16)
        cs.append(dict(at=at, rt=rt, vb=_pair_rows(v.astype(BF16)), v=v, dl=jnp.exp(cum_l),
                       bkh=jnp.concatenate([b * e_end, k * e_end], axis=0).astype(BF16),
                       lhs=jnp.concatenate([at.astype(BF16), rt], axis=0),
                       rhs=jnp.concatenate([_pair_rows(bt), _pair_rows(kt)], axis=0)))
    for c in cs:
        mm = _mm_nt(c.pop("lhs"), c.pop("rhs"))
        c["pw"] = jnp.where(strict, mm[:L, :2 * L], 0.0)
        c["m_ka"] = jnp.where(strict, mm[:L, 2 * L:], 0.0)
        c["m_r"] = jnp.concatenate([jnp.where(incl, mm[L:, :2 * L], 0.0), jnp.where(incl, mm[L:, 2 * L:], 0.0)],
                                   axis=1).astype(BF16)
    for c in cs:
        c["x"] = jnp.concatenate([c.pop("at"), _mm(c.pop("m_ka"), c["vb"])], axis=1)
    span = 1
    while span < L:
        for c in cs:
            c["x"] = c["x"] + _mm(c["pw"], _pair_rows(c["x"].astype(BF16)))
        span *= 2
        if span < L:
            for c in cs:
                c["pw"] = _mm(c["pw"], _pair_rows(c["pw"].astype(BF16)))
    for c in cs:
        x = c.pop("x")
        c["w1"], c["uloc"] = x[:, :LANES].astype(BF16), x[:, LANES:]
    return cs


def _rwkv_chunks_apply(cs, states, diag):
    sbs = [s.astype(BF16) for s in states]
    urs = [_mm_nt(c["w1"], sb) + c["uloc"] for c, sb in zip(cs, sbs)]
    o1 = [_mm_nt(c["rt"], sb) for c, sb in zip(cs, sbs)]
    upds = [_mm(jnp.concatenate([ur, c["v"]], axis=0).T, c["bkh"]) for c, ur in zip(cs, urs)]
    o2 = [_mm(c["m_r"], jnp.concatenate([_pair_rows(ur.astype(BF16)), c["vb"]], axis=0)) for c, ur in zip(cs, urs)]
    return [(a + b, s * c["dl"] + jnp.where(diag, u, 0.0)) for a, b, s, c, u in zip(o1, o2, states, cs, upds)]


def _rwkv_prompt_body(rd, lora, hd, L, p_ref, mu_ref, w0_ref, w2a2_ref, a0_ref, g2_ref, kk_ref, ka_ref, rk_ref,
                      lng_ref, lnb_ref, e_ref, og_ref, st_ref, prev_ref, s_ref):
    step = pl.program_id(0)
    nb, rows, _ = p_ref.shape
    npair = rd // LANES

    @pl.when(step == 0)
    def _():
        prev_ref[...] = jnp.zeros_like(prev_ref)
        s_ref[...] = jnp.zeros_like(s_ref)

    e = e_ref[...]
    ti = lax.broadcasted_iota(jnp.int32, (L, 2 * L), 0)
    si = lax.broadcasted_iota(jnp.int32, (L, 2 * L), 1) % L
    strict, incl = si < ti, si <= ti
    half = LANES // 2
    diag = ((lax.broadcasted_iota(jnp.int32, (LANES, LANES), 0) < half)
            == (lax.broadcasted_iota(jnp.int32, (LANES, LANES), 1) < half))
    tr = lax.broadcasted_iota(jnp.int32, (rows, rows), 0)
    tc = lax.broadcasted_iota(jnp.int32, (rows, rows), 1)
    tri = ((tc <= tr) & (tc // L == tr // L)).astype(BF16)
    rowi = lax.broadcasted_iota(jnp.int32, (rows, p_ref.shape[2]), 0)

    keys, items, vecs = [], [], []
    for bi in range(nb):
        p = p_ref[bi]
        prev = jnp.where(rowi == 0, prev_ref[bi], pltpu.roll(p, 1, axis=0))
        prev_ref[bi] = p[rows - 1:rows, :]
        r, k, v, ld, a, b, g = _rwkv_prep(p, prev, mu_ref[...], w0_ref[...], w2a2_ref[...], a0_ref[...], g2_ref[...],
                                          kk_ref[...], ka_ref[...], e, rd, lora)
        hi = ld.astype(BF16)
        r1 = ld - hi.astype(F32)
        mid = r1.astype(BF16)
        lo = (r1 - mid.astype(F32)).astype(BF16)
        cum = (jnp.dot(tri, hi, preferred_element_type=F32) + jnp.dot(tri, mid, preferred_element_type=F32)
               + jnp.dot(tri, lo, preferred_element_type=F32))
        vecs.append((r, k, v, g))
        for cc in range(rows // L):
            for pr in range(npair):
                sl = (slice(L * cc, L * (cc + 1)), slice(LANES * pr, LANES * (pr + 1)))
                keys.append((bi, cc, pr))
                items.append((r[sl], k[sl], v[sl], ld[sl], cum[sl], a[sl], b[sl]))
    local = dict(zip(keys, _rwkv_chunks_local(items, strict, incl)))
    chains = [(bi, pr) for bi in range(nb) for pr in range(npair)]
    state = {ch: s_ref[ch[0] * npair + ch[1]] for ch in chains}
    outs = {}
    for cc in range(rows // L):
        new = _rwkv_chunks_apply([local[bi, cc, pr] for bi, pr in chains], [state[ch] for ch in chains], diag)
        for ch, (o, s) in zip(chains, new):
            outs[ch, cc], state[ch] = o, s
    for bi in range(nb):
        for pr in range(npair):
            s_ref[bi * npair + pr] = state[bi, pr]
        o = jnp.concatenate([jnp.concatenate([outs[(bi, pr), cc] for cc in range(rows // L)], axis=0)
                             for pr in range(npair)], axis=1)
        r, k, v, g = vecs[bi]
        og_ref[bi] = _rwkv_post(o, r, k, v, g, rk_ref[...], lng_ref[...], lnb_ref[...], e, hd).astype(BF16)

    @pl.when(step == pl.num_programs(0) - 1)
    def _():
        for bi in range(nb):
            for pr in range(npair):
                s = s_ref[bi * npair + pr]
                st_ref[bi, 2 * pr] = s[:hd, :hd]
                st_ref[bi, 2 * pr + 1] = s[hd:, hd:]


def _rwkv_prompt(prw, rw, nh, hd, lora):
    b, t, pw = prw.shape
    rd = nh * hd
    rows = _tile(t, RWKV_CHUNK * RWKV_CHUNKS_PER_STEP)
    names = ("mu", "w0", "w2a2", "a0", "g2", "k_k", "k_a", "r_k", "lnx_g", "lnx_b", "e")
    return pl.pallas_call(
        functools.partial(_rwkv_prompt_body, rd, lora, hd, RWKV_CHUNK),
        grid=(t // rows,),
        in_specs=[pl.BlockSpec((b, rows, pw), lambda c: (0, c, 0))] + [_resident(rw[k].shape) for k in names],
        out_specs=[pl.BlockSpec((b, rows, rd), lambda c: (0, c, 0)),
                   pl.BlockSpec((b, nh, hd, hd), lambda c: (0, 0, 0, 0))],
        out_shape=[jax.ShapeDtypeStruct((b, t, rd), BF16), jax.ShapeDtypeStruct((b, nh, hd, hd), F32)],
        scratch_shapes=[pltpu.VMEM((b, 1, pw), F32), pltpu.VMEM((b * (rd // LANES), LANES, LANES), F32)],
        compiler_params=_cparams("arbitrary"),
        name="rwkv_prompt",
    )(prw, *[rw[k] for k in names])


def _rwkv_prep_body(rd, lora, p_ref, prev_ref, mu_ref, w0_ref, w2a2_ref, a0_ref, g2_ref, kk_ref, ka_ref, e_ref,
                    r_ref, k_ref, v_ref, g_ref, *t_refs):
    r, k, v, ld, a, b, g = _rwkv_prep(p_ref[...], prev_ref[...], mu_ref[...], w0_ref[...], w2a2_ref[...], a0_ref[...],
                                      g2_ref[...], kk_ref[...], ka_ref[...], e_ref[...], rd, lora)
    r_ref[...] = r
    k_ref[...] = k
    v_ref[...] = v
    g_ref[...] = g
    for ref, x in zip(t_refs, (r, k, v, jnp.exp(ld), a, b)):
        ref[...] = x.T


def _rwkv_step_body(s_ref, r_ref, k_ref, v_ref, w_ref, a_ref, b_ref, so_ref, o_ref):
    for h in range(s_ref.shape[0]):
        s = s_ref[h]
        sa = jnp.sum(s * a_ref[h][None], axis=1)
        s = s * w_ref[h][None] + sa[:, None, :] * b_ref[h][None] + v_ref[h][:, None, :] * k_ref[h][None]
        so_ref[h] = s
        o_ref[h] = jnp.sum(s * r_ref[h][None], axis=1)


def _rwkv_post_body(hd, o_ref, r_ref, k_ref, v_ref, g_ref, rk_ref, lng_ref, lnb_ref, e_ref, og_ref):
    og_ref[...] = _rwkv_post(o_ref[...].T, r_ref[...], k_ref[...], v_ref[...], g_ref[...], rk_ref[...], lng_ref[...],
                             lnb_ref[...], e_ref[...], hd).astype(BF16)


def _rwkv_sample(prw, shift, state_t, layer, rw, nh, hd, lora, hb):
    n, pw = prw.shape
    rd = nh * hd
    names = ("mu", "w0", "w2a2", "a0", "g2", "k_k", "k_a", "e")
    full = lambda s: pl.BlockSpec(s, lambda: (0,) * len(s))
    vecs = pl.pallas_call(
        functools.partial(_rwkv_prep_body, rd, lora),
        in_specs=[full((n, pw)), full((n, pw))] + [full(rw[k].shape) for k in names],
        out_specs=[full((n, rd))] * 4 + [full((rd, n))] * 6,
        out_shape=[jax.ShapeDtypeStruct((n, rd), F32)] * 4 + [jax.ShapeDtypeStruct((rd, n), F32)] * 6,
        name="rwkv_prep",
    )(prw, shift, *[rw[k] for k in names])
    r, k, v, g = vecs[:4]
    vspec = pl.BlockSpec((hb, hd, n), lambda i: (i, 0, 0))
    sspec = pl.BlockSpec((hb, hd, hd, n), lambda i: (i, 0, 0, 0))
    s_new, o = pl.pallas_call(
        _rwkv_step_body,
        grid=(nh // hb,),
        in_specs=[pl.BlockSpec((None, hb, hd, hd, n), lambda i: (layer, i, 0, 0, 0))] + [vspec] * 6,
        out_specs=[sspec, vspec],
        out_shape=[jax.ShapeDtypeStruct(state_t.shape[1:], F32), jax.ShapeDtypeStruct((nh, hd, n), F32)],
        compiler_params=_cparams("parallel"),
        name="rwkv_step",
    )(state_t, *[x.reshape(nh, hd, n) for x in vecs[4:]])
    pnames = ("r_k", "lnx_g", "lnx_b", "e")
    og = pl.pallas_call(
        functools.partial(_rwkv_post_body, hd),
        in_specs=[full((rd, n))] + [full((n, rd))] * 4 + [full(rw[k].shape) for k in pnames],
        out_specs=full((n, rd)),
        out_shape=jax.ShapeDtypeStruct((n, rd), BF16),
        name="rwkv_post",
    )(o.reshape(rd, n), r, k, v, g, *[rw[k] for k in pnames])
    return og, s_new


def _lanes(x, n):
    return x if n == LANES else jnp.concatenate([x] * (n // LANES), axis=1)


def _attn_prompt_body(kl, tk, rg, q_ref, k_ref, o_ref, m_ref, l_ref, acc_ref, s_ref):
    i = pl.program_id(1)
    nh, tq, dk = q_ref.shape[1:]
    rows = nh * tq
    q = q_ref[0].reshape(rows, dk)
    groups = [slice(g * rg, (g + 1) * rg) for g in range(rows // rg)]
    m_ref[...] = jnp.full_like(m_ref, -jnp.inf)
    l_ref[...] = jnp.zeros_like(l_ref)
    acc_ref[...] = jnp.zeros_like(acc_ref)

    def keys(j):
        return k_ref[0, pl.ds(pl.multiple_of(j * tk, tk), tk), :]

    def scores(r, k):
        return lax.dot_general(q[r], k, (((1,), (1,)), ((), ())), preferred_element_type=F32)

    def update(r, s, v):
        m_old = m_ref[r]
        m_new = jnp.maximum(m_old, jnp.max(s, axis=-1, keepdims=True))
        alpha = jnp.exp2(m_old - m_new)
        p = jnp.exp2(s - _lanes(m_new, tk))
        l_ref[r] = alpha * l_ref[r] + jnp.sum(p, axis=-1, keepdims=True)
        acc_ref[r] = _lanes(alpha, kl) * acc_ref[r] + jnp.dot(p.astype(BF16), v, preferred_element_type=F32)
        m_ref[r] = m_new

    k0 = keys(0)
    for r in groups:
        s_ref[r] = scores(r, k0)

    def body(j, carry):
        v = keys(j)[:, :kl]
        k_next = keys(j + 1)
        for r in groups:
            s = s_ref[r]
            s_ref[r] = scores(r, k_next)
            update(r, s, v)
        return carry

    last = (i * tq) // tk
    lax.fori_loop(0, last, body, 0)
    v = keys(last)[:, :kl]
    for g, r in enumerate(groups):
        s = s_ref[r]
        qpos = i * tq + (g * rg + lax.broadcasted_iota(jnp.int32, s.shape, 0)) % tq
        kpos = last * tk + lax.broadcasted_iota(jnp.int32, s.shape, 1)
        update(r, jnp.where(kpos <= qpos, s, -jnp.inf), v)
    o_ref[0] = (acc_ref[...] / _lanes(l_ref[...], kl)).reshape(nh, tq, kl).astype(o_ref.dtype)


def _attn_prompt(q, kbf, kl):
    b, nh, t, dk = q.shape
    tq, tk = _tile(t, ATT_TQ), _tile(t, ATT_TK)
    rows = nh * tq
    rg = _tile(rows, ATT_ROW_GROUP)
    assert tk % tq == 0 and rg % tq == 0
    return pl.pallas_call(
        functools.partial(_attn_prompt_body, kl, tk, rg),
        grid=(b, t // tq),
        in_specs=[pl.BlockSpec((1, nh, tq, dk), lambda bi, i: (bi, 0, i, 0)),
                  pl.BlockSpec((1, t, dk), lambda bi, i: (bi, 0, 0))],
        out_specs=pl.BlockSpec((1, nh, tq, kl), lambda bi, i: (bi, 0, i, 0)),
        out_shape=jax.ShapeDtypeStruct((b, nh, t, kl), BF16),
        scratch_shapes=[pltpu.VMEM((rows, LANES), F32), pltpu.VMEM((rows, LANES), F32), pltpu.VMEM((rows, kl), F32),
                        pltpu.VMEM((rows, tk), F32)],
        compiler_params=_cparams("parallel", "arbitrary"),
        name="attn_prompt",
    )(q, kbf)


def _attn_sample_body(kl, npg, pt_ref, q_ref, kself_ref, *rest):
    pages, (o_ref, m_ref, l_ref, acc_ref) = rest[:npg], rest[npg:]
    j = pl.program_id(1)
    q = q_ref[0]

    @pl.when(j == 0)
    def _():
        ks = kself_ref[0]
        m_ref[...] = jnp.sum(q.astype(F32) * ks.astype(F32), axis=-1, keepdims=True)
        l_ref[...] = jnp.ones_like(l_ref)
        acc_ref[...] = jnp.broadcast_to(ks[:, :kl].astype(F32), acc_ref.shape)

    grp = 2 if npg % 2 == 0 else 1
    kts = [jnp.concatenate([pages[n + e][0] for e in range(grp)], axis=1).astype(BF16) for n in range(0, npg, grp)]
    s = jnp.concatenate([jnp.dot(q, kt, preferred_element_type=F32) for kt in kts], axis=1)
    m_old = m_ref[...]
    m_new = jnp.maximum(m_old, jnp.max(s, axis=-1, keepdims=True))
    alpha = jnp.exp2(m_old - m_new)
    p = jnp.exp2(s - m_new)
    l_ref[...] = alpha * l_ref[...] + jnp.sum(p, axis=-1, keepdims=True)
    pb = p.astype(BF16)
    w = kts[0].shape[1]
    pv = _mm_nt(pb[:, :w], kts[0][:kl, :])
    for n in range(1, len(kts)):
        pv = pv + _mm_nt(pb[:, n * w:(n + 1) * w], kts[n][:kl, :])
    acc_ref[...] = alpha * acc_ref[...] + pv
    m_ref[...] = m_new

    @pl.when(j == pl.num_programs(1) - 1)
    def _():
        o_ref[0] = (acc_ref[...] / l_ref[...]).astype(o_ref.dtype)


def _attn_sample(q, kself, cache_t, layer, page_table, kl):
    n, nh, dk = q.shape
    ps = cache_t.shape[3]
    npages = page_table.shape[1]
    npg = _tile(npages, PAGES_PER_STEP)

    def page_spec(kk):
        return pl.BlockSpec((None, 1, dk, ps), lambda bi, j, pt: (layer, pt[bi, j * npg + kk], 0, 0))

    grid_spec = pltpu.PrefetchScalarGridSpec(
        num_scalar_prefetch=1,
        grid=(n, npages // npg),
        in_specs=[pl.BlockSpec((1, nh, dk), lambda bi, j, pt: (bi, 0, 0)),
                  pl.BlockSpec((1, 1, dk), lambda bi, j, pt: (bi, 0, 0))] + [page_spec(kk) for kk in range(npg)],
        out_specs=pl.BlockSpec((1, nh, kl), lambda bi, j, pt: (bi, 0, 0)),
        scratch_shapes=[pltpu.VMEM((nh, 1), F32), pltpu.VMEM((nh, 1), F32), pltpu.VMEM((nh, kl), F32)],
    )
    return pl.pallas_call(
        functools.partial(_attn_sample_body, kl, npg),
        grid_spec=grid_spec,
        out_shape=jax.ShapeDtypeStruct((n, nh, kl), BF16),
        compiler_params=_cparams("parallel", "arbitrary"),
        name="attn_sample",
    )(page_table, q, kself, *([cache_t] * npg))


def _memkv_body(m_ref, g_ref, w_ref, o_ref):
    o_ref[...] = jnp.dot(_rms(m_ref[...], g_ref[...]).astype(BF16), w_ref[...], preferred_element_type=F32)


def _memkv(mem, g, wkv):
    n, d = mem.shape
    full = lambda s: pl.BlockSpec(s, lambda: (0,) * len(s))
    return pl.pallas_call(
        _memkv_body,
        in_specs=[full((n, d)), full((1, d)), full(wkv.shape)],
        out_specs=full((n, wkv.shape[1])),
        out_shape=jax.ShapeDtypeStruct((n, wkv.shape[1]), F32),
        name="mem_kv",
    )(mem, g, wkv)


def _mem_attend(q, k, v, scale):
    s = _mm_nt(q, k) * scale
    p = jnp.exp(s - jnp.max(s, axis=-1, keepdims=True))
    p = p / jnp.sum(p, axis=-1, keepdims=True)
    return _mm(p, v)


def _memattn_prompt_body(nh, scale, q_ref, k_ref, v_ref, o_ref):
    hd = q_ref.shape[1] // nh
    q, k, v = q_ref[...], k_ref[0], v_ref[0]
    heads = [slice(h * hd, (h + 1) * hd) for h in range(nh)]
    o_ref[...] = jnp.concatenate([_mem_attend(q[:, sl], k[:, sl], v[:, sl], scale) for sl in heads],
                                 axis=1).astype(o_ref.dtype)


def _memattn_prompt(q, mk, mv, nh, scale, tm):
    n, md = q.shape
    b, m, _ = mk.shape
    nb = (n // b) // tm
    kv = pl.BlockSpec((1, m, md), lambda i: (i // nb, 0, 0))
    return pl.pallas_call(
        functools.partial(_memattn_prompt_body, nh, scale),
        grid=(n // tm,),
        in_specs=[pl.BlockSpec((tm, md), lambda i: (i, 0)), kv, kv],
        out_specs=pl.BlockSpec((tm, md), lambda i: (i, 0)),
        out_shape=jax.ShapeDtypeStruct((n, md), BF16),
        compiler_params=_cparams("parallel"),
        name="memattn_prompt",
    )(q, mk, mv)


def _memattn_sample_body(scale, q_ref, k_ref, v_ref, o_ref):
    gr, rows, _ = k_ref.shape
    nh = q_ref.shape[1]
    own = (lax.broadcasted_iota(jnp.int32, (nh, rows), 1) % nh) == lax.broadcasted_iota(jnp.int32, (nh, rows), 0)
    ss = [jnp.where(own, _mm_nt(q_ref[g], k_ref[g]) * scale, -jnp.inf) for g in range(gr)]
    ps = [jnp.exp(s - jnp.max(s, axis=-1, keepdims=True)) for s in ss]
    ps = [p / jnp.sum(p, axis=-1, keepdims=True) for p in ps]
    for g in range(gr):
        o_ref[g] = _mm(ps[g], v_ref[g]).astype(o_ref.dtype)


def _memattn_sample(q, cache_k, cache_v, layer, scale, gr):
    n, md = q.shape
    depth, _, m, nh, hd = cache_k.shape
    kv = pl.BlockSpec((None, gr, m * nh, hd), lambda i: (layer, i, 0, 0))
    qs = pl.BlockSpec((gr, nh, hd), lambda i: (i, 0, 0))
    return pl.pallas_call(
        functools.partial(_memattn_sample_body, scale),
        grid=(n // gr,),
        in_specs=[qs, kv, kv],
        out_specs=qs,
        out_shape=jax.ShapeDtypeStruct((n, nh, hd), BF16),
        compiler_params=_cparams("parallel"),
        name="memattn_sample",
    )(q.reshape(n, nh, hd), cache_k.reshape(depth, n, m * nh, hd), cache_v.reshape(depth, n, m * nh, hd)).reshape(n, md)


def _merge_body(nh, h_ref, og_ref, ctx_ref, om_ref, gates_ref, wo_ref, wuv_ref, mwo_ref, memwo_ref, wout_ref, post_ref,
                o_ref):
    d = h_ref.shape[1]
    o_rwkv = jnp.dot(og_ref[...], wo_ref[...], preferred_element_type=F32)
    vs = []
    for pr in range(nh // 2):
        vp = (jnp.dot(ctx_ref[0, 2 * pr], wuv_ref[2 * pr], preferred_element_type=F32)
              + jnp.dot(ctx_ref[0, 2 * pr + 1], wuv_ref[2 * pr + 1], preferred_element_type=F32))
        vs.append(vp.astype(BF16))
    o_mla = jnp.dot(jnp.concatenate(vs, axis=1), mwo_ref[...], preferred_element_type=F32)
    o_mem = jnp.dot(om_ref[...], memwo_ref[...], preferred_element_type=F32)
    merged = gates_ref[:, 0:d] * o_rwkv + gates_ref[:, d:2 * d] * o_mla + gates_ref[:, 2 * d:3 * d] * o_mem
    y = jnp.dot(merged.astype(BF16), wout_ref[...], preferred_element_type=F32)
    o_ref[...] = h_ref[...] + _rms(y, post_ref[...])


def _merge(h, nbatch, og, ctx, om, gates, wo, wuv, mwo, memwo, wout, post, nh, tm):
    n, d = h.shape
    nb = (n // nbatch) // tm
    row = lambda i: (i, 0)
    kl = ctx.shape[-1]
    return pl.pallas_call(
        functools.partial(_merge_body, nh),
        grid=(n // tm,),
        in_specs=[pl.BlockSpec((tm, d), row), pl.BlockSpec((tm, og.shape[1]), row),
                  pl.BlockSpec((1, nh, tm, kl), lambda i: (i // nb, 0, i % nb, 0)),
                  pl.BlockSpec((tm, om.shape[1]), row), pl.BlockSpec((tm, 3 * d), row),
                  _resident(wo.shape), _resident(wuv.shape), _resident(mwo.shape), _resident(memwo.shape),
                  _resident(wout.shape), _resident((1, d))],
        out_specs=pl.BlockSpec((tm, d), row),
        out_shape=jax.ShapeDtypeStruct((n, d), F32),
        compiler_params=_cparams("parallel"),
        name="merge",
    )(h, og, ctx, om, gates, wo, wuv, mwo, memwo, wout, post)


def _rope_tables(pos, rope, nh):
    half = rope // 2
    freqs = ROPE_BASE ** (-jnp.arange(half, dtype=F32) / half)
    ang = pos.astype(F32)[:, None] * freqs
    cos, sin = jnp.cos(ang), jnp.sin(ang)
    return jnp.tile(jnp.concatenate([cos, cos], axis=1), (1, nh)), jnp.tile(jnp.concatenate([-sin, sin], axis=1), (1, nh))


def _prep_weights(W, d):
    nh, hd = W["rwkv_r_k"].shape
    rd = nh * hd
    lora = W["rwkv_w2"].shape[0]
    glora = W["rwkv_g2"].shape[0]
    rp = 3 * rd + 2 * lora + glora
    ql = W["mla_q_norm"].shape[0]
    kl, mh, vh = W["mla_w_uv"].shape
    nope = W["mla_w_uk"].shape[2]
    rope = W["mla_w_qb"].shape[1] // mh - nope
    md = W["mem_w_k"].shape[1]
    half = rope // 2
    row = lambda x: x.reshape(1, -1)
    w_in = W["w_in"]
    o_cq, o_kv, o_pe, o_mem, o_g = rp, rp + ql, rp + ql + kl, rp + ql + kl + rope, rp + ql + kl + rope + md
    cols = [w_in[:, :o_pe], w_in[:, o_mem:], w_in[:, o_pe:o_mem],
            w_in[:, o_pe + half:o_mem], w_in[:, o_pe:o_pe + half]]
    width = sum(c.shape[1] for c in cols)
    pad = (-width) % LANES
    win = jnp.concatenate(cols + [jnp.zeros((d, pad), F32)], axis=1).astype(BF16)
    wqb = W["mla_w_qb"].reshape(ql, mh, nope + rope)
    wqb = jnp.concatenate([wqb[:, :, :nope].reshape(ql, mh * nope),
                           wqb[:, :, nope:].reshape(ql, mh * rope),
                           jnp.concatenate([wqb[:, :, nope + half:], wqb[:, :, nope:nope + half]], axis=2).reshape(ql, mh * rope)],
                          axis=1).astype(BF16)
    ukt = jnp.transpose(W["mla_w_uk"], (1, 2, 0))
    z = jnp.zeros_like(ukt[0])
    wuk = jnp.stack([jnp.concatenate([jnp.concatenate([ukt[2 * p], z], axis=1),
                                      jnp.concatenate([z, ukt[2 * p + 1]], axis=1)], axis=0)
                     for p in range(mh // 2)]).astype(BF16)
    uv = jnp.transpose(W["mla_w_uv"], (1, 0, 2))
    zv = jnp.zeros_like(uv[0])
    wuv = jnp.stack([jnp.concatenate([uv[h], zv] if h % 2 == 0 else [zv, uv[h]], axis=1)
                     for h in range(mh)]).astype(BF16)
    zl = jnp.zeros((lora, rd), F32)
    w2a2 = jnp.concatenate([jnp.concatenate([W["rwkv_w2"], zl], axis=1),
                            jnp.concatenate([zl, W["rwkv_a2"]], axis=1)], axis=0).astype(BF16)
    hid = jnp.arange(rd) // hd
    rw = dict(mu=row(W["rwkv_mu"]), w0=row(W["rwkv_w0"]), w2a2=w2a2, a0=row(W["rwkv_a0"]), g2=W["rwkv_g2"].astype(BF16),
              k_k=row(W["rwkv_k_k"]), k_a=row(W["rwkv_k_a"]), r_k=row(W["rwkv_r_k"]), lnx_g=row(W["rwkv_lnx_g"]),
              lnx_b=row(W["rwkv_lnx_b"]), e=(hid[:, None] == hid[None, :]).astype(BF16))
    dims = (rp, ql, kl, rope, md, 3 * d, mh, nope)
    return dict(
        dims=dims, nh=nh, hd=hd, lora=lora, rw=rw, win=win, wqb=wqb, wuk=wuk, wuv=wuv,
        ffn1=(row(W["ffn1_pre"]), row(W["ffn1_post"]), W["ffn1_gate"].astype(BF16), W["ffn1_up"].astype(BF16),
              W["ffn1_down"].astype(BF16)),
        ffn2=(row(W["ffn2_pre"]), row(W["ffn2_post"]), W["ffn2_gate"].astype(BF16), W["ffn2_up"].astype(BF16),
              W["ffn2_down"].astype(BF16)),
        mix_pre=row(W["mix_pre"]), mix_post=row(W["mix_post"]), q_norm=row(W["mla_q_norm"]), kv_norm=row(W["mla_kv_norm"]),
        mem_norm=row(W["mem_norm"]), mem_wkv=jnp.concatenate([W["mem_w_k"], W["mem_w_v"]], axis=1).astype(BF16),
        rwkv_wo=W["rwkv_w_o"].astype(BF16), mla_wo=W["mla_w_o"].astype(BF16), mem_wo=W["mem_w_o"].astype(BF16),
        w_out=W["w_out"].astype(BF16), qscale=float(nope + rope) ** -0.5 * LOG2E,
    )


def _tile(n, pref):
    t = min(pref, n)
    assert n % t == 0, (n, t)
    return t


def kernel(x_prompt, x_sample, cache_mla, state_rwkv, state_shift, cache_mem_k, cache_mem_v, page_table, mem_prompt, ffn1_pre, ffn1_post, ffn1_gate, ffn1_up, ffn1_down, mix_pre, mix_post, w_in, rwkv_mu, rwkv_w0, rwkv_w2, rwkv_a0, rwkv_a2, rwkv_g2, rwkv_k_k, rwkv_k_a, rwkv_r_k, rwkv_lnx_g, rwkv_lnx_b, rwkv_w_o, mla_q_norm, mla_w_qb, mla_kv_norm, mla_w_uk, mla_w_uv, mla_w_o, mem_norm, mem_w_k, mem_w_v, mem_w_o, w_out, ffn2_pre, ffn2_post, ffn2_gate, ffn2_up, ffn2_down):
    names = ("ffn1_pre", "ffn1_post", "ffn1_gate", "ffn1_up", "ffn1_down", "mix_pre", "mix_post", "w_in",
             "rwkv_mu", "rwkv_w0", "rwkv_w2", "rwkv_a0", "rwkv_a2", "rwkv_g2", "rwkv_k_k", "rwkv_k_a", "rwkv_r_k",
             "rwkv_lnx_g", "rwkv_lnx_b", "rwkv_w_o", "mla_q_norm", "mla_w_qb", "mla_kv_norm", "mla_w_uk", "mla_w_uv",
             "mla_w_o", "mem_norm", "mem_w_k", "mem_w_v", "mem_w_o", "w_out", "ffn2_pre", "ffn2_post", "ffn2_gate",
             "ffn2_up", "ffn2_down")
    stacked = (ffn1_pre, ffn1_post, ffn1_gate, ffn1_up, ffn1_down, mix_pre, mix_post, w_in,
               rwkv_mu, rwkv_w0, rwkv_w2, rwkv_a0, rwkv_a2, rwkv_g2, rwkv_k_k, rwkv_k_a, rwkv_r_k,
               rwkv_lnx_g, rwkv_lnx_b, rwkv_w_o, mla_q_norm, mla_w_qb, mla_kv_norm, mla_w_uk, mla_w_uv,
               mla_w_o, mem_norm, mem_w_k, mem_w_v, mem_w_o, w_out, ffn2_pre, ffn2_post, ffn2_gate,
               ffn2_up, ffn2_down)
    B, S, D = x_prompt.shape
    DB, T, _ = x_sample.shape
    assert T == 1, "decode groups carry one new token per request"
    depth = ffn1_pre.shape[0]
    page = cache_mla.shape[2]
    past_len = page_table.shape[1] * page
    mem_tokens, mem_heads, mem_hd = cache_mem_k.shape[2:]
    mem_scale = float(mem_hd) ** -0.5

    cache_t = jnp.swapaxes(cache_mla, 2, 3)
    state_t = jnp.transpose(state_rwkv, (0, 2, 3, 4, 1))
    xp = x_prompt.reshape(B * S, D)
    xs = x_sample.reshape(DB, D)
    outs = [[] for _ in range(8)]
    for l in range(depth):
        P = _prep_weights({n: w[l] for n, w in zip(names, stacked)}, D)
        rp, ql, kl, rope, md, gd, mh, nope = P["dims"]
        nh, hd, lora, rw = P["nh"], P["hd"], P["lora"], P["rw"]
        cos_p, sin_p = _rope_tables(jnp.arange(S), rope, mh)
        cos_s, sin_s = _rope_tables(jnp.full((DB,), past_len), rope, mh)
        tm_p, tm_s = _tile(S, 256), DB

        mkv = _memkv(mem_prompt.reshape(B * mem_tokens, D), P["mem_norm"], P["mem_wkv"])
        mk_p, mv_p = mkv[:, :md].reshape(B, mem_tokens, md), mkv[:, md:].reshape(B, mem_tokens, md)

        h = _ffn(xp, *P["ffn1"], tm_p)
        prw, q, rows, kbf, qmem, gates = _inproj(h, B, P["mix_pre"], P["win"], P["q_norm"], P["wqb"], P["wuk"],
                                                 P["kv_norm"], cos_p, sin_p, P["dims"], P["qscale"], tm_p)
        og, wkv_p = _rwkv_prompt(prw.reshape(B, S, rp), rw, nh, hd, lora)
        ctx = _attn_prompt(q, kbf.reshape(B, S, kl + rope), kl)
        om = _memattn_prompt(qmem, mk_p, mv_p, mem_heads, mem_scale, tm_p)
        h = _merge(h, B, og.reshape(B * S, nh * hd), ctx, om, gates, P["rwkv_wo"], P["wuv"], P["mla_wo"], P["mem_wo"],
                   P["w_out"], P["mix_post"], mh, tm_p)
        xp = _ffn(h, *P["ffn2"], tm_p)
        rows_p, shift_p = rows.reshape(B, S, kl + rope), prw.reshape(B, S, rp)[:, -1]

        h = _ffn(xs, *P["ffn1"], tm_s)
        prw, q, rows, kbf, qmem, gates = _inproj(h, 1, P["mix_pre"], P["win"], P["q_norm"], P["wqb"], P["wuk"],
                                                 P["kv_norm"], cos_s, sin_s, P["dims"], P["qscale"], tm_s)
        og, wkv_t = _rwkv_sample(prw, state_shift[l], state_t, l, rw, nh, hd, lora, _tile(nh, 2))
        wkv_s = jnp.transpose(wkv_t, (3, 0, 1, 2))
        ctx = _attn_sample(jnp.swapaxes(q[0], 0, 1), kbf.reshape(DB, 1, kl + rope), cache_t, l, page_table, kl)
        om = _memattn_sample(qmem, cache_mem_k, cache_mem_v, l, mem_scale, _tile(DB, 4))
        h = _merge(h, 1, og, jnp.swapaxes(ctx, 0, 1)[None], om, gates, P["rwkv_wo"], P["wuv"], P["mla_wo"], P["mem_wo"],
                   P["w_out"], P["mix_post"], mh, tm_s)
        xs = _ffn(h, *P["ffn2"], tm_s)

        for lst, val in zip(outs, (rows_p, rows.reshape(DB, T, kl + rope), wkv_p, wkv_s, shift_p, prw,
                                   mk_p.reshape(B, mem_tokens, mem_heads, mem_hd),
                                   mv_p.reshape(B, mem_tokens, mem_heads, mem_hd))):
            lst.append(val)
    return (xp.reshape(B, S, D), xs.reshape(DB, T, D)) + tuple(jnp.stack(o) for o in outs)
```

```python
import functools

import jax
import jax.numpy as jnp
from jax import lax
from jax.experimental import pallas as pl
from jax.experimental.pallas import tpu as pltpu

F32, BF16 = jnp.float32, jnp.bfloat16
RMS_EPS = 1e-6
LNX_EPS = 64e-5
ROPE_BASE = 10000.0
LANES = 128
VMEM_LIMIT = 52 * 1024 * 1024
RWKV_CHUNK = 64
RWKV_CHUNKS_PER_STEP = 2
ATT_TQ = 256
ATT_TK = 512
ATT_ROW_GROUP = 512
PAGES_PER_GROUP = 32
LOG2E = 1.4426950408889634


def _cparams(*sem):
    return pltpu.CompilerParams(dimension_semantics=sem, vmem_limit_bytes=VMEM_LIMIT)


def _resident(shape):
    nd = len(shape)
    return pl.BlockSpec(shape, lambda *_: (0,) * nd, pipeline_mode=pl.Buffered(1))


def _rms(x, g):
    return x * lax.rsqrt(jnp.mean(x * x, axis=-1, keepdims=True) + RMS_EPS) * g


def _sigmoid(x):
    return 1.0 / (1.0 + jnp.exp(-x))


def _mm(a, b):
    return jnp.dot(a.astype(BF16), b.astype(BF16), preferred_element_type=F32)


def _mm_nt(a, b):
    return lax.dot_general(a.astype(BF16), b.astype(BF16), (((1,), (1,)), ((), ())), preferred_element_type=F32)


def _ffn_body(x_ref, pre_ref, post_ref, wg_ref, wu_ref, wd_ref, o_ref):
    x = x_ref[...]
    h = _rms(x, pre_ref[...]).astype(BF16)
    g = jnp.dot(h, wg_ref[...], preferred_element_type=F32)
    u = jnp.dot(h, wu_ref[...], preferred_element_type=F32)
    act = (g * _sigmoid(g)) * u
    y = jnp.dot(act.astype(BF16), wd_ref[...], preferred_element_type=F32)
    o_ref[...] = x + 0.5 * _rms(y, post_ref[...])


def _ffn(x, pre, post, wg, wu, wd, tm):
    n, d = x.shape
    f = wg.shape[1]
    return pl.pallas_call(
        _ffn_body,
        grid=(n // tm,),
        in_specs=[pl.BlockSpec((tm, d), lambda i: (i, 0)), _resident((1, d)), _resident((1, d)),
                  _resident((d, f)), _resident((d, f)), _resident((f, d))],
        out_specs=pl.BlockSpec((tm, d), lambda i: (i, 0)),
        out_shape=jax.ShapeDtypeStruct((n, d), F32),
        compiler_params=_cparams("parallel"),
        name="ffn",
    )(x, pre, post, wg, wu, wd)


def _inproj_body(dims, qscale, h_ref, pre_ref, win_ref, qn_ref, wqb_ref, wuk_ref, kvn_ref, cos_ref, sin_ref,
                 prw_ref, q_ref, rows_ref, kbf_ref, qmem_ref, gates_ref):
    rp, ql, kl, rope, md, gd, nh, nope = dims
    u = _rms(h_ref[...], pre_ref[...]).astype(BF16)
    p = jnp.dot(u, win_ref[...], preferred_element_type=F32)
    o = 0
    prw_ref[...] = p[:, o:o + rp]; o += rp
    cq = p[:, o:o + ql]; o += ql
    ckv = p[:, o:o + kl]; o += kl
    qmem_ref[...] = p[:, o:o + md].astype(BF16); o += md
    gates_ref[...] = _sigmoid(p[:, o:o + gd]); o += gd
    kpe = p[:, o:o + rope]; o += rope
    kpe_sw = p[:, o:o + rope]
    cos = cos_ref[...]
    sin = sin_ref[...]
    q = jnp.dot(_rms(cq, qn_ref[...]).astype(BF16), wqb_ref[...], preferred_element_type=F32)
    nn = nh * nope
    nr = nh * rope
    qpe = ((q[:, nn:nn + nr] * cos + q[:, nn + nr:nn + 2 * nr] * sin) * qscale).astype(BF16)
    qn = q[:, :nn].astype(BF16)
    for pr in range(nh // 2):
        qlat = (jnp.dot(qn[:, LANES * pr:LANES * (pr + 1)], wuk_ref[pr], preferred_element_type=F32) * qscale).astype(BF16)
        for e in range(2):
            hh = 2 * pr + e
            q_ref[0, hh, :, 0:kl] = qlat[:, kl * e:kl * (e + 1)]
            q_ref[0, hh, :, kl:kl + rope] = qpe[:, rope * hh:rope * (hh + 1)]
    ckvn = _rms(ckv, kvn_ref[...])
    kper = kpe * cos[:, :rope] + kpe_sw * sin[:, :rope]
    rows_ref[:, 0:kl] = ckvn
    rows_ref[:, kl:kl + rope] = kper
    kbf_ref[:, 0:kl] = ckvn.astype(BF16)
    kbf_ref[:, kl:kl + rope] = kper.astype(BF16)


def _inproj(h, nbatch, pre, win, qn, wqb, wuk, kvn, cos, sin, dims, qscale, tm):
    n, d = h.shape
    rp, ql, kl, rope, md, gd, nh, nope = dims
    t = n // nbatch
    nb = t // tm
    cw = win.shape[1]
    row = lambda i: (i, 0)
    tab = lambda i: (i % nb, 0)
    return pl.pallas_call(
        functools.partial(_inproj_body, dims, qscale),
        grid=(n // tm,),
        in_specs=[pl.BlockSpec((tm, d), row), _resident((1, d)), _resident((d, cw)), _resident((1, ql)),
                  _resident(wqb.shape), _resident(wuk.shape), _resident((1, kl)),
                  pl.BlockSpec((tm, nh * rope), tab), pl.BlockSpec((tm, nh * rope), tab)],
        out_specs=[pl.BlockSpec((tm, rp), row),
                   pl.BlockSpec((1, nh, tm, kl + rope), lambda i: (i // nb, 0, i % nb, 0)),
                   pl.BlockSpec((tm, kl + rope), row), pl.BlockSpec((tm, kl + rope), row),
                   pl.BlockSpec((tm, md), row), pl.BlockSpec((tm, gd), row)],
        out_shape=[jax.ShapeDtypeStruct((n, rp), F32),
                   jax.ShapeDtypeStruct((nbatch, nh, t, kl + rope), BF16),
                   jax.ShapeDtypeStruct((n, kl + rope), F32),
                   jax.ShapeDtypeStruct((n, kl + rope), BF16),
                   jax.ShapeDtypeStruct((n, md), BF16),
                   jax.ShapeDtypeStruct((n, gd), F32)],
        compiler_params=_cparams("parallel"),
        name="inproj",
    )(h, pre, win, qn, wqb, wuk, kvn, cos, sin)


def _segsum(x, e):
    hi = x.astype(BF16)
    lo = (x - hi.astype(F32)).astype(BF16)
    return jnp.dot(hi, e, preferred_element_type=F32) + jnp.dot(lo, e, preferred_element_type=F32)


def _rwkv_prep(p, prev, mu, w0, w2a2, a0, g2, k_k, k_a, e, rd, lora):
    ps = p + (prev - p) * mu
    r = ps[:, 0:rd]
    k = ps[:, rd:2 * rd]
    v = ps[:, 2 * rd:3 * rd]
    wa = ps[:, 3 * rd:3 * rd + 2 * lora]
    gl = ps[:, 3 * rd + 2 * lora:]
    lane = lax.broadcasted_iota(jnp.int32, wa.shape, 1)
    wa = jnp.where(lane < lora, jnp.tanh(wa), wa)
    wa2 = _mm(wa, w2a2)
    x = -(w0 + wa2[:, :rd])
    softplus = jnp.maximum(x, 0.0) + jnp.log(1.0 + jnp.exp(-jnp.abs(x)))
    logdec = -jnp.exp(-softplus - 0.5)
    a = _sigmoid(a0 + wa2[:, rd:])
    g = _mm(_sigmoid(gl), g2)
    kk = k * k_k
    kk = kk / jnp.maximum(jnp.sqrt(_segsum(kk * kk, e)), 1e-12)
    k = k * (1.0 + (a - 1.0) * k_a)
    return r, k, v, logdec, -kk, kk * a, g


def _rwkv_post(o, r, k, v, g, r_k, lng, lnb, e, hd):
    mean = _segsum(o, e) * (1.0 / hd)
    oc = o - mean
    var = _segsum(oc * oc, e) * (1.0 / hd)
    o = oc * lax.rsqrt(var + LNX_EPS) * lng + lnb
    bonus = _segsum(r * k * r_k, e) * v
    return (o + bonus) * g


def _pair_rows(y):
    lo = (lax.broadcasted_iota(jnp.int32, y.shape, 1) % LANES) < (LANES // 2)
    z = jnp.zeros_like(y)
    return jnp.concatenate([jnp.where(lo, y, z), jnp.where(lo, z, y)], axis=0)


def _rwkv_chunks_local(items, strict, incl):
    L = items[0][0].shape[0]
    cs = []
    for r, k, v, ld, cum, a, b in items:
        cum_l = cum[L - 1:L, :]
        e_neg = jnp.exp(-cum)
        e_end = jnp.exp(cum_l - cum)
        kt, bt = (k * e_neg).astype(BF16), (b * e_neg).astype(BF16)
        at, rt = a * jnp.exp(cum - ld), (r * jnp.exp(cum)).astype(BF16)
        cs.append(dict(at=at, rt=rt, vb=_pair_rows(v.astype(BF16)), v=v, dl=jnp.exp(cum_l),
                       bkh=jnp.concatenate([b * e_end, k * e_end], axis=0).astype(BF16),
                       lhs=jnp.concatenate([at.astype(BF16), rt], axis=0),
                       rhs=jnp.concatenate([_pair_rows(bt), _pair_rows(kt)], axis=0)))
    for c in cs:
        mm = _mm_nt(c.pop("lhs"), c.pop("rhs"))
        c["pw"] = jnp.where(strict, mm[:L, :2 * L], 0.0)
        c["m_ka"] = jnp.where(strict, mm[:L, 2 * L:], 0.0)
        c["m_r"] = jnp.concatenate([jnp.where(incl, mm[L:, :2 * L], 0.0), jnp.where(incl, mm[L:, 2 * L:], 0.0)],
                                   axis=1).astype(BF16)
    for c in cs:
        c["x"] = jnp.concatenate([c.pop("at"), _mm(c.pop("m_ka"), c["vb"])], axis=1)
    span = 1
    while span < L:
        for c in cs:
            c["x"] = c["x"] + _mm(c["pw"], _pair_rows(c["x"].astype(BF16)))
        span *= 2
        if span < L:
            for c in cs:
                c["pw"] = _mm(c["pw"], _pair_rows(c["pw"].astype(BF16)))
    for c in cs:
        x = c.pop("x")
        c["w1"], c["uloc"] = x[:, :LANES].astype(BF16), x[:, LANES:]
    return cs


def _rwkv_chunks_apply(cs, states, diag):
    sbs = [s.astype(BF16) for s in states]
    urs = [_mm_nt(c["w1"], sb) + c["uloc"] for c, sb in zip(cs, sbs)]
    o1 = [_mm_nt(c["rt"], sb) for c, sb in zip(cs, sbs)]
    upds = [_mm(jnp.concatenate([ur, c["v"]], axis=0).T, c["bkh"]) for c, ur in zip(cs, urs)]
    o2 = [_mm(c["m_r"], jnp.concatenate([_pair_rows(ur.astype(BF16)), c["vb"]], axis=0)) for c, ur in zip(cs, urs)]
    return [(a + b, s * c["dl"] + jnp.where(diag, u, 0.0)) for a, b, s, c, u in zip(o1, o2, states, cs, upds)]


def _rwkv_prompt_body(rd, lora, hd, L, p_ref, mu_ref, w0_ref, w2a2_ref, a0_ref, g2_ref, kk_ref, ka_ref, rk_ref,
                      lng_ref, lnb_ref, e_ref, og_ref, st_ref, prev_ref, s_ref):
    step = pl.program_id(0)
    nb, rows, _ = p_ref.shape
    npair = rd // LANES

    @pl.when(step == 0)
    def _():
        prev_ref[...] = jnp.zeros_like(prev_ref)
        s_ref[...] = jnp.zeros_like(s_ref)

    e = e_ref[...]
    ti = lax.broadcasted_iota(jnp.int32, (L, 2 * L), 0)
    si = lax.broadcasted_iota(jnp.int32, (L, 2 * L), 1) % L
    strict, incl = si < ti, si <= ti
    half = LANES // 2
    diag = ((lax.broadcasted_iota(jnp.int32, (LANES, LANES), 0) < half)
            == (lax.broadcasted_iota(jnp.int32, (LANES, LANES), 1) < half))
    tr = lax.broadcasted_iota(jnp.int32, (rows, rows), 0)
    tc = lax.broadcasted_iota(jnp.int32, (rows, rows), 1)
    tri = ((tc <= tr) & (tc // L == tr // L)).astype(BF16)
    rowi = lax.broadcasted_iota(jnp.int32, (rows, p_ref.shape[2]), 0)

    keys, items, vecs = [], [], []
    for bi in range(nb):
        p = p_ref[bi]
        prev = jnp.where(rowi == 0, prev_ref[bi], pltpu.roll(p, 1, axis=0))
        prev_ref[bi] = p[rows - 1:rows, :]
        r, k, v, ld, a, b, g = _rwkv_prep(p, prev, mu_ref[...], w0_ref[...], w2a2_ref[...], a0_ref[...], g2_ref[...],
                                          kk_ref[...], ka_ref[...], e, rd, lora)
        hi = ld.astype(BF16)
        r1 = ld - hi.astype(F32)
        mid = r1.astype(BF16)
        lo = (r1 - mid.astype(F32)).astype(BF16)
        cum = (jnp.dot(tri, hi, preferred_element_type=F32) + jnp.dot(tri, mid, preferred_element_type=F32)
               + jnp.dot(tri, lo, preferred_element_type=F32))
        vecs.append((r, k, v, g))
        for cc in range(rows // L):
            for pr in range(npair):
                sl = (slice(L * cc, L * (cc + 1)), slice(LANES * pr, LANES * (pr + 1)))
                keys.append((bi, cc, pr))
                items.append((r[sl], k[sl], v[sl], ld[sl], cum[sl], a[sl], b[sl]))
    local = dict(zip(keys, _rwkv_chunks_local(items, strict, incl)))
    chains = [(bi, pr) for bi in range(nb) for pr in range(npair)]
    state = {ch: s_ref[ch[0] * npair + ch[1]] for ch in chains}
    outs = {}
    for cc in range(rows // L):
        new = _rwkv_chunks_apply([local[bi, cc, pr] for bi, pr in chains], [state[ch] for ch in chains], diag)
        for ch, (o, s) in zip(chains, new):
            outs[ch, cc], state[ch] = o, s
    for bi in range(nb):
        for pr in range(npair):
            s_ref[bi * npair + pr] = state[bi, pr]
        o = jnp.concatenate([jnp.concatenate([outs[(bi, pr), cc] for cc in range(rows // L)], axis=0)
                             for pr in range(npair)], axis=1)
        r, k, v, g = vecs[bi]
        og_ref[bi] = _rwkv_post(o, r, k, v, g, rk_ref[...], lng_ref[...], lnb_ref[...], e, hd).astype(BF16)

    @pl.when(step == pl.num_programs(0) - 1)
    def _():
        for bi in range(nb):
            for pr in range(npair):
                s = s_ref[bi * npair + pr]
                st_ref[bi, 2 * pr] = s[:hd, :hd]
                st_ref[bi, 2 * pr + 1] = s[hd:, hd:]


def _rwkv_prompt(prw, rw, nh, hd, lora):
    b, t, pw = prw.shape
    rd = nh * hd
    rows = _tile(t, RWKV_CHUNK * RWKV_CHUNKS_PER_STEP)
    names = ("mu", "w0", "w2a2", "a0", "g2", "k_k", "k_a", "r_k", "lnx_g", "lnx_b", "e")
    return pl.pallas_call(
        functools.partial(_rwkv_prompt_body, rd, lora, hd, RWKV_CHUNK),
        grid=(t // rows,),
        in_specs=[pl.BlockSpec((b, rows, pw), lambda c: (0, c, 0))] + [_resident(rw[k].shape) for k in names],
        out_specs=[pl.BlockSpec((b, rows, rd), lambda c: (0, c, 0)),
                   pl.BlockSpec((b, nh, hd, hd), lambda c: (0, 0, 0, 0))],
        out_shape=[jax.ShapeDtypeStruct((b, t, rd), BF16), jax.ShapeDtypeStruct((b, nh, hd, hd), F32)],
        scratch_shapes=[pltpu.VMEM((b, 1, pw), F32), pltpu.VMEM((b * (rd // LANES), LANES, LANES), F32)],
        compiler_params=_cparams("arbitrary"),
        name="rwkv_prompt",
    )(prw, *[rw[k] for k in names])


def _rwkv_prep_body(rd, lora, p_ref, prev_ref, mu_ref, w0_ref, w2a2_ref, a0_ref, g2_ref, kk_ref, ka_ref, e_ref,
                    r_ref, k_ref, v_ref, g_ref, *t_refs):
    r, k, v, ld, a, b, g = _rwkv_prep(p_ref[...], prev_ref[...], mu_ref[...], w0_ref[...], w2a2_ref[...], a0_ref[...],
                                      g2_ref[...], kk_ref[...], ka_ref[...], e_ref[...], rd, lora)
    r_ref[...] = r
    k_ref[...] = k
    v_ref[...] = v
    g_ref[...] = g
    for ref, x in zip(t_refs, (r, k, v, jnp.exp(ld), a, b)):
        ref[...] = x.T


def _rwkv_step_body(s_ref, r_ref, k_ref, v_ref, w_ref, a_ref, b_ref, so_ref, o_ref):
    for h in range(s_ref.shape[0]):
        s = s_ref[h]
        sa = jnp.sum(s * a_ref[h][None], axis=1)
        s = s * w_ref[h][None] + sa[:, None, :] * b_ref[h][None] + v_ref[h][:, None, :] * k_ref[h][None]
        so_ref[h] = s
        o_ref[h] = jnp.sum(s * r_ref[h][None], axis=1)


def _rwkv_post_body(hd, o_ref, r_ref, k_ref, v_ref, g_ref, rk_ref, lng_ref, lnb_ref, e_ref, og_ref):
    og_ref[...] = _rwkv_post(o_ref[...].T, r_ref[...], k_ref[...], v_ref[...], g_ref[...], rk_ref[...], lng_ref[...],
                             lnb_ref[...], e_ref[...], hd).astype(BF16)


def _rwkv_sample(prw, shift, state_t, layer, rw, nh, hd, lora, hb):
    n, pw = prw.shape
    rd = nh * hd
    names = ("mu", "w0", "w2a2", "a0", "g2", "k_k", "k_a", "e")
    full = lambda s: pl.BlockSpec(s, lambda: (0,) * len(s))
    vecs = pl.pallas_call(
        functools.partial(_rwkv_prep_body, rd, lora),
        in_specs=[full((n, pw)), full((n, pw))] + [full(rw[k].shape) for k in names],
        out_specs=[full((n, rd))] * 4 + [full((rd, n))] * 6,
        out_shape=[jax.ShapeDtypeStruct((n, rd), F32)] * 4 + [jax.ShapeDtypeStruct((rd, n), F32)] * 6,
        name="rwkv_prep",
    )(prw, shift, *[rw[k] for k in names])
    r, k, v, g = vecs[:4]
    vspec = pl.BlockSpec((hb, hd, n), lambda i: (i, 0, 0))
    sspec = pl.BlockSpec((hb, hd, hd, n), lambda i: (i, 0, 0, 0))
    s_new, o = pl.pallas_call(
        _rwkv_step_body,
        grid=(nh // hb,),
        in_specs=[pl.BlockSpec((None, hb, hd, hd, n), lambda i: (layer, i, 0, 0, 0))] + [vspec] * 6,
        out_specs=[sspec, vspec],
        out_shape=[jax.ShapeDtypeStruct(state_t.shape[1:], F32), jax.ShapeDtypeStruct((nh, hd, n), F32)],
        compiler_params=_cparams("parallel"),
        name="rwkv_step",
    )(state_t, *[x.reshape(nh, hd, n) for x in vecs[4:]])
    pnames = ("r_k", "lnx_g", "lnx_b", "e")
    og = pl.pallas_call(
        functools.partial(_rwkv_post_body, hd),
        in_specs=[full((rd, n))] + [full((n, rd))] * 4 + [full(rw[k].shape) for k in pnames],
        out_specs=full((n, rd)),
        out_shape=jax.ShapeDtypeStruct((n, rd), BF16),
        name="rwkv_post",
    )(o.reshape(rd, n), r, k, v, g, *[rw[k] for k in pnames])
    return og, s_new


def _lanes(x, n):
    return x if n == LANES else jnp.concatenate([x] * (n // LANES), axis=1)


def _attn_prompt_body(kl, tk, rg, q_ref, k_ref, o_ref, m_ref, l_ref, acc_ref, s_ref):
    i = pl.program_id(1)
    nh, tq, dk = q_ref.shape[1:]
    rows = nh * tq
    q = q_ref[0].reshape(rows, dk)
    groups = [slice(g * rg, (g + 1) * rg) for g in range(rows // rg)]
    m_ref[...] = jnp.full_like(m_ref, -jnp.inf)
    l_ref[...] = jnp.zeros_like(l_ref)
    acc_ref[...] = jnp.zeros_like(acc_ref)

    def keys(j):
        return k_ref[0, pl.ds(pl.multiple_of(j * tk, tk), tk), :]

    def scores(r, k):
        return lax.dot_general(q[r], k, (((1,), (1,)), ((), ())), preferred_element_type=F32)

    def update(r, s, v):
        m_old = m_ref[r]
        m_new = jnp.maximum(m_old, jnp.max(s, axis=-1, keepdims=True))
        alpha = jnp.exp2(m_old - m_new)
        p = jnp.exp2(s - _lanes(m_new, tk))
        l_ref[r] = alpha * l_ref[r] + jnp.sum(p, axis=-1, keepdims=True)
        acc_ref[r] = _lanes(alpha, kl) * acc_ref[r] + jnp.dot(p.astype(BF16), v, preferred_element_type=F32)
        m_ref[r] = m_new

    k0 = keys(0)
    for r in groups:
        s_ref[r] = scores(r, k0)

    def body(j, carry):
        v = keys(j)[:, :kl]
        k_next = keys(j + 1)
        for r in groups:
            s = s_ref[r]
            s_ref[r] = scores(r, k_next)
            update(r, s, v)
        return carry

    last = (i * tq) // tk
    lax.fori_loop(0, last, body, 0)
    v = keys(last)[:, :kl]
    for g, r in enumerate(groups):
        s = s_ref[r]
        qpos = i * tq + (g * rg + lax.broadcasted_iota(jnp.int32, s.shape, 0)) % tq
        kpos = last * tk + lax.broadcasted_iota(jnp.int32, s.shape, 1)
        update(r, jnp.where(kpos <= qpos, s, -jnp.inf), v)
    o_ref[0] = (acc_ref[...] / _lanes(l_ref[...], kl)).reshape(nh, tq, kl).astype(o_ref.dtype)


def _attn_prompt(q, kbf, kl):
    b, nh, t, dk = q.shape
    tq, tk = _tile(t, ATT_TQ), _tile(t, ATT_TK)
    rows = nh * tq
    rg = _tile(rows, ATT_ROW_GROUP)
    assert tk % tq == 0 and rg % tq == 0
    return pl.pallas_call(
        functools.partial(_attn_prompt_body, kl, tk, rg),
        grid=(b, t // tq),
        in_specs=[pl.BlockSpec((1, nh, tq, dk), lambda bi, i: (bi, 0, i, 0)),
                  pl.BlockSpec((1, t, dk), lambda bi, i: (bi, 0, 0))],
        out_specs=pl.BlockSpec((1, nh, tq, kl), lambda bi, i: (bi, 0, i, 0)),
        out_shape=jax.ShapeDtypeStruct((b, nh, t, kl), BF16),
        scratch_shapes=[pltpu.VMEM((rows, LANES), F32), pltpu.VMEM((rows, LANES), F32), pltpu.VMEM((rows, kl), F32),
                        pltpu.VMEM((rows, tk), F32)],
        compiler_params=_cparams("parallel", "arbitrary"),
        name="attn_prompt",
    )(q, kbf)


def _attn_sample_body(kl, npg, ngrp, layer, pt_ref, q_ref, kself_ref, cache_ref, o_ref, buf_ref, sem_ref):
    b = pl.program_id(0)
    nreq = pl.num_programs(0)

    def copies(req, grp, slot):
        return [pltpu.make_async_copy(cache_ref.at[layer, pt_ref[req, grp * npg + n]], buf_ref.at[slot, n],
                                      sem_ref.at[slot]) for n in range(npg)]

    def slot_of(grp):
        return grp % 2 if ngrp % 2 == 0 else (b * ngrp + grp) % 2

    @pl.when(b == 0)
    def _():
        for c in copies(0, 0, 0):
            c.start()

    q = q_ref[0]
    ks = kself_ref[0]
    m = jnp.sum(q.astype(F32) * ks.astype(F32), axis=-1, keepdims=True)
    l = jnp.ones_like(m)
    acc = jnp.broadcast_to(ks[:, :kl].astype(F32), (q.shape[0], kl))
    for g in range(ngrp):
        slot = slot_of(g)
        if g + 1 < ngrp:
            for c in copies(b, g + 1, 1 - slot):
                c.start()
        else:
            @pl.when(b + 1 < nreq)
            def _():
                for c in copies(b + 1, 0, 1 - slot):
                    c.start()
        for c in copies(b, g, slot):
            c.wait()
        pair = 2 if npg % 2 == 0 else 1
        kts = [jnp.concatenate([buf_ref[slot, n + e] for e in range(pair)], axis=1).astype(BF16)
               for n in range(0, npg, pair)]
        s = jnp.concatenate([jnp.dot(q, kt, preferred_element_type=F32) for kt in kts], axis=1)
        m_new = jnp.maximum(m, jnp.max(s, axis=-1, keepdims=True))
        alpha = jnp.exp2(m - m_new)
        p = jnp.exp2(s - m_new)
        l = alpha * l + jnp.sum(p, axis=-1, keepdims=True)
        pb = p.astype(BF16)
        w = kts[0].shape[1]
        pv = _mm_nt(pb[:, :w], kts[0][:kl, :])
        for n in range(1, len(kts)):
            pv = pv + _mm_nt(pb[:, n * w:(n + 1) * w], kts[n][:kl, :])
        acc = alpha * acc + pv
        m = m_new
    o_ref[0] = (acc / l).astype(o_ref.dtype)


def _attn_sample(q, kself, cache_t, layer, page_table, kl):
    n, nh, dk = q.shape
    ps = cache_t.shape[3]
    npages = page_table.shape[1]
    npg = _tile(npages, PAGES_PER_GROUP)
    grid_spec = pltpu.PrefetchScalarGridSpec(
        num_scalar_prefetch=1,
        grid=(n,),
        in_specs=[pl.BlockSpec((1, nh, dk), lambda bi, pt: (bi, 0, 0)),
                  pl.BlockSpec((1, 1, dk), lambda bi, pt: (bi, 0, 0)),
                  pl.BlockSpec(memory_space=pl.ANY)],
        out_specs=pl.BlockSpec((1, nh, kl), lambda bi, pt: (bi, 0, 0)),
        scratch_shapes=[pltpu.VMEM((2, npg, dk, ps), F32), pltpu.SemaphoreType.DMA((2,))],
    )
    return pl.pallas_call(
        functools.partial(_attn_sample_body, kl, npg, npages // npg, layer),
        grid_spec=grid_spec,
        out_shape=jax.ShapeDtypeStruct((n, nh, kl), BF16),
        compiler_params=_cparams("arbitrary"),
        name="attn_sample",
    )(page_table, q, kself, cache_t)


def _memkv_body(m_ref, g_ref, w_ref, o_ref):
    o_ref[...] = jnp.dot(_rms(m_ref[...], g_ref[...]).astype(BF16), w_ref[...], preferred_element_type=F32)


def _memkv(mem, g, wkv):
    n, d = mem.shape
    full = lambda s: pl.BlockSpec(s, lambda: (0,) * len(s))
    return pl.pallas_call(
        _memkv_body,
        in_specs=[full((n, d)), full((1, d)), full(wkv.shape)],
        out_specs=full((n, wkv.shape[1])),
        out_shape=jax.ShapeDtypeStruct((n, wkv.shape[1]), F32),
        name="mem_kv",
    )(mem, g, wkv)


def _mem_attend(q, k, v, scale):
    s = _mm_nt(q, k) * scale
    p = jnp.exp(s - jnp.max(s, axis=-1, keepdims=True))
    p = p / jnp.sum(p, axis=-1, keepdims=True)
    return _mm(p, v)


def _memattn_prompt_body(nh, scale, q_ref, k_ref, v_ref, o_ref):
    hd = q_ref.shape[1] // nh
    q, k, v = q_ref[...], k_ref[0], v_ref[0]
    heads = [slice(h * hd, (h + 1) * hd) for h in range(nh)]
    o_ref[...] = jnp.concatenate([_mem_attend(q[:, sl], k[:, sl], v[:, sl], scale) for sl in heads],
                                 axis=1).astype(o_ref.dtype)


def _memattn_prompt(q, mk, mv, nh, scale, tm):
    n, md = q.shape
    b, m, _ = mk.shape
    nb = (n // b) // tm
    kv = pl.BlockSpec((1, m, md), lambda i: (i // nb, 0, 0))
    return pl.pallas_call(
        functools.partial(_memattn_prompt_body, nh, scale),
        grid=(n // tm,),
        in_specs=[pl.BlockSpec((tm, md), lambda i: (i, 0)), kv, kv],
        out_specs=pl.BlockSpec((tm, md), lambda i: (i, 0)),
        out_shape=jax.ShapeDtypeStruct((n, md), BF16),
        compiler_params=_cparams("parallel"),
        name="memattn_prompt",
    )(q, mk, mv)


def _memattn_sample_body(scale, q_ref, k_ref, v_ref, o_ref):
    gr, rows, _ = k_ref.shape
    nh = q_ref.shape[1]
    own = (lax.broadcasted_iota(jnp.int32, (nh, rows), 1) % nh) == lax.broadcasted_iota(jnp.int32, (nh, rows), 0)
    ss = [jnp.where(own, _mm_nt(q_ref[g], k_ref[g]) * scale, -jnp.inf) for g in range(gr)]
    ps = [jnp.exp(s - jnp.max(s, axis=-1, keepdims=True)) for s in ss]
    ps = [p / jnp.sum(p, axis=-1, keepdims=True) for p in ps]
    for g in range(gr):
        o_ref[g] = _mm(ps[g], v_ref[g]).astype(o_ref.dtype)


def _memattn_sample(q, cache_k, cache_v, layer, scale, gr):
    n, md = q.shape
    depth, _, m, nh, hd = cache_k.shape
    kv = pl.BlockSpec((None, gr, m * nh, hd), lambda i: (layer, i, 0, 0))
    qs = pl.BlockSpec((gr, nh, hd), lambda i: (i, 0, 0))
    return pl.pallas_call(
        functools.partial(_memattn_sample_body, scale),
        grid=(n // gr,),
        in_specs=[qs, kv, kv],
        out_specs=qs,
        out_shape=jax.ShapeDtypeStruct((n, nh, hd), BF16),
        compiler_params=_cparams("parallel"),
        name="memattn_sample",
    )(q.reshape(n, nh, hd), cache_k.reshape(depth, n, m * nh, hd), cache_v.reshape(depth, n, m * nh, hd)).reshape(n, md)


def _merge_body(nh, h_ref, og_ref, ctx_ref, om_ref, gates_ref, wo_ref, wuv_ref, mwo_ref, memwo_ref, wout_ref, post_ref,
                o_ref):
    d = h_ref.shape[1]
    o_rwkv = jnp.dot(og_ref[...], wo_ref[...], preferred_element_type=F32)
    vs = []
    for pr in range(nh // 2):
        vp = (jnp.dot(ctx_ref[0, 2 * pr], wuv_ref[2 * pr], preferred_element_type=F32)
              + jnp.dot(ctx_ref[0, 2 * pr + 1], wuv_ref[2 * pr + 1], preferred_element_type=F32))
        vs.append(vp.astype(BF16))
    o_mla = jnp.dot(jnp.concatenate(vs, axis=1), mwo_ref[...], preferred_element_type=F32)
    o_mem = jnp.dot(om_ref[...], memwo_ref[...], preferred_element_type=F32)
    merged = gates_ref[:, 0:d] * o_rwkv + gates_ref[:, d:2 * d] * o_mla + gates_ref[:, 2 * d:3 * d] * o_mem
    y = jnp.dot(merged.astype(BF16), wout_ref[...], preferred_element_type=F32)
    o_ref[...] = h_ref[...] + _rms(y, post_ref[...])


def _merge(h, nbatch, og, ctx, om, gates, wo, wuv, mwo, memwo, wout, post, nh, tm):
    n, d = h.shape
    nb = (n // nbatch) // tm
    row = lambda i: (i, 0)
    kl = ctx.shape[-1]
    return pl.pallas_call(
        functools.partial(_merge_body, nh),
        grid=(n // tm,),
        in_specs=[pl.BlockSpec((tm, d), row), pl.BlockSpec((tm, og.shape[1]), row),
                  pl.BlockSpec((1, nh, tm, kl), lambda i: (i // nb, 0, i % nb, 0)),
                  pl.BlockSpec((tm, om.shape[1]), row), pl.BlockSpec((tm, 3 * d), row),
                  _resident(wo.shape), _resident(wuv.shape), _resident(mwo.shape), _resident(memwo.shape),
                  _resident(wout.shape), _resident((1, d))],
        out_specs=pl.BlockSpec((tm, d), row),
        out_shape=jax.ShapeDtypeStruct((n, d), F32),
        compiler_params=_cparams("parallel"),
        name="merge",
    )(h, og, ctx, om, gates, wo, wuv, mwo, memwo, wout, post)


def _rope_tables(pos, rope, nh):
    half = rope // 2
    freqs = ROPE_BASE ** (-jnp.arange(half, dtype=F32) / half)
    ang = pos.astype(F32)[:, None] * freqs
    cos, sin = jnp.cos(ang), jnp.sin(ang)
    return jnp.tile(jnp.concatenate([cos, cos], axis=1), (1, nh)), jnp.tile(jnp.concatenate([-sin, sin], axis=1), (1, nh))


def _prep_weights(W, d):
    nh, hd = W["rwkv_r_k"].shape
    rd = nh * hd
    lora = W["rwkv_w2"].shape[0]
    glora = W["rwkv_g2"].shape[0]
    rp = 3 * rd + 2 * lora + glora
    ql = W["mla_q_norm"].shape[0]
    kl, mh, vh = W["mla_w_uv"].shape
    nope = W["mla_w_uk"].shape[2]
    rope = W["mla_w_qb"].shape[1] // mh - nope
    md = W["mem_w_k"].shape[1]
    half = rope // 2
    row = lambda x: x.reshape(1, -1)
    w_in = W["w_in"]
    o_cq, o_kv, o_pe, o_mem, o_g = rp, rp + ql, rp + ql + kl, rp + ql + kl + rope, rp + ql + kl + rope + md
    cols = [w_in[:, :o_pe], w_in[:, o_mem:], w_in[:, o_pe:o_mem],
            w_in[:, o_pe + half:o_mem], w_in[:, o_pe:o_pe + half]]
    width = sum(c.shape[1] for c in cols)
    pad = (-width) % LANES
    win = jnp.concatenate(cols + [jnp.zeros((d, pad), F32)], axis=1).astype(BF16)
    wqb = W["mla_w_qb"].reshape(ql, mh, nope + rope)
    wqb = jnp.concatenate([wqb[:, :, :nope].reshape(ql, mh * nope),
                           wqb[:, :, nope:].reshape(ql, mh * rope),
                           jnp.concatenate([wqb[:, :, nope + half:], wqb[:, :, nope:nope + half]], axis=2).reshape(ql, mh * rope)],
                          axis=1).astype(BF16)
    ukt = jnp.transpose(W["mla_w_uk"], (1, 2, 0))
    z = jnp.zeros_like(ukt[0])
    wuk = jnp.stack([jnp.concatenate([jnp.concatenate([ukt[2 * p], z], axis=1),
                                      jnp.concatenate([z, ukt[2 * p + 1]], axis=1)], axis=0)
                     for p in range(mh // 2)]).astype(BF16)
    uv = jnp.transpose(W["mla_w_uv"], (1, 0, 2))
    zv = jnp.zeros_like(uv[0])
    wuv = jnp.stack([jnp.concatenate([uv[h], zv] if h % 2 == 0 else [zv, uv[h]], axis=1)
                     for h in range(mh)]).astype(BF16)
    zl = jnp.zeros((lora, rd), F32)
    w2a2 = jnp.concatenate([jnp.concatenate([W["rwkv_w2"], zl], axis=1),
                            jnp.concatenate([zl, W["rwkv_a2"]], axis=1)], axis=0).astype(BF16)
    hid = jnp.arange(rd) // hd
    rw = dict(mu=row(W["rwkv_mu"]), w0=row(W["rwkv_w0"]), w2a2=w2a2, a0=row(W["rwkv_a0"]), g2=W["rwkv_g2"].astype(BF16),
              k_k=row(W["rwkv_k_k"]), k_a=row(W["rwkv_k_a"]), r_k=row(W["rwkv_r_k"]), lnx_g=row(W["rwkv_lnx_g"]),
              lnx_b=row(W["rwkv_lnx_b"]), e=(hid[:, None] == hid[None, :]).astype(BF16))
    dims = (rp, ql, kl, rope, md, 3 * d, mh, nope)
    return dict(
        dims=dims, nh=nh, hd=hd, lora=lora, rw=rw, win=win, wqb=wqb, wuk=wuk, wuv=wuv,
        ffn1=(row(W["ffn1_pre"]), row(W["ffn1_post"]), W["ffn1_gate"].astype(BF16), W["ffn1_up"].astype(BF16),
              W["ffn1_down"].astype(BF16)),
        ffn2=(row(W["ffn2_pre"]), row(W["ffn2_post"]), W["ffn2_gate"].astype(BF16), W["ffn2_up"].astype(BF16),
              W["ffn2_down"].astype(BF16)),
        mix_pre=row(W["mix_pre"]), mix_post=row(W["mix_post"]), q_norm=row(W["mla_q_norm"]), kv_norm=row(W["mla_kv_norm"]),
        mem_norm=row(W["mem_norm"]), mem_wkv=jnp.concatenate([W["mem_w_k"], W["mem_w_v"]], axis=1).astype(BF16),
        rwkv_wo=W["rwkv_w_o"].astype(BF16), mla_wo=W["mla_w_o"].astype(BF16), mem_wo=W["mem_w_o"].astype(BF16),
        w_out=W["w_out"].astype(BF16), qscale=float(nope + rope) ** -0.5 * LOG2E,
    )


def _tile(n, pref):
    t = min(pref, n)
    assert n % t == 0, (n, t)
    return t


def kernel(x_prompt, x_sample, cache_mla, state_rwkv, state_shift, cache_mem_k, cache_mem_v, page_table, mem_prompt, ffn1_pre, ffn1_post, ffn1_gate, ffn1_up, ffn1_down, mix_pre, mix_post, w_in, rwkv_mu, rwkv_w0, rwkv_w2, rwkv_a0, rwkv_a2, rwkv_g2, rwkv_k_k, rwkv_k_a, rwkv_r_k, rwkv_lnx_g, rwkv_lnx_b, rwkv_w_o, mla_q_norm, mla_w_qb, mla_kv_norm, mla_w_uk, mla_w_uv, mla_w_o, mem_norm, mem_w_k, mem_w_v, mem_w_o, w_out, ffn2_pre, ffn2_post, ffn2_gate, ffn2_up, ffn2_down):
    names = ("ffn1_pre", "ffn1_post", "ffn1_gate", "ffn1_up", "ffn1_down", "mix_pre", "mix_post", "w_in",
             "rwkv_mu", "rwkv_w0", "rwkv_w2", "rwkv_a0", "rwkv_a2", "rwkv_g2", "rwkv_k_k", "rwkv_k_a", "rwkv_r_k",
             "rwkv_lnx_g", "rwkv_lnx_b", "rwkv_w_o", "mla_q_norm", "mla_w_qb", "mla_kv_norm", "mla_w_uk", "mla_w_uv",
             "mla_w_o", "mem_norm", "mem_w_k", "mem_w_v", "mem_w_o", "w_out", "ffn2_pre", "ffn2_post", "ffn2_gate",
             "ffn2_up", "ffn2_down")
    stacked = (ffn1_pre, ffn1_post, ffn1_gate, ffn1_up, ffn1_down, mix_pre, mix_post, w_in,
               rwkv_mu, rwkv_w0, rwkv_w2, rwkv_a0, rwkv_a2, rwkv_g2, rwkv_k_k, rwkv_k_a, rwkv_r_k,
               rwkv_lnx_g, rwkv_lnx_b, rwkv_w_o, mla_q_norm, mla_w_qb, mla_kv_norm, mla_w_uk, mla_w_uv,
               mla_w_o, mem_norm, mem_w_k, mem_w_v, mem_w_o, w_out, ffn2_pre, ffn2_post, ffn2_gate,
               ffn2_up, ffn2_down)
    B, S, D = x_prompt.shape
    DB, T, _ = x_sample.shape
    assert T == 1, "decode groups carry one new token per request"
    depth = ffn1_pre.shape[0]
    page = cache_mla.shape[2]
    past_len = page_table.shape[1] * page
    mem_tokens, mem_heads, mem_hd = cache_mem_k.shape[2:]
    mem_scale = float(mem_hd) ** -0.5

    cache_t = jnp.swapaxes(cache_mla, 2, 3)
    state_t = jnp.transpose(state_rwkv, (0, 2, 3, 4, 1))
    xp = x_prompt.reshape(B * S, D)
    xs = x_sample.reshape(DB, D)
    outs = [[] for _ in range(8)]
    for l in range(depth):
        P = _prep_weights({n: w[l] for n, w in zip(names, stacked)}, D)
        rp, ql, kl, rope, md, gd, mh, nope = P["dims"]
        nh, hd, lora, rw = P["nh"], P["hd"], P["lora"], P["rw"]
        cos_p, sin_p = _rope_tables(jnp.arange(S), rope, mh)
        cos_s, sin_s = _rope_tables(jnp.full((DB,), past_len), rope, mh)
        tm_p, tm_s = _tile(S, 256), DB

        h = _ffn(xs, *P["ffn1"], tm_s)
        prw_s, q, rows_s, kbf, qmem, gates = _inproj(h, 1, P["mix_pre"], P["win"], P["q_norm"], P["wqb"], P["wuk"],
                                                     P["kv_norm"], cos_s, sin_s, P["dims"], P["qscale"], tm_s)
        og, wkv_t = _rwkv_sample(prw_s, state_shift[l], state_t, l, rw, nh, hd, lora, _tile(nh, 2))
        wkv_s = jnp.transpose(wkv_t, (3, 0, 1, 2))
        ctx = _attn_sample(jnp.swapaxes(q[0], 0, 1), kbf.reshape(DB, 1, kl + rope), cache_t, l, page_table, kl)
        om = _memattn_sample(qmem, cache_mem_k, cache_mem_v, l, mem_scale, _tile(DB, 4))
        h = _merge(h, 1, og, jnp.swapaxes(ctx, 0, 1)[None], om, gates, P["rwkv_wo"], P["wuv"], P["mla_wo"], P["mem_wo"],
                   P["w_out"], P["mix_post"], mh, tm_s)
        xs = _ffn(h, *P["ffn2"], tm_s)

        mkv = _memkv(mem_prompt.reshape(B * mem_tokens, D), P["mem_norm"], P["mem_wkv"])
        mk_p, mv_p = mkv[:, :md].reshape(B, mem_tokens, md), mkv[:, md:].reshape(B, mem_tokens, md)
        h = _ffn(xp, *P["ffn1"], tm_p)
        prw, q, rows, kbf, qmem, gates = _inproj(h, B, P["mix_pre"], P["win"], P["q_norm"], P["wqb"], P["wuk"],
                                                 P["kv_norm"], cos_p, sin_p, P["dims"], P["qscale"], tm_p)
        og, wkv_p = _rwkv_prompt(prw.reshape(B, S, rp), rw, nh, hd, lora)
        ctx = _attn_prompt(q, kbf.reshape(B, S, kl + rope), kl)
        om = _memattn_prompt(qmem, mk_p, mv_p, mem_heads, mem_scale, tm_p)
        h = _merge(h, B, og.reshape(B * S, nh * hd), ctx, om, gates, P["rwkv_wo"], P["wuv"], P["mla_wo"], P["mem_wo"],
                   P["w_out"], P["mix_post"], mh, tm_p)
        xp = _ffn(h, *P["ffn2"], tm_p)
        rows_p, shift_p = rows.reshape(B, S, kl + rope), prw.reshape(B, S, rp)[:, -1]

        for lst, val in zip(outs, (rows_p, rows_s.reshape(DB, T, kl + rope), wkv_p, wkv_s, shift_p, prw_s,
                                   mk_p.reshape(B, mem_tokens, mem_heads, mem_hd),
                                   mv_p.reshape(B, mem_tokens, mem_heads, mem_hd))):
            lst.append(val)
    return (xp.reshape(B, S, D), xs.reshape(DB, T, D)) + tuple(jnp.stack(o) for o in outs)
```

```python
import functools

import jax
import jax.numpy as jnp
from jax import lax
from jax.experimental import pallas as pl
from jax.experimental.pallas import tpu as pltpu

F32, BF16 = jnp.float32, jnp.bfloat16
RMS_EPS = 1e-6
LNX_EPS = 64e-5
ROPE_BASE = 10000.0
LANES = 128
VMEM_LIMIT = 52 * 1024 * 1024
ROWS_PER_STEP = 256
ROWS_PER_STEP_WIDE = 512
RWKV_CHUNK = 64
RWKV_CHUNKS_PER_STEP = 2
ATT_TQ = 256
ATT_TK = 512
ATT_ROW_GROUP = 512
PAGES_PER_GROUP = 32
LOG2E = 1.4426950408889634


def _cparams(*sem):
    return pltpu.CompilerParams(dimension_semantics=sem, vmem_limit_bytes=VMEM_LIMIT)


def _resident(shape):
    nd = len(shape)
    return pl.BlockSpec(shape, lambda *_: (0,) * nd, pipeline_mode=pl.Buffered(1))


def _rms(x, g):
    return x * lax.rsqrt(jnp.mean(x * x, axis=-1, keepdims=True) + RMS_EPS) * g


def _sigmoid(x):
    return 1.0 / (1.0 + jnp.exp(-x))


def _mm(a, b):
    return jnp.dot(a.astype(BF16), b.astype(BF16), preferred_element_type=F32)


def _mm_nt(a, b):
    return lax.dot_general(a.astype(BF16), b.astype(BF16), (((1,), (1,)), ((), ())), preferred_element_type=F32)


def _ffn_body(x_ref, pre_ref, post_ref, wg_ref, wu_ref, wd_ref, o_ref):
    x = x_ref[...]
    h = _rms(x, pre_ref[...]).astype(BF16)
    g = jnp.dot(h, wg_ref[...], preferred_element_type=F32)
    u = jnp.dot(h, wu_ref[...], preferred_element_type=F32)
    act = (g * _sigmoid(g)) * u
    y = jnp.dot(act.astype(BF16), wd_ref[...], preferred_element_type=F32)
    o_ref[...] = x + 0.5 * _rms(y, post_ref[...])


def _ffn(x, pre, post, wg, wu, wd, tm):
    n, d = x.shape
    f = wg.shape[1]
    return pl.pallas_call(
        _ffn_body,
        grid=(n // tm,),
        in_specs=[pl.BlockSpec((tm, d), lambda i: (i, 0)), _resident((1, d)), _resident((1, d)),
                  _resident((d, f)), _resident((d, f)), _resident((f, d))],
        out_specs=pl.BlockSpec((tm, d), lambda i: (i, 0)),
        out_shape=jax.ShapeDtypeStruct((n, d), F32),
        compiler_params=_cparams("parallel"),
        name="ffn",
    )(x, pre, post, wg, wu, wd)


def _inproj_body(dims, qscale, h_ref, pre_ref, win_ref, qn_ref, wqb_ref, wuk_ref, kvn_ref, cos_ref, sin_ref,
                 prw_ref, q_ref, rows_ref, kbf_ref, qmem_ref, gates_ref):
    rp, ql, kl, rope, md, gd, nh, nope = dims
    u = _rms(h_ref[...], pre_ref[...]).astype(BF16)
    p = jnp.dot(u, win_ref[...], preferred_element_type=F32)
    o = 0
    prw_ref[...] = p[:, o:o + rp]; o += rp
    cq = p[:, o:o + ql]; o += ql
    ckv = p[:, o:o + kl]; o += kl
    qmem_ref[...] = p[:, o:o + md].astype(BF16); o += md
    gates_ref[...] = _sigmoid(p[:, o:o + gd]); o += gd
    kpe = p[:, o:o + rope]; o += rope
    kpe_sw = p[:, o:o + rope]
    cos = cos_ref[...]
    sin = sin_ref[...]
    q = jnp.dot(_rms(cq, qn_ref[...]).astype(BF16), wqb_ref[...], preferred_element_type=F32)
    nn = nh * nope
    nr = nh * rope
    qpe = ((q[:, nn:nn + nr] * cos + q[:, nn + nr:nn + 2 * nr] * sin) * qscale).astype(BF16)
    qn = q[:, :nn].astype(BF16)
    for pr in range(nh // 2):
        qlat = (jnp.dot(qn[:, LANES * pr:LANES * (pr + 1)], wuk_ref[pr], preferred_element_type=F32) * qscale).astype(BF16)
        for e in range(2):
            hh = 2 * pr + e
            q_ref[0, hh, :, 0:kl] = qlat[:, kl * e:kl * (e + 1)]
            q_ref[0, hh, :, kl:kl + rope] = qpe[:, rope * hh:rope * (hh + 1)]
    ckvn = _rms(ckv, kvn_ref[...])
    kper = kpe * cos[:, :rope] + kpe_sw * sin[:, :rope]
    rows_ref[:, 0:kl] = ckvn
    rows_ref[:, kl:kl + rope] = kper
    kbf_ref[:, 0:kl] = ckvn.astype(BF16)
    kbf_ref[:, kl:kl + rope] = kper.astype(BF16)


def _inproj(h, nbatch, pre, win, qn, wqb, wuk, kvn, cos, sin, dims, qscale, tm):
    n, d = h.shape
    rp, ql, kl, rope, md, gd, nh, nope = dims
    t = n // nbatch
    nb = t // tm
    cw = win.shape[1]
    row = lambda i: (i, 0)
    tab = lambda i: (i % nb, 0)
    return pl.pallas_call(
        functools.partial(_inproj_body, dims, qscale),
        grid=(n // tm,),
        in_specs=[pl.BlockSpec((tm, d), row), _resident((1, d)), _resident((d, cw)), _resident((1, ql)),
                  _resident(wqb.shape), _resident(wuk.shape), _resident((1, kl)),
                  pl.BlockSpec((tm, nh * rope), tab), pl.BlockSpec((tm, nh * rope), tab)],
        out_specs=[pl.BlockSpec((tm, rp), row),
                   pl.BlockSpec((1, nh, tm, kl + rope), lambda i: (i // nb, 0, i % nb, 0)),
                   pl.BlockSpec((tm, kl + rope), row), pl.BlockSpec((tm, kl + rope), row),
                   pl.BlockSpec((tm, md), row), pl.BlockSpec((tm, gd), row)],
        out_shape=[jax.ShapeDtypeStruct((n, rp), F32),
                   jax.ShapeDtypeStruct((nbatch, nh, t, kl + rope), BF16),
                   jax.ShapeDtypeStruct((n, kl + rope), F32),
                   jax.ShapeDtypeStruct((n, kl + rope), BF16),
                   jax.ShapeDtypeStruct((n, md), BF16),
                   jax.ShapeDtypeStruct((n, gd), F32)],
        compiler_params=_cparams("parallel"),
        name="inproj",
    )(h, pre, win, qn, wqb, wuk, kvn, cos, sin)


def _segsum(x, e):
    hi = x.astype(BF16)
    lo = (x - hi.astype(F32)).astype(BF16)
    return jnp.dot(hi, e, preferred_element_type=F32) + jnp.dot(lo, e, preferred_element_type=F32)


def _rwkv_prep(p, prev, mu, w0, w2a2, a0, g2, k_k, k_a, e, rd, lora):
    ps = p + (prev - p) * mu
    r = ps[:, 0:rd]
    k = ps[:, rd:2 * rd]
    v = ps[:, 2 * rd:3 * rd]
    wa = ps[:, 3 * rd:3 * rd + 2 * lora]
    gl = ps[:, 3 * rd + 2 * lora:]
    lane = lax.broadcasted_iota(jnp.int32, wa.shape, 1)
    wa = jnp.where(lane < lora, jnp.tanh(wa), wa)
    wa2 = _mm(wa, w2a2)
    x = -(w0 + wa2[:, :rd])
    softplus = jnp.maximum(x, 0.0) + jnp.log(1.0 + jnp.exp(-jnp.abs(x)))
    logdec = -jnp.exp(-softplus - 0.5)
    a = _sigmoid(a0 + wa2[:, rd:])
    g = _mm(_sigmoid(gl), g2)
    kk = k * k_k
    kk = kk / jnp.maximum(jnp.sqrt(_segsum(kk * kk, e)), 1e-12)
    k = k * (1.0 + (a - 1.0) * k_a)
    return r, k, v, logdec, -kk, kk * a, g


def _rwkv_post(o, r, k, v, g, r_k, lng, lnb, e, hd):
    mean = _segsum(o, e) * (1.0 / hd)
    oc = o - mean
    var = _segsum(oc * oc, e) * (1.0 / hd)
    o = oc * lax.rsqrt(var + LNX_EPS) * lng + lnb
    bonus = _segsum(r * k * r_k, e) * v
    return (o + bonus) * g


def _pair_rows(y):
    lo = (lax.broadcasted_iota(jnp.int32, y.shape, 1) % LANES) < (LANES // 2)
    z = jnp.zeros_like(y)
    return jnp.concatenate([jnp.where(lo, y, z), jnp.where(lo, z, y)], axis=0)


def _rwkv_chunks_local(items, strict, incl):
    L = items[0][0].shape[0]
    cs = []
    for r, k, v, ld, cum, a, b in items:
        cum_l = cum[L - 1:L, :]
        e_neg = jnp.exp(-cum)
        e_end = jnp.exp(cum_l - cum)
        kt, bt = (k * e_neg).astype(BF16), (b * e_neg).astype(BF16)
        at, rt = a * jnp.exp(cum - ld), (r * jnp.exp(cum)).astype(BF16)
        cs.append(dict(at=at, rt=rt, vb=_pair_rows(v.astype(BF16)), v=v, dl=jnp.exp(cum_l),
                       bkh=jnp.concatenate([b * e_end, k * e_end], axis=0).astype(BF16),
                       lhs=jnp.concatenate([at.astype(BF16), rt], axis=0),
                       rhs=jnp.concatenate([_pair_rows(bt), _pair_rows(kt)], axis=0)))
    for c in cs:
        mm = _mm_nt(c.pop("lhs"), c.pop("rhs"))
        c["pw"] = jnp.where(strict, mm[:L, :2 * L], 0.0)
        c["m_ka"] = jnp.where(strict, mm[:L, 2 * L:], 0.0)
        c["m_r"] = jnp.concatenate([jnp.where(incl, mm[L:, :2 * L], 0.0), jnp.where(incl, mm[L:, 2 * L:], 0.0)],
                                   axis=1).astype(BF16)
    for c in cs:
        c["x"] = jnp.concatenate([c.pop("at"), _mm(c.pop("m_ka"), c["vb"])], axis=1)
    span = 1
    while span < L:
        for c in cs:
            c["x"] = c["x"] + _mm(c["pw"], _pair_rows(c["x"].astype(BF16)))
        span *= 2
        if span < L:
            for c in cs:
                c["pw"] = _mm(c["pw"], _pair_rows(c["pw"].astype(BF16)))
    for c in cs:
        x = c.pop("x")
        c["w1"], c["uloc"] = x[:, :LANES].astype(BF16), x[:, LANES:]
    return cs


def _rwkv_chunks_apply(cs, states, diag):
    sbs = [s.astype(BF16) for s in states]
    urs = [_mm_nt(c["w1"], sb) + c["uloc"] for c, sb in zip(cs, sbs)]
    o1 = [_mm_nt(c["rt"], sb) for c, sb in zip(cs, sbs)]
    upds = [_mm(jnp.concatenate([ur, c["v"]], axis=0).T, c["bkh"]) for c, ur in zip(cs, urs)]
    o2 = [_mm(c["m_r"], jnp.concatenate([_pair_rows(ur.astype(BF16)), c["vb"]], axis=0)) for c, ur in zip(cs, urs)]
    return [(a + b, s * c["dl"] + jnp.where(diag, u, 0.0)) for a, b, s, c, u in zip(o1, o2, states, cs, upds)]


def _rwkv_prompt_body(rd, lora, hd, L, p_ref, mu_ref, w0_ref, w2a2_ref, a0_ref, g2_ref, kk_ref, ka_ref, rk_ref,
                      lng_ref, lnb_ref, e_ref, og_ref, st_ref, prev_ref, s_ref):
    step = pl.program_id(0)
    nb, rows, _ = p_ref.shape
    npair = rd // LANES

    @pl.when(step == 0)
    def _():
        prev_ref[...] = jnp.zeros_like(prev_ref)
        s_ref[...] = jnp.zeros_like(s_ref)

    e = e_ref[...]
    ti = lax.broadcasted_iota(jnp.int32, (L, 2 * L), 0)
    si = lax.broadcasted_iota(jnp.int32, (L, 2 * L), 1) % L
    strict, incl = si < ti, si <= ti
    half = LANES // 2
    diag = ((lax.broadcasted_iota(jnp.int32, (LANES, LANES), 0) < half)
            == (lax.broadcasted_iota(jnp.int32, (LANES, LANES), 1) < half))
    tr = lax.broadcasted_iota(jnp.int32, (rows, rows), 0)
    tc = lax.broadcasted_iota(jnp.int32, (rows, rows), 1)
    tri = ((tc <= tr) & (tc // L == tr // L)).astype(BF16)
    rowi = lax.broadcasted_iota(jnp.int32, (rows, p_ref.shape[2]), 0)

    keys, items, vecs = [], [], []
    for bi in range(nb):
        p = p_ref[bi]
        prev = jnp.where(rowi == 0, prev_ref[bi], pltpu.roll(p, 1, axis=0))
        prev_ref[bi] = p[rows - 1:rows, :]
        r, k, v, ld, a, b, g = _rwkv_prep(p, prev, mu_ref[...], w0_ref[...], w2a2_ref[...], a0_ref[...], g2_ref[...],
                                          kk_ref[...], ka_ref[...], e, rd, lora)
        hi = ld.astype(BF16)
        r1 = ld - hi.astype(F32)
        mid = r1.astype(BF16)
        lo = (r1 - mid.astype(F32)).astype(BF16)
        cum = (jnp.dot(tri, hi, preferred_element_type=F32) + jnp.dot(tri, mid, preferred_element_type=F32)
               + jnp.dot(tri, lo, preferred_element_type=F32))
        vecs.append((r, k, v, g))
        for cc in range(rows // L):
            for pr in range(npair):
                sl = (slice(L * cc, L * (cc + 1)), slice(LANES * pr, LANES * (pr + 1)))
                keys.append((bi, cc, pr))
                items.append((r[sl], k[sl], v[sl], ld[sl], cum[sl], a[sl], b[sl]))
    local = dict(zip(keys, _rwkv_chunks_local(items, strict, incl)))
    chains = [(bi, pr) for bi in range(nb) for pr in range(npair)]
    state = {ch: s_ref[ch[0] * npair + ch[1]] for ch in chains}
    outs = {}
    for cc in range(rows // L):
        new = _rwkv_chunks_apply([local[bi, cc, pr] for bi, pr in chains], [state[ch] for ch in chains], diag)
        for ch, (o, s) in zip(chains, new):
            outs[ch, cc], state[ch] = o, s
    for bi in range(nb):
        for pr in range(npair):
            s_ref[bi * npair + pr] = state[bi, pr]
        o = jnp.concatenate([jnp.concatenate([outs[(bi, pr), cc] for cc in range(rows // L)], axis=0)
                             for pr in range(npair)], axis=1)
        r, k, v, g = vecs[bi]
        og_ref[bi] = _rwkv_post(o, r, k, v, g, rk_ref[...], lng_ref[...], lnb_ref[...], e, hd).astype(BF16)

    @pl.when(step == pl.num_programs(0) - 1)
    def _():
        for bi in range(nb):
            for pr in range(npair):
                s = s_ref[bi * npair + pr]
                st_ref[bi, 2 * pr] = s[:hd, :hd]
                st_ref[bi, 2 * pr + 1] = s[hd:, hd:]


def _rwkv_prompt(prw, rw, nh, hd, lora):
    b, t, pw = prw.shape
    rd = nh * hd
    rows = _tile(t, RWKV_CHUNK * RWKV_CHUNKS_PER_STEP)
    names = ("mu", "w0", "w2a2", "a0", "g2", "k_k", "k_a", "r_k", "lnx_g", "lnx_b", "e")
    return pl.pallas_call(
        functools.partial(_rwkv_prompt_body, rd, lora, hd, RWKV_CHUNK),
        grid=(t // rows,),
        in_specs=[pl.BlockSpec((b, rows, pw), lambda c: (0, c, 0))] + [_resident(rw[k].shape) for k in names],
        out_specs=[pl.BlockSpec((b, rows, rd), lambda c: (0, c, 0)),
                   pl.BlockSpec((b, nh, hd, hd), lambda c: (0, 0, 0, 0))],
        out_shape=[jax.ShapeDtypeStruct((b, t, rd), BF16), jax.ShapeDtypeStruct((b, nh, hd, hd), F32)],
        scratch_shapes=[pltpu.VMEM((b, 1, pw), F32), pltpu.VMEM((b * (rd // LANES), LANES, LANES), F32)],
        compiler_params=_cparams("arbitrary"),
        name="rwkv_prompt",
    )(prw, *[rw[k] for k in names])


def _rwkv_prep_body(rd, lora, p_ref, prev_ref, mu_ref, w0_ref, w2a2_ref, a0_ref, g2_ref, kk_ref, ka_ref, e_ref,
                    r_ref, k_ref, v_ref, g_ref, *t_refs):
    r, k, v, ld, a, b, g = _rwkv_prep(p_ref[...], prev_ref[...], mu_ref[...], w0_ref[...], w2a2_ref[...], a0_ref[...],
                                      g2_ref[...], kk_ref[...], ka_ref[...], e_ref[...], rd, lora)
    r_ref[...] = r
    k_ref[...] = k
    v_ref[...] = v
    g_ref[...] = g
    for ref, x in zip(t_refs, (r, k, v, jnp.exp(ld), a, b)):
        ref[...] = x.T


def _rwkv_step_body(s_ref, r_ref, k_ref, v_ref, w_ref, a_ref, b_ref, so_ref, o_ref):
    for h in range(s_ref.shape[0]):
        s = s_ref[h]
        sa = jnp.sum(s * a_ref[h][None], axis=1)
        s = s * w_ref[h][None] + sa[:, None, :] * b_ref[h][None] + v_ref[h][:, None, :] * k_ref[h][None]
        so_ref[h] = s
        o_ref[h] = jnp.sum(s * r_ref[h][None], axis=1)


def _rwkv_post_body(hd, o_ref, r_ref, k_ref, v_ref, g_ref, rk_ref, lng_ref, lnb_ref, e_ref, og_ref):
    og_ref[...] = _rwkv_post(o_ref[...].T, r_ref[...], k_ref[...], v_ref[...], g_ref[...], rk_ref[...], lng_ref[...],
                             lnb_ref[...], e_ref[...], hd).astype(BF16)


def _rwkv_sample(prw, shift, state_t, layer, rw, nh, hd, lora, hb):
    n, pw = prw.shape
    rd = nh * hd
    names = ("mu", "w0", "w2a2", "a0", "g2", "k_k", "k_a", "e")
    full = lambda s: pl.BlockSpec(s, lambda: (0,) * len(s))
    vecs = pl.pallas_call(
        functools.partial(_rwkv_prep_body, rd, lora),
        in_specs=[full((n, pw)), full((n, pw))] + [full(rw[k].shape) for k in names],
        out_specs=[full((n, rd))] * 4 + [full((rd, n))] * 6,
        out_shape=[jax.ShapeDtypeStruct((n, rd), F32)] * 4 + [jax.ShapeDtypeStruct((rd, n), F32)] * 6,
        name="rwkv_prep",
    )(prw, shift, *[rw[k] for k in names])
    r, k, v, g = vecs[:4]
    vspec = pl.BlockSpec((hb, hd, n), lambda i: (i, 0, 0))
    sspec = pl.BlockSpec((hb, hd, hd, n), lambda i: (i, 0, 0, 0))
    s_new, o = pl.pallas_call(
        _rwkv_step_body,
        grid=(nh // hb,),
        in_specs=[pl.BlockSpec((None, hb, hd, hd, n), lambda i: (layer, i, 0, 0, 0))] + [vspec] * 6,
        out_specs=[sspec, vspec],
        out_shape=[jax.ShapeDtypeStruct(state_t.shape[1:], F32), jax.ShapeDtypeStruct((nh, hd, n), F32)],
        compiler_params=_cparams("parallel"),
        name="rwkv_step",
    )(state_t, *[x.reshape(nh, hd, n) for x in vecs[4:]])
    pnames = ("r_k", "lnx_g", "lnx_b", "e")
    og = pl.pallas_call(
        functools.partial(_rwkv_post_body, hd),
        in_specs=[full((rd, n))] + [full((n, rd))] * 4 + [full(rw[k].shape) for k in pnames],
        out_specs=full((n, rd)),
        out_shape=jax.ShapeDtypeStruct((n, rd), BF16),
        name="rwkv_post",
    )(o.reshape(rd, n), r, k, v, g, *[rw[k] for k in pnames])
    return og, s_new


def _lanes(x, n):
    return x if n == LANES else jnp.concatenate([x] * (n // LANES), axis=1)


def _attn_prompt_body(kl, tk, rg, q_ref, k_ref, o_ref, m_ref, l_ref, acc_ref, s_ref):
    i = pl.program_id(1)
    nh, tq, dk = q_ref.shape[1:]
    rows = nh * tq
    q = q_ref[0].reshape(rows, dk)
    groups = [slice(g * rg, (g + 1) * rg) for g in range(rows // rg)]
    m_ref[...] = jnp.full_like(m_ref, -jnp.inf)
    l_ref[...] = jnp.zeros_like(l_ref)
    acc_ref[...] = jnp.zeros_like(acc_ref)

    def keys(j):
        return k_ref[0, pl.ds(pl.multiple_of(j * tk, tk), tk), :]

    def scores(r, k):
        return lax.dot_general(q[r], k, (((1,), (1,)), ((), ())), preferred_element_type=F32)

    def update(r, s, v):
        m_old = m_ref[r]
        m_new = jnp.maximum(m_old, jnp.max(s, axis=-1, keepdims=True))
        alpha = jnp.exp2(m_old - m_new)
        p = jnp.exp2(s - _lanes(m_new, tk))
        l_ref[r] = alpha * l_ref[r] + jnp.sum(p, axis=-1, keepdims=True)
        acc_ref[r] = _lanes(alpha, kl) * acc_ref[r] + jnp.dot(p.astype(BF16), v, preferred_element_type=F32)
        m_ref[r] = m_new

    k0 = keys(0)
    for r in groups:
        s_ref[r] = scores(r, k0)

    def body(j, carry):
        v = keys(j)[:, :kl]
        k_next = keys(j + 1)
        for r in groups:
            s = s_ref[r]
            s_ref[r] = scores(r, k_next)
            update(r, s, v)
        return carry

    last = (i * tq) // tk
    lax.fori_loop(0, last, body, 0)
    v = keys(last)[:, :kl]
    for g, r in enumerate(groups):
        s = s_ref[r]
        qpos = i * tq + (g * rg + lax.broadcasted_iota(jnp.int32, s.shape, 0)) % tq
        kpos = last * tk + lax.broadcasted_iota(jnp.int32, s.shape, 1)
        update(r, jnp.where(kpos <= qpos, s, -jnp.inf), v)
    o_ref[0] = (acc_ref[...] / _lanes(l_ref[...], kl)).reshape(nh, tq, kl).astype(o_ref.dtype)


def _attn_prompt(q, kbf, kl):
    b, nh, t, dk = q.shape
    tq, tk = _tile(t, ATT_TQ), _tile(t, ATT_TK)
    rows = nh * tq
    rg = _tile(rows, ATT_ROW_GROUP)
    assert tk % tq == 0 and rg % tq == 0
    return pl.pallas_call(
        functools.partial(_attn_prompt_body, kl, tk, rg),
        grid=(b, t // tq),
        in_specs=[pl.BlockSpec((1, nh, tq, dk), lambda bi, i: (bi, 0, i, 0)),
                  pl.BlockSpec((1, t, dk), lambda bi, i: (bi, 0, 0))],
        out_specs=pl.BlockSpec((1, nh, tq, kl), lambda bi, i: (bi, 0, i, 0)),
        out_shape=jax.ShapeDtypeStruct((b, nh, t, kl), BF16),
        scratch_shapes=[pltpu.VMEM((rows, LANES), F32), pltpu.VMEM((rows, LANES), F32), pltpu.VMEM((rows, kl), F32),
                        pltpu.VMEM((rows, tk), F32)],
        compiler_params=_cparams("parallel", "arbitrary"),
        name="attn_prompt",
    )(q, kbf)


def _attn_sample_body(kl, npg, ngrp, layer, pt_ref, q_ref, kself_ref, cache_ref, o_ref, buf_ref, sem_ref):
    b = pl.program_id(0)
    nreq = pl.num_programs(0)

    def copies(req, grp, slot):
        return [pltpu.make_async_copy(cache_ref.at[layer, pt_ref[req, grp * npg + n]], buf_ref.at[slot, n],
                                      sem_ref.at[slot]) for n in range(npg)]

    def start(req, grp, slot):
        for n, c in enumerate(copies(req, grp, slot)):
            c.start(priority=n % 2)

    def slot_of(grp):
        return grp % 2 if ngrp % 2 == 0 else (b * ngrp + grp) % 2

    @pl.when(b == 0)
    def _():
        start(0, 0, 0)

    q = q_ref[0]
    ks = kself_ref[0]
    m = jnp.sum(q.astype(F32) * ks.astype(F32), axis=-1, keepdims=True)
    l = jnp.ones_like(m)
    acc = jnp.broadcast_to(ks[:, :kl].astype(F32), (q.shape[0], kl))
    for g in range(ngrp):
        slot = slot_of(g)
        if g + 1 < ngrp:
            start(b, g + 1, 1 - slot)
        else:
            @pl.when(b + 1 < nreq)
            def _():
                start(b + 1, 0, 1 - slot)
        for c in copies(b, g, slot):
            c.wait()
        pair = 2 if npg % 2 == 0 else 1
        kts = [jnp.concatenate([buf_ref[slot, n + e] for e in range(pair)], axis=1).astype(BF16)
               for n in range(0, npg, pair)]
        s = jnp.concatenate([jnp.dot(q, kt, preferred_element_type=F32) for kt in kts], axis=1)
        m_new = jnp.maximum(m, jnp.max(s, axis=-1, keepdims=True))
        alpha = jnp.exp2(m - m_new)
        p = jnp.exp2(s - m_new)
        l = alpha * l + jnp.sum(p, axis=-1, keepdims=True)
        pb = p.astype(BF16)
        w = kts[0].shape[1]
        pv = _mm_nt(pb[:, :w], kts[0][:kl, :])
        for n in range(1, len(kts)):
            pv = pv + _mm_nt(pb[:, n * w:(n + 1) * w], kts[n][:kl, :])
        acc = alpha * acc + pv
        m = m_new
    o_ref[0] = (acc / l).astype(o_ref.dtype)


def _attn_sample(q, kself, cache_t, layer, page_table, kl):
    n, nh, dk = q.shape
    ps = cache_t.shape[3]
    npages = page_table.shape[1]
    npg = _tile(npages, PAGES_PER_GROUP)
    grid_spec = pltpu.PrefetchScalarGridSpec(
        num_scalar_prefetch=1,
        grid=(n,),
        in_specs=[pl.BlockSpec((1, nh, dk), lambda bi, pt: (bi, 0, 0)),
                  pl.BlockSpec((1, 1, dk), lambda bi, pt: (bi, 0, 0)),
                  pl.BlockSpec(memory_space=pl.ANY)],
        out_specs=pl.BlockSpec((1, nh, kl), lambda bi, pt: (bi, 0, 0)),
        scratch_shapes=[pltpu.VMEM((2, npg, dk, ps), F32), pltpu.SemaphoreType.DMA((2,))],
    )
    return pl.pallas_call(
        functools.partial(_attn_sample_body, kl, npg, npages // npg, layer),
        grid_spec=grid_spec,
        out_shape=jax.ShapeDtypeStruct((n, nh, kl), BF16),
        compiler_params=_cparams("arbitrary"),
        name="attn_sample",
    )(page_table, q, kself, cache_t)


def _memkv_body(m_ref, g_ref, w_ref, o_ref):
    o_ref[...] = jnp.dot(_rms(m_ref[...], g_ref[...]).astype(BF16), w_ref[...], preferred_element_type=F32)


def _memkv(mem, g, wkv):
    n, d = mem.shape
    full = lambda s: pl.BlockSpec(s, lambda: (0,) * len(s))
    return pl.pallas_call(
        _memkv_body,
        in_specs=[full((n, d)), full((1, d)), full(wkv.shape)],
        out_specs=full((n, wkv.shape[1])),
        out_shape=jax.ShapeDtypeStruct((n, wkv.shape[1]), F32),
        name="mem_kv",
    )(mem, g, wkv)


def _memattn_prompt_body(nh, scale, q_ref, k_ref, v_ref, o_ref):
    hd = q_ref.shape[1] // nh
    q, k, v = q_ref[...], k_ref[0], v_ref[0]
    heads = [slice(h * hd, (h + 1) * hd) for h in range(nh)]
    ss = [_mm_nt(q[:, sl], k[:, sl]) * scale for sl in heads]
    ps = [jnp.exp(s - jnp.max(s, axis=-1, keepdims=True)) for s in ss]
    ps = [p / jnp.sum(p, axis=-1, keepdims=True) for p in ps]
    o_ref[...] = jnp.concatenate([_mm(p, v[:, sl]) for p, sl in zip(ps, heads)], axis=1).astype(o_ref.dtype)


def _memattn_prompt(q, mk, mv, nh, scale, tm):
    n, md = q.shape
    b, m, _ = mk.shape
    nb = (n // b) // tm
    kv = pl.BlockSpec((1, m, md), lambda i: (i // nb, 0, 0))
    return pl.pallas_call(
        functools.partial(_memattn_prompt_body, nh, scale),
        grid=(n // tm,),
        in_specs=[pl.BlockSpec((tm, md), lambda i: (i, 0)), kv, kv],
        out_specs=pl.BlockSpec((tm, md), lambda i: (i, 0)),
        out_shape=jax.ShapeDtypeStruct((n, md), BF16),
        compiler_params=_cparams("parallel"),
        name="memattn_prompt",
    )(q, mk, mv)


def _memattn_sample_body(scale, q_ref, k_ref, v_ref, o_ref):
    gr, rows, _ = k_ref.shape
    nh = q_ref.shape[1]
    own = (lax.broadcasted_iota(jnp.int32, (nh, rows), 1) % nh) == lax.broadcasted_iota(jnp.int32, (nh, rows), 0)
    ss = [jnp.where(own, _mm_nt(q_ref[g], k_ref[g]) * scale, -jnp.inf) for g in range(gr)]
    ps = [jnp.exp(s - jnp.max(s, axis=-1, keepdims=True)) for s in ss]
    ps = [p / jnp.sum(p, axis=-1, keepdims=True) for p in ps]
    for g in range(gr):
        o_ref[g] = _mm(ps[g], v_ref[g]).astype(o_ref.dtype)


def _memattn_sample(q, cache_k, cache_v, layer, scale, gr):
    n, md = q.shape
    depth, _, m, nh, hd = cache_k.shape
    kv = pl.BlockSpec((None, gr, m * nh, hd), lambda i: (layer, i, 0, 0))
    qs = pl.BlockSpec((gr, nh, hd), lambda i: (i, 0, 0))
    return pl.pallas_call(
        functools.partial(_memattn_sample_body, scale),
        grid=(n // gr,),
        in_specs=[qs, kv, kv],
        out_specs=qs,
        out_shape=jax.ShapeDtypeStruct((n, nh, hd), BF16),
        compiler_params=_cparams("parallel"),
        name="memattn_sample",
    )(q.reshape(n, nh, hd), cache_k.reshape(depth, n, m * nh, hd), cache_v.reshape(depth, n, m * nh, hd)).reshape(n, md)


def _merge_body(nh, h_ref, og_ref, ctx_ref, om_ref, gates_ref, wo_ref, wuv_ref, mwo_ref, memwo_ref, wout_ref, post_ref,
                o_ref):
    d = h_ref.shape[1]
    o_rwkv = jnp.dot(og_ref[...], wo_ref[...], preferred_element_type=F32)
    vs = []
    for pr in range(nh // 2):
        vp = (jnp.dot(ctx_ref[0, 2 * pr], wuv_ref[2 * pr], preferred_element_type=F32)
              + jnp.dot(ctx_ref[0, 2 * pr + 1], wuv_ref[2 * pr + 1], preferred_element_type=F32))
        vs.append(vp.astype(BF16))
    o_mla = jnp.dot(jnp.concatenate(vs, axis=1), mwo_ref[...], preferred_element_type=F32)
    o_mem = jnp.dot(om_ref[...], memwo_ref[...], preferred_element_type=F32)
    merged = gates_ref[:, 0:d] * o_rwkv + gates_ref[:, d:2 * d] * o_mla + gates_ref[:, 2 * d:3 * d] * o_mem
    y = jnp.dot(merged.astype(BF16), wout_ref[...], preferred_element_type=F32)
    o_ref[...] = h_ref[...] + _rms(y, post_ref[...])


def _merge(h, nbatch, og, ctx, om, gates, wo, wuv, mwo, memwo, wout, post, nh, tm):
    n, d = h.shape
    nb = (n // nbatch) // tm
    row = lambda i: (i, 0)
    kl = ctx.shape[-1]
    return pl.pallas_call(
        functools.partial(_merge_body, nh),
        grid=(n // tm,),
        in_specs=[pl.BlockSpec((tm, d), row), pl.BlockSpec((tm, og.shape[1]), row),
                  pl.BlockSpec((1, nh, tm, kl), lambda i: (i // nb, 0, i % nb, 0)),
                  pl.BlockSpec((tm, om.shape[1]), row), pl.BlockSpec((tm, 3 * d), row),
                  _resident(wo.shape), _resident(wuv.shape), _resident(mwo.shape), _resident(memwo.shape),
                  _resident(wout.shape), _resident((1, d))],
        out_specs=pl.BlockSpec((tm, d), row),
        out_shape=jax.ShapeDtypeStruct((n, d), F32),
        compiler_params=_cparams("parallel"),
        name="merge",
    )(h, og, ctx, om, gates, wo, wuv, mwo, memwo, wout, post)


def _rope_tables(pos, rope, nh):
    half = rope // 2
    freqs = ROPE_BASE ** (-jnp.arange(half, dtype=F32) / half)
    ang = pos.astype(F32)[:, None] * freqs
    cos, sin = jnp.cos(ang), jnp.sin(ang)
    return jnp.tile(jnp.concatenate([cos, cos], axis=1), (1, nh)), jnp.tile(jnp.concatenate([-sin, sin], axis=1), (1, nh))


def _prep_weights(W, d):
    nh, hd = W["rwkv_r_k"].shape
    rd = nh * hd
    lora = W["rwkv_w2"].shape[0]
    glora = W["rwkv_g2"].shape[0]
    rp = 3 * rd + 2 * lora + glora
    ql = W["mla_q_norm"].shape[0]
    kl, mh, vh = W["mla_w_uv"].shape
    nope = W["mla_w_uk"].shape[2]
    rope = W["mla_w_qb"].shape[1] // mh - nope
    md = W["mem_w_k"].shape[1]
    half = rope // 2
    row = lambda x: x.reshape(1, -1)
    w_in = W["w_in"]
    o_cq, o_kv, o_pe, o_mem, o_g = rp, rp + ql, rp + ql + kl, rp + ql + kl + rope, rp + ql + kl + rope + md
    cols = [w_in[:, :o_pe], w_in[:, o_mem:], w_in[:, o_pe:o_mem],
            w_in[:, o_pe + half:o_mem], w_in[:, o_pe:o_pe + half]]
    width = sum(c.shape[1] for c in cols)
    pad = (-width) % LANES
    win = jnp.concatenate(cols + [jnp.zeros((d, pad), F32)], axis=1).astype(BF16)
    wqb = W["mla_w_qb"].reshape(ql, mh, nope + rope)
    wqb = jnp.concatenate([wqb[:, :, :nope].reshape(ql, mh * nope),
                           wqb[:, :, nope:].reshape(ql, mh * rope),
                           jnp.concatenate([wqb[:, :, nope + half:], wqb[:, :, nope:nope + half]], axis=2).reshape(ql, mh * rope)],
                          axis=1).astype(BF16)
    ukt = jnp.transpose(W["mla_w_uk"], (1, 2, 0))
    z = jnp.zeros_like(ukt[0])
    wuk = jnp.stack([jnp.concatenate([jnp.concatenate([ukt[2 * p], z], axis=1),
                                      jnp.concatenate([z, ukt[2 * p + 1]], axis=1)], axis=0)
                     for p in range(mh // 2)]).astype(BF16)
    uv = jnp.transpose(W["mla_w_uv"], (1, 0, 2))
    zv = jnp.zeros_like(uv[0])
    wuv = jnp.stack([jnp.concatenate([uv[h], zv] if h % 2 == 0 else [zv, uv[h]], axis=1)
                     for h in range(mh)]).astype(BF16)
    zl = jnp.zeros((lora, rd), F32)
    w2a2 = jnp.concatenate([jnp.concatenate([W["rwkv_w2"], zl], axis=1),
                            jnp.concatenate([zl, W["rwkv_a2"]], axis=1)], axis=0).astype(BF16)
    hid = jnp.arange(rd) // hd
    rw = dict(mu=row(W["rwkv_mu"]), w0=row(W["rwkv_w0"]), w2a2=w2a2, a0=row(W["rwkv_a0"]), g2=W["rwkv_g2"].astype(BF16),
              k_k=row(W["rwkv_k_k"]), k_a=row(W["rwkv_k_a"]), r_k=row(W["rwkv_r_k"]), lnx_g=row(W["rwkv_lnx_g"]),
              lnx_b=row(W["rwkv_lnx_b"]), e=(hid[:, None] == hid[None, :]).astype(BF16))
    dims = (rp, ql, kl, rope, md, 3 * d, mh, nope)
    return dict(
        dims=dims, nh=nh, hd=hd, lora=lora, rw=rw, win=win, wqb=wqb, wuk=wuk, wuv=wuv,
        ffn1=(row(W["ffn1_pre"]), row(W["ffn1_post"]), W["ffn1_gate"].astype(BF16), W["ffn1_up"].astype(BF16),
              W["ffn1_down"].astype(BF16)),
        ffn2=(row(W["ffn2_pre"]), row(W["ffn2_post"]), W["ffn2_gate"].astype(BF16), W["ffn2_up"].astype(BF16),
              W["ffn2_down"].astype(BF16)),
        mix_pre=row(W["mix_pre"]), mix_post=row(W["mix_post"]), q_norm=row(W["mla_q_norm"]), kv_norm=row(W["mla_kv_norm"]),
        mem_norm=row(W["mem_norm"]), mem_wkv=jnp.concatenate([W["mem_w_k"], W["mem_w_v"]], axis=1).astype(BF16),
        rwkv_wo=W["rwkv_w_o"].astype(BF16), mla_wo=W["mla_w_o"].astype(BF16), mem_wo=W["mem_w_o"].astype(BF16),
        w_out=W["w_out"].astype(BF16), qscale=float(nope + rope) ** -0.5 * LOG2E,
    )


def _tile(n, pref):
    t = min(pref, n)
    assert n % t == 0, (n, t)
    return t


def kernel(x_prompt, x_sample, cache_mla, state_rwkv, state_shift, cache_mem_k, cache_mem_v, page_table, mem_prompt, ffn1_pre, ffn1_post, ffn1_gate, ffn1_up, ffn1_down, mix_pre, mix_post, w_in, rwkv_mu, rwkv_w0, rwkv_w2, rwkv_a0, rwkv_a2, rwkv_g2, rwkv_k_k, rwkv_k_a, rwkv_r_k, rwkv_lnx_g, rwkv_lnx_b, rwkv_w_o, mla_q_norm, mla_w_qb, mla_kv_norm, mla_w_uk, mla_w_uv, mla_w_o, mem_norm, mem_w_k, mem_w_v, mem_w_o, w_out, ffn2_pre, ffn2_post, ffn2_gate, ffn2_up, ffn2_down):
    names = ("ffn1_pre", "ffn1_post", "ffn1_gate", "ffn1_up", "ffn1_down", "mix_pre", "mix_post", "w_in",
             "rwkv_mu", "rwkv_w0", "rwkv_w2", "rwkv_a0", "rwkv_a2", "rwkv_g2", "rwkv_k_k", "rwkv_k_a", "rwkv_r_k",
             "rwkv_lnx_g", "rwkv_lnx_b", "rwkv_w_o", "mla_q_norm", "mla_w_qb", "mla_kv_norm", "mla_w_uk", "mla_w_uv",
             "mla_w_o", "mem_norm", "mem_w_k", "mem_w_v", "mem_w_o", "w_out", "ffn2_pre", "ffn2_post", "ffn2_gate",
             "ffn2_up", "ffn2_down")
    stacked = (ffn1_pre, ffn1_post, ffn1_gate, ffn1_up, ffn1_down, mix_pre, mix_post, w_in,
               rwkv_mu, rwkv_w0, rwkv_w2, rwkv_a0, rwkv_a2, rwkv_g2, rwkv_k_k, rwkv_k_a, rwkv_r_k,
               rwkv_lnx_g, rwkv_lnx_b, rwkv_w_o, mla_q_norm, mla_w_qb, mla_kv_norm, mla_w_uk, mla_w_uv,
               mla_w_o, mem_norm, mem_w_k, mem_w_v, mem_w_o, w_out, ffn2_pre, ffn2_post, ffn2_gate,
               ffn2_up, ffn2_down)
    B, S, D = x_prompt.shape
    DB, T, _ = x_sample.shape
    assert T == 1, "decode groups carry one new token per request"
    depth = ffn1_pre.shape[0]
    page = cache_mla.shape[2]
    past_len = page_table.shape[1] * page
    mem_tokens, mem_heads, mem_hd = cache_mem_k.shape[2:]
    mem_scale = float(mem_hd) ** -0.5

    cache_t = jnp.swapaxes(cache_mla, 2, 3)
    state_t = jnp.transpose(state_rwkv, (0, 2, 3, 4, 1))
    xp = x_prompt.reshape(B * S, D)
    xs = x_sample.reshape(DB, D)
    outs = [[] for _ in range(8)]
    for l in range(depth):
        P = _prep_weights({n: w[l] for n, w in zip(names, stacked)}, D)
        rp, ql, kl, rope, md, gd, mh, nope = P["dims"]
        nh, hd, lora, rw = P["nh"], P["hd"], P["lora"], P["rw"]
        cos_p, sin_p = _rope_tables(jnp.arange(S), rope, mh)
        cos_s, sin_s = _rope_tables(jnp.full((DB,), past_len), rope, mh)
        tm_p, tm_w, tm_s = _tile(S, ROWS_PER_STEP), _tile(S, ROWS_PER_STEP_WIDE), DB

        h = _ffn(xs, *P["ffn1"], tm_s)
        prw_s, q, rows_s, kbf, qmem, gates = _inproj(h, 1, P["mix_pre"], P["win"], P["q_norm"], P["wqb"], P["wuk"],
                                                     P["kv_norm"], cos_s, sin_s, P["dims"], P["qscale"], tm_s)
        og, wkv_t = _rwkv_sample(prw_s, state_shift[l], state_t, l, rw, nh, hd, lora, _tile(nh, 2))
        wkv_s = jnp.transpose(wkv_t, (3, 0, 1, 2))
        ctx = _attn_sample(jnp.swapaxes(q[0], 0, 1), kbf.reshape(DB, 1, kl + rope), cache_t, l, page_table, kl)
        om = _memattn_sample(qmem, cache_mem_k, cache_mem_v, l, mem_scale, _tile(DB, 4))
        h = _merge(h, 1, og, jnp.swapaxes(ctx, 0, 1)[None], om, gates, P["rwkv_wo"], P["wuv"], P["mla_wo"], P["mem_wo"],
                   P["w_out"], P["mix_post"], mh, tm_s)
        xs = _ffn(h, *P["ffn2"], tm_s)

        mkv = _memkv(mem_prompt.reshape(B * mem_tokens, D), P["mem_norm"], P["mem_wkv"])
        mk_p, mv_p = mkv[:, :md].reshape(B, mem_tokens, md), mkv[:, md:].reshape(B, mem_tokens, md)
        h = _ffn(xp, *P["ffn1"], tm_w)
        prw, q, rows, kbf, qmem, gates = _inproj(h, B, P["mix_pre"], P["win"], P["q_norm"], P["wqb"], P["wuk"],
                                                 P["kv_norm"], cos_p, sin_p, P["dims"], P["qscale"], tm_p)
        og, wkv_p = _rwkv_prompt(prw.reshape(B, S, rp), rw, nh, hd, lora)
        ctx = _attn_prompt(q, kbf.reshape(B, S, kl + rope), kl)
        om = _memattn_prompt(qmem, mk_p, mv_p, mem_heads, mem_scale, tm_w)
        h = _merge(h, B, og.reshape(B * S, nh * hd), ctx, om, gates, P["rwkv_wo"], P["wuv"], P["mla_wo"], P["mem_wo"],
                   P["w_out"], P["mix_post"], mh, tm_p)
        xp = _ffn(h, *P["ffn2"], tm_w)
        rows_p, shift_p = rows.reshape(B, S, kl + rope), prw.reshape(B, S, rp)[:, -1]

        for lst, val in zip(outs, (rows_p, rows_s.reshape(DB, T, kl + rope), wkv_p, wkv_s, shift_p, prw_s,
                                   mk_p.reshape(B, mem_tokens, mem_heads, mem_hd),
                                   mv_p.reshape(B, mem_tokens, mem_heads, mem_hd))):
            lst.append(val)
    return (xp.reshape(B, S, D), xs.reshape(DB, T, D)) + tuple(jnp.stack(o) for o in outs)
```

```python
import functools

import jax
import jax.numpy as jnp
from jax import lax
from jax.experimental import pallas as pl
from jax.experimental.pallas import tpu as pltpu

F32, BF16 = jnp.float32, jnp.bfloat16
RMS_EPS = 1e-6
LNX_EPS = 64e-5
ROPE_BASE = 10000.0
LANES = 128
VMEM_LIMIT = 52 * 1024 * 1024
ROWS_PER_STEP = 256
ROWS_PER_STEP_WIDE = 512
RWKV_CHUNK = 64
RWKV_CHUNKS_PER_STEP = 2
ATT_TQ = 256
ATT_TK = 512
ATT_ROW_GROUP = 512
PAGES_PER_GROUP = 32
LOG2E = 1.4426950408889634


def _cparams(*sem):
    return pltpu.CompilerParams(dimension_semantics=sem, vmem_limit_bytes=VMEM_LIMIT)


def _resident(shape):
    nd = len(shape)
    return pl.BlockSpec(shape, lambda *_: (0,) * nd, pipeline_mode=pl.Buffered(1))


def _rms(x, g):
    return x * lax.rsqrt(jnp.mean(x * x, axis=-1, keepdims=True) + RMS_EPS) * g


def _sigmoid(x):
    return 1.0 / (1.0 + jnp.exp(-x))


def _mm(a, b):
    return jnp.dot(a.astype(BF16), b.astype(BF16), preferred_element_type=F32)


def _mm_nt(a, b):
    return lax.dot_general(a.astype(BF16), b.astype(BF16), (((1,), (1,)), ((), ())), preferred_element_type=F32)


def _ffn_body(x_ref, pre_ref, post_ref, wg_ref, wu_ref, wd_ref, o_ref):
    x = x_ref[...]
    h = _rms(x, pre_ref[...]).astype(BF16)
    g = jnp.dot(h, wg_ref[...], preferred_element_type=F32)
    u = jnp.dot(h, wu_ref[...], preferred_element_type=F32)
    act = (g * _sigmoid(g)) * u
    y = jnp.dot(act.astype(BF16), wd_ref[...], preferred_element_type=F32)
    o_ref[...] = x + 0.5 * _rms(y, post_ref[...])


def _ffn(x, pre, post, wg, wu, wd, tm):
    n, d = x.shape
    f = wg.shape[1]
    return pl.pallas_call(
        _ffn_body,
        grid=(n // tm,),
        in_specs=[pl.BlockSpec((tm, d), lambda i: (i, 0)), _resident((1, d)), _resident((1, d)),
                  _resident((d, f)), _resident((d, f)), _resident((f, d))],
        out_specs=pl.BlockSpec((tm, d), lambda i: (i, 0)),
        out_shape=jax.ShapeDtypeStruct((n, d), F32),
        compiler_params=_cparams("parallel"),
        name="ffn",
    )(x, pre, post, wg, wu, wd)


def _inproj_body(dims, qscale, h_ref, pre_ref, win_ref, qn_ref, wqb_ref, wuk_ref, kvn_ref, cos_ref, sin_ref,
                 prw_ref, q_ref, rows_ref, kbf_ref, qmem_ref, gates_ref):
    rp, ql, kl, rope, md, gd, nh, nope = dims
    u = _rms(h_ref[...], pre_ref[...]).astype(BF16)
    p = jnp.dot(u, win_ref[...], preferred_element_type=F32)
    o = 0
    prw_ref[...] = p[:, o:o + rp]; o += rp
    cq = p[:, o:o + ql]; o += ql
    ckv = p[:, o:o + kl]; o += kl
    qmem_ref[...] = p[:, o:o + md].astype(BF16); o += md
    gates_ref[...] = _sigmoid(p[:, o:o + gd]); o += gd
    kpe = p[:, o:o + rope]; o += rope
    kpe_sw = p[:, o:o + rope]
    cos = cos_ref[...]
    sin = sin_ref[...]
    q = jnp.dot(_rms(cq, qn_ref[...]).astype(BF16), wqb_ref[...], preferred_element_type=F32)
    nn = nh * nope
    nr = nh * rope
    qpe = ((q[:, nn:nn + nr] * cos + q[:, nn + nr:nn + 2 * nr] * sin) * qscale).astype(BF16)
    qn = q[:, :nn].astype(BF16)
    for pr in range(nh // 2):
        qlat = (jnp.dot(qn[:, LANES * pr:LANES * (pr + 1)], wuk_ref[pr], preferred_element_type=F32) * qscale).astype(BF16)
        for e in range(2):
            hh = 2 * pr + e
            q_ref[0, hh, :, 0:kl] = qlat[:, kl * e:kl * (e + 1)]
            q_ref[0, hh, :, kl:kl + rope] = qpe[:, rope * hh:rope * (hh + 1)]
    ckvn = _rms(ckv, kvn_ref[...])
    kper = kpe * cos[:, :rope] + kpe_sw * sin[:, :rope]
    rows_ref[:, 0:kl] = ckvn
    rows_ref[:, kl:kl + rope] = kper
    kbf_ref[:, 0:kl] = ckvn.astype(BF16)
    kbf_ref[:, kl:kl + rope] = kper.astype(BF16)


def _inproj(h, nbatch, pre, win, qn, wqb, wuk, kvn, cos, sin, dims, qscale, tm):
    n, d = h.shape
    rp, ql, kl, rope, md, gd, nh, nope = dims
    t = n // nbatch
    nb = t // tm
    cw = win.shape[1]
    row = lambda i: (i, 0)
    tab = lambda i: (i % nb, 0)
    return pl.pallas_call(
        functools.partial(_inproj_body, dims, qscale),
        grid=(n // tm,),
        in_specs=[pl.BlockSpec((tm, d), row), _resident((1, d)), _resident((d, cw)), _resident((1, ql)),
                  _resident(wqb.shape), _resident(wuk.shape), _resident((1, kl)),
                  pl.BlockSpec((tm, nh * rope), tab), pl.BlockSpec((tm, nh * rope), tab)],
        out_specs=[pl.BlockSpec((tm, rp), row),
                   pl.BlockSpec((1, nh, tm, kl + rope), lambda i: (i // nb, 0, i % nb, 0)),
                   pl.BlockSpec((tm, kl + rope), row), pl.BlockSpec((tm, kl + rope), row),
                   pl.BlockSpec((tm, md), row), pl.BlockSpec((tm, gd), row)],
        out_shape=[jax.ShapeDtypeStruct((n, rp), F32),
                   jax.ShapeDtypeStruct((nbatch, nh, t, kl + rope), BF16),
                   jax.ShapeDtypeStruct((n, kl + rope), F32),
                   jax.ShapeDtypeStruct((n, kl + rope), BF16),
                   jax.ShapeDtypeStruct((n, md), BF16),
                   jax.ShapeDtypeStruct((n, gd), F32)],
        compiler_params=_cparams("parallel"),
        name="inproj",
    )(h, pre, win, qn, wqb, wuk, kvn, cos, sin)


def _segsum(x, e):
    hi = x.astype(BF16)
    lo = (x - hi.astype(F32)).astype(BF16)
    return jnp.dot(hi, e, preferred_element_type=F32) + jnp.dot(lo, e, preferred_element_type=F32)


def _rwkv_prep(p, prev, mu, w0, w2a2, a0, g2, k_k, k_a, e, rd, lora):
    ps = p + (prev - p) * mu
    r = ps[:, 0:rd]
    k = ps[:, rd:2 * rd]
    v = ps[:, 2 * rd:3 * rd]
    wa = ps[:, 3 * rd:3 * rd + 2 * lora]
    gl = ps[:, 3 * rd + 2 * lora:]
    lane = lax.broadcasted_iota(jnp.int32, wa.shape, 1)
    wa = jnp.where(lane < lora, jnp.tanh(wa), wa)
    wa2 = _mm(wa, w2a2)
    x = -(w0 + wa2[:, :rd])
    softplus = jnp.maximum(x, 0.0) + jnp.log(1.0 + jnp.exp(-jnp.abs(x)))
    logdec = -jnp.exp(-softplus - 0.5)
    a = _sigmoid(a0 + wa2[:, rd:])
    g = _mm(_sigmoid(gl), g2)
    kk = k * k_k
    kk = kk / jnp.maximum(jnp.sqrt(_segsum(kk * kk, e)), 1e-12)
    k = k * (1.0 + (a - 1.0) * k_a)
    return r, k, v, logdec, -kk, kk * a, g


def _rwkv_post(o, r, k, v, g, r_k, lng, lnb, e, hd):
    mean = _segsum(o, e) * (1.0 / hd)
    oc = o - mean
    var = _segsum(oc * oc, e) * (1.0 / hd)
    o = oc * lax.rsqrt(var + LNX_EPS) * lng + lnb
    bonus = _segsum(r * k * r_k, e) * v
    return (o + bonus) * g


def _pair_rows(y):
    lo = (lax.broadcasted_iota(jnp.int32, y.shape, 1) % LANES) < (LANES // 2)
    z = jnp.zeros_like(y)
    return jnp.concatenate([jnp.where(lo, y, z), jnp.where(lo, z, y)], axis=0)


def _rwkv_chunks_local(items, strict, incl):
    L = items[0][0].shape[0]
    cs = []
    for r, k, v, ld, cum, a, b in items:
        cum_l = cum[L - 1:L, :]
        e_neg = jnp.exp(-cum)
        e_end = jnp.exp(cum_l - cum)
        kt, bt = (k * e_neg).astype(BF16), (b * e_neg).astype(BF16)
        at, rt = a * jnp.exp(cum - ld), (r * jnp.exp(cum)).astype(BF16)
        cs.append(dict(at=at, rt=rt, vb=_pair_rows(v.astype(BF16)), v=v, dl=jnp.exp(cum_l),
                       bkh=jnp.concatenate([b * e_end, k * e_end], axis=0).astype(BF16),
                       lhs=jnp.concatenate([at.astype(BF16), rt], axis=0),
                       rhs=jnp.concatenate([_pair_rows(bt), _pair_rows(kt)], axis=0)))
    for c in cs:
        mm = _mm_nt(c.pop("lhs"), c.pop("rhs"))
        c["pw"] = jnp.where(strict, mm[:L, :2 * L], 0.0)
        c["m_ka"] = jnp.where(strict, mm[:L, 2 * L:], 0.0)
        c["m_r"] = jnp.concatenate([jnp.where(incl, mm[L:, :2 * L], 0.0), jnp.where(incl, mm[L:, 2 * L:], 0.0)],
                                   axis=1).astype(BF16)
    for c in cs:
        c["x"] = jnp.concatenate([c.pop("at"), _mm(c.pop("m_ka"), c["vb"])], axis=1)
    span = 1
    while span < L:
        for c in cs:
            c["x"] = c["x"] + _mm(c["pw"], _pair_rows(c["x"].astype(BF16)))
        span *= 2
        if span < L:
            for c in cs:
                c["pw"] = _mm(c["pw"], _pair_rows(c["pw"].astype(BF16)))
    for c in cs:
        x = c.pop("x")
        c["w1"], c["uloc"] = x[:, :LANES].astype(BF16), x[:, LANES:]
    return cs


def _rwkv_chunks_apply(cs, states, diag):
    sbs = [s.astype(BF16) for s in states]
    urs = [_mm_nt(c["w1"], sb) + c["uloc"] for c, sb in zip(cs, sbs)]
    o1 = [_mm_nt(c["rt"], sb) for c, sb in zip(cs, sbs)]
    upds = [_mm(jnp.concatenate([ur, c["v"]], axis=0).T, c["bkh"]) for c, ur in zip(cs, urs)]
    o2 = [_mm(c["m_r"], jnp.concatenate([_pair_rows(ur.astype(BF16)), c["vb"]], axis=0)) for c, ur in zip(cs, urs)]
    return [(a + b, s * c["dl"] + jnp.where(diag, u, 0.0)) for a, b, s, c, u in zip(o1, o2, states, cs, upds)]


def _rwkv_prompt_body(rd, lora, hd, L, p_ref, mu_ref, w0_ref, w2a2_ref, a0_ref, g2_ref, kk_ref, ka_ref, rk_ref,
                      lng_ref, lnb_ref, e_ref, og_ref, st_ref, prev_ref, s_ref):
    step = pl.program_id(0)
    nb, rows, _ = p_ref.shape
    npair = rd // LANES

    @pl.when(step == 0)
    def _():
        prev_ref[...] = jnp.zeros_like(prev_ref)
        s_ref[...] = jnp.zeros_like(s_ref)

    e = e_ref[...]
    ti = lax.broadcasted_iota(jnp.int32, (L, 2 * L), 0)
    si = lax.broadcasted_iota(jnp.int32, (L, 2 * L), 1) % L
    strict, incl = si < ti, si <= ti
    half = LANES // 2
    diag = ((lax.broadcasted_iota(jnp.int32, (LANES, LANES), 0) < half)
            == (lax.broadcasted_iota(jnp.int32, (LANES, LANES), 1) < half))
    tr = lax.broadcasted_iota(jnp.int32, (rows, rows), 0)
    tc = lax.broadcasted_iota(jnp.int32, (rows, rows), 1)
    tri = ((tc <= tr) & (tc // L == tr // L)).astype(BF16)
    rowi = lax.broadcasted_iota(jnp.int32, (rows, p_ref.shape[2]), 0)

    keys, items, vecs = [], [], []
    for bi in range(nb):
        p = p_ref[bi]
        prev = jnp.where(rowi == 0, prev_ref[bi], pltpu.roll(p, 1, axis=0))
        prev_ref[bi] = p[rows - 1:rows, :]
        r, k, v, ld, a, b, g = _rwkv_prep(p, prev, mu_ref[...], w0_ref[...], w2a2_ref[...], a0_ref[...], g2_ref[...],
                                          kk_ref[...], ka_ref[...], e, rd, lora)
        hi = ld.astype(BF16)
        r1 = ld - hi.astype(F32)
        mid = r1.astype(BF16)
        lo = (r1 - mid.astype(F32)).astype(BF16)
        cum = (jnp.dot(tri, hi, preferred_element_type=F32) + jnp.dot(tri, mid, preferred_element_type=F32)
               + jnp.dot(tri, lo, preferred_element_type=F32))
        vecs.append((r, k, v, g))
        for cc in range(rows // L):
            for pr in range(npair):
                sl = (slice(L * cc, L * (cc + 1)), slice(LANES * pr, LANES * (pr + 1)))
                keys.append((bi, cc, pr))
                items.append((r[sl], k[sl], v[sl], ld[sl], cum[sl], a[sl], b[sl]))
    local = dict(zip(keys, _rwkv_chunks_local(items, strict, incl)))
    chains = [(bi, pr) for bi in range(nb) for pr in range(npair)]
    state = {ch: s_ref[ch[0] * npair + ch[1]] for ch in chains}
    outs = {}
    for cc in range(rows // L):
        new = _rwkv_chunks_apply([local[bi, cc, pr] for bi, pr in chains], [state[ch] for ch in chains], diag)
        for ch, (o, s) in zip(chains, new):
            outs[ch, cc], state[ch] = o, s
    for bi in range(nb):
        for pr in range(npair):
            s_ref[bi * npair + pr] = state[bi, pr]
        o = jnp.concatenate([jnp.concatenate([outs[(bi, pr), cc] for cc in range(rows // L)], axis=0)
                             for pr in range(npair)], axis=1)
        r, k, v, g = vecs[bi]
        og_ref[bi] = _rwkv_post(o, r, k, v, g, rk_ref[...], lng_ref[...], lnb_ref[...], e, hd).astype(BF16)

    @pl.when(step == pl.num_programs(0) - 1)
    def _():
        for bi in range(nb):
            for pr in range(npair):
                s = s_ref[bi * npair + pr]
                st_ref[bi, 2 * pr] = s[:hd, :hd]
                st_ref[bi, 2 * pr + 1] = s[hd:, hd:]


def _rwkv_prompt(prw, rw, nh, hd, lora):
    b, t, pw = prw.shape
    rd = nh * hd
    rows = _tile(t, RWKV_CHUNK * RWKV_CHUNKS_PER_STEP)
    names = ("mu", "w0", "w2a2", "a0", "g2", "k_k", "k_a", "r_k", "lnx_g", "lnx_b", "e")
    return pl.pallas_call(
        functools.partial(_rwkv_prompt_body, rd, lora, hd, RWKV_CHUNK),
        grid=(t // rows,),
        in_specs=[pl.BlockSpec((b, rows, pw), lambda c: (0, c, 0))] + [_resident(rw[k].shape) for k in names],
        out_specs=[pl.BlockSpec((b, rows, rd), lambda c: (0, c, 0)),
                   pl.BlockSpec((b, nh, hd, hd), lambda c: (0, 0, 0, 0))],
        out_shape=[jax.ShapeDtypeStruct((b, t, rd), BF16), jax.ShapeDtypeStruct((b, nh, hd, hd), F32)],
        scratch_shapes=[pltpu.VMEM((b, 1, pw), F32), pltpu.VMEM((b * (rd // LANES), LANES, LANES), F32)],
        compiler_params=_cparams("arbitrary"),
        name="rwkv_prompt",
    )(prw, *[rw[k] for k in names])


def _rwkv_prep_body(rd, lora, p_ref, prev_ref, mu_ref, w0_ref, w2a2_ref, a0_ref, g2_ref, kk_ref, ka_ref, e_ref,
                    r_ref, k_ref, v_ref, g_ref, *t_refs):
    r, k, v, ld, a, b, g = _rwkv_prep(p_ref[...], prev_ref[...], mu_ref[...], w0_ref[...], w2a2_ref[...], a0_ref[...],
                                      g2_ref[...], kk_ref[...], ka_ref[...], e_ref[...], rd, lora)
    r_ref[...] = r
    k_ref[...] = k
    v_ref[...] = v
    g_ref[...] = g
    for ref, x in zip(t_refs, (r, k, v, jnp.exp(ld), a, b)):
        ref[...] = x.T


def _rwkv_step_body(s_ref, r_ref, k_ref, v_ref, w_ref, a_ref, b_ref, so_ref, o_ref):
    for h in range(s_ref.shape[0]):
        s = s_ref[h]
        sa = jnp.sum(s * a_ref[h][None], axis=1)
        s = s * w_ref[h][None] + sa[:, None, :] * b_ref[h][None] + v_ref[h][:, None, :] * k_ref[h][None]
        so_ref[h] = s
        o_ref[h] = jnp.sum(s * r_ref[h][None], axis=1)


def _rwkv_post_body(hd, o_ref, r_ref, k_ref, v_ref, g_ref, rk_ref, lng_ref, lnb_ref, e_ref, og_ref):
    og_ref[...] = _rwkv_post(o_ref[...].T, r_ref[...], k_ref[...], v_ref[...], g_ref[...], rk_ref[...], lng_ref[...],
                             lnb_ref[...], e_ref[...], hd).astype(BF16)


def _rwkv_sample(prw, shift, state_t, layer, rw, nh, hd, lora, hb):
    n, pw = prw.shape
    rd = nh * hd
    names = ("mu", "w0", "w2a2", "a0", "g2", "k_k", "k_a", "e")
    full = lambda s: pl.BlockSpec(s, lambda: (0,) * len(s))
    vecs = pl.pallas_call(
        functools.partial(_rwkv_prep_body, rd, lora),
        in_specs=[full((n, pw)), full((n, pw))] + [full(rw[k].shape) for k in names],
        out_specs=[full((n, rd))] * 4 + [full((rd, n))] * 6,
        out_shape=[jax.ShapeDtypeStruct((n, rd), F32)] * 4 + [jax.ShapeDtypeStruct((rd, n), F32)] * 6,
        name="rwkv_prep",
    )(prw, shift, *[rw[k] for k in names])
    r, k, v, g = vecs[:4]
    vspec = pl.BlockSpec((hb, hd, n), lambda i: (i, 0, 0))
    sspec = pl.BlockSpec((hb, hd, hd, n), lambda i: (i, 0, 0, 0))
    s_new, o = pl.pallas_call(
        _rwkv_step_body,
        grid=(nh // hb,),
        in_specs=[pl.BlockSpec((None, hb, hd, hd, n), lambda i: (layer, i, 0, 0, 0))] + [vspec] * 6,
        out_specs=[sspec, vspec],
        out_shape=[jax.ShapeDtypeStruct(state_t.shape[1:], F32), jax.ShapeDtypeStruct((nh, hd, n), F32)],
        compiler_params=_cparams("parallel"),
        name="rwkv_step",
    )(state_t, *[x.reshape(nh, hd, n) for x in vecs[4:]])
    pnames = ("r_k", "lnx_g", "lnx_b", "e")
    og = pl.pallas_call(
        functools.partial(_rwkv_post_body, hd),
        in_specs=[full((rd, n))] + [full((n, rd))] * 4 + [full(rw[k].shape) for k in pnames],
        out_specs=full((n, rd)),
        out_shape=jax.ShapeDtypeStruct((n, rd), BF16),
        name="rwkv_post",
    )(o.reshape(rd, n), r, k, v, g, *[rw[k] for k in pnames])
    return og, s_new


def _lanes(x, n):
    return x if n == LANES else jnp.concatenate([x] * (n // LANES), axis=1)


def _attn_prompt_body(kl, tk, rg, q_ref, k_ref, o_ref, m_ref, l_ref, acc_ref, s_ref):
    i = pl.program_id(1)
    nh, tq, dk = q_ref.shape[1:]
    rows = nh * tq
    q = q_ref[0].reshape(rows, dk)
    groups = [slice(g * rg, (g + 1) * rg) for g in range(rows // rg)]
    m_ref[...] = jnp.full_like(m_ref, -jnp.inf)
    l_ref[...] = jnp.zeros_like(l_ref)
    acc_ref[...] = jnp.zeros_like(acc_ref)

    def keys(j):
        return k_ref[0, pl.ds(pl.multiple_of(j * tk, tk), tk), :]

    def scores(r, k):
        return lax.dot_general(q[r], k, (((1,), (1,)), ((), ())), preferred_element_type=F32)

    def update(r, s, v):
        m_old = m_ref[r]
        m_new = jnp.maximum(m_old, jnp.max(s, axis=-1, keepdims=True))
        alpha = jnp.exp2(m_old - m_new)
        p = jnp.exp2(s - _lanes(m_new, tk))
        l_ref[r] = alpha * l_ref[r] + jnp.sum(p, axis=-1, keepdims=True)
        acc_ref[r] = _lanes(alpha, kl) * acc_ref[r] + jnp.dot(p.astype(BF16), v, preferred_element_type=F32)
        m_ref[r] = m_new

    k0 = keys(0)
    for r in groups:
        s_ref[r] = scores(r, k0)

    def body(j, carry):
        v = keys(j)[:, :kl]
        k_next = keys(j + 1)
        for r in groups:
            s = s_ref[r]
            s_ref[r] = scores(r, k_next)
            update(r, s, v)
        return carry

    last = (i * tq) // tk
    lax.fori_loop(0, last, body, 0)
    v = keys(last)[:, :kl]
    for g, r in enumerate(groups):
        s = s_ref[r]
        qpos = i * tq + (g * rg + lax.broadcasted_iota(jnp.int32, s.shape, 0)) % tq
        kpos = last * tk + lax.broadcasted_iota(jnp.int32, s.shape, 1)
        update(r, jnp.where(kpos <= qpos, s, -jnp.inf), v)
    o_ref[0] = (acc_ref[...] / _lanes(l_ref[...], kl)).reshape(nh, tq, kl).astype(o_ref.dtype)


def _attn_prompt(q, kbf, kl):
    b, nh, t, dk = q.shape
    tq, tk = _tile(t, ATT_TQ), _tile(t, ATT_TK)
    rows = nh * tq
    rg = _tile(rows, ATT_ROW_GROUP)
    assert tk % tq == 0 and rg % tq == 0
    return pl.pallas_call(
        functools.partial(_attn_prompt_body, kl, tk, rg),
        grid=(b, t // tq),
        in_specs=[pl.BlockSpec((1, nh, tq, dk), lambda bi, i: (bi, 0, i, 0)),
                  pl.BlockSpec((1, t, dk), lambda bi, i: (bi, 0, 0))],
        out_specs=pl.BlockSpec((1, nh, tq, kl), lambda bi, i: (bi, 0, i, 0)),
        out_shape=jax.ShapeDtypeStruct((b, nh, t, kl), BF16),
        scratch_shapes=[pltpu.VMEM((rows, LANES), F32), pltpu.VMEM((rows, LANES), F32), pltpu.VMEM((rows, kl), F32),
                        pltpu.VMEM((rows, tk), F32)],
        compiler_params=_cparams("parallel", "arbitrary"),
        name="attn_prompt",
    )(q, kbf)


def _attn_sample_body(kl, npg, ngrp, layer, pt_ref, q_ref, kself_ref, cache_ref, o_ref, buf_ref, sem_ref):
    b = pl.program_id(0)
    nreq = pl.num_programs(0)

    def copies(req, grp, slot):
        return [pltpu.make_async_copy(cache_ref.at[layer, pt_ref[req, grp * npg + n]], buf_ref.at[slot, n],
                                      sem_ref.at[slot]) for n in range(npg)]

    def start(req, grp, slot):
        for c in copies(req, grp, slot):
            c.start()

    def slot_of(grp):
        return grp % 2 if ngrp % 2 == 0 else (b * ngrp + grp) % 2

    @pl.when(b == 0)
    def _():
        start(0, 0, 0)

    q = q_ref[0]
    ks = kself_ref[0]
    m = jnp.sum(q.astype(F32) * ks.astype(F32), axis=-1, keepdims=True)
    l = jnp.ones_like(m)
    acc = jnp.broadcast_to(ks[:, :kl].astype(F32), (q.shape[0], kl))
    for g in range(ngrp):
        slot = slot_of(g)
        if g + 1 < ngrp:
            start(b, g + 1, 1 - slot)
        else:
            @pl.when(b + 1 < nreq)
            def _():
                start(b + 1, 0, 1 - slot)
        for c in copies(b, g, slot):
            c.wait()
        pair = 2 if npg % 2 == 0 else 1
        kts = [jnp.concatenate([buf_ref[slot, n + e] for e in range(pair)], axis=1).astype(BF16)
               for n in range(0, npg, pair)]
        s = jnp.concatenate([jnp.dot(q, kt, preferred_element_type=F32) for kt in kts], axis=1)
        m_new = jnp.maximum(m, jnp.max(s, axis=-1, keepdims=True))
        alpha = jnp.exp2(m - m_new)
        p = jnp.exp2(s - m_new)
        l = alpha * l + jnp.sum(p, axis=-1, keepdims=True)
        pb = p.astype(BF16)
        w = kts[0].shape[1]
        pv = _mm_nt(pb[:, :w], kts[0][:kl, :])
        for n in range(1, len(kts)):
            pv = pv + _mm_nt(pb[:, n * w:(n + 1) * w], kts[n][:kl, :])
        acc = alpha * acc + pv
        m = m_new
    o_ref[0] = (acc / l).astype(o_ref.dtype)


def _attn_sample(q, kself, cache_t, layer, page_table, kl):
    n, nh, dk = q.shape
    ps = cache_t.shape[3]
    npages = page_table.shape[1]
    npg = _tile(npages, PAGES_PER_GROUP)
    grid_spec = pltpu.PrefetchScalarGridSpec(
        num_scalar_prefetch=1,
        grid=(n,),
        in_specs=[pl.BlockSpec((1, nh, dk), lambda bi, pt: (bi, 0, 0)),
                  pl.BlockSpec((1, 1, dk), lambda bi, pt: (bi, 0, 0)),
                  pl.BlockSpec(memory_space=pl.ANY)],
        out_specs=pl.BlockSpec((1, nh, kl), lambda bi, pt: (bi, 0, 0)),
        scratch_shapes=[pltpu.VMEM((2, npg, dk, ps), F32), pltpu.SemaphoreType.DMA((2,))],
    )
    return pl.pallas_call(
        functools.partial(_attn_sample_body, kl, npg, npages // npg, layer),
        grid_spec=grid_spec,
        out_shape=jax.ShapeDtypeStruct((n, nh, kl), BF16),
        compiler_params=_cparams("arbitrary"),
        name="attn_sample",
    )(page_table, q, kself, cache_t)


def _memkv_body(m_ref, g_ref, w_ref, o_ref):
    o_ref[...] = jnp.dot(_rms(m_ref[...], g_ref[...]).astype(BF16), w_ref[...], preferred_element_type=F32)


def _memkv(mem, g, wkv):
    n, d = mem.shape
    full = lambda s: pl.BlockSpec(s, lambda: (0,) * len(s))
    return pl.pallas_call(
        _memkv_body,
        in_specs=[full((n, d)), full((1, d)), full(wkv.shape)],
        out_specs=full((n, wkv.shape[1])),
        out_shape=jax.ShapeDtypeStruct((n, wkv.shape[1]), F32),
        name="mem_kv",
    )(mem, g, wkv)


def _memattn_prompt_body(nh, scale, q_ref, k_ref, v_ref, o_ref):
    hd = q_ref.shape[1] // nh
    q, k, v = q_ref[...], k_ref[0], v_ref[0]
    heads = [slice(h * hd, (h + 1) * hd) for h in range(nh)]
    ss = [_mm_nt(q[:, sl], k[:, sl]) * scale for sl in heads]
    ps = [jnp.exp(s - jnp.max(s, axis=-1, keepdims=True)) for s in ss]
    ps = [p / jnp.sum(p, axis=-1, keepdims=True) for p in ps]
    o_ref[...] = jnp.concatenate([_mm(p, v[:, sl]) for p, sl in zip(ps, heads)], axis=1).astype(o_ref.dtype)


def _memattn_prompt(q, mk, mv, nh, scale, tm):
    n, md = q.shape
    b, m, _ = mk.shape
    nb = (n // b) // tm
    kv = pl.BlockSpec((1, m, md), lambda i: (i // nb, 0, 0))
    return pl.pallas_call(
        functools.partial(_memattn_prompt_body, nh, scale),
        grid=(n // tm,),
        in_specs=[pl.BlockSpec((tm, md), lambda i: (i, 0)), kv, kv],
        out_specs=pl.BlockSpec((tm, md), lambda i: (i, 0)),
        out_shape=jax.ShapeDtypeStruct((n, md), BF16),
        compiler_params=_cparams("parallel"),
        name="memattn_prompt",
    )(q, mk, mv)


def _memattn_sample_body(scale, q_ref, k_ref, v_ref, o_ref):
    gr, rows, _ = k_ref.shape
    nh = q_ref.shape[1]
    own = (lax.broadcasted_iota(jnp.int32, (nh, rows), 1) % nh) == lax.broadcasted_iota(jnp.int32, (nh, rows), 0)
    ss = [jnp.where(own, _mm_nt(q_ref[g], k_ref[g]) * scale, -jnp.inf) for g in range(gr)]
    ps = [jnp.exp(s - jnp.max(s, axis=-1, keepdims=True)) for s in ss]
    ps = [p / jnp.sum(p, axis=-1, keepdims=True) for p in ps]
    for g in range(gr):
        o_ref[g] = _mm(ps[g], v_ref[g]).astype(o_ref.dtype)


def _memattn_sample(q, cache_k, cache_v, layer, scale, gr):
    n, md = q.shape
    depth, _, m, nh, hd = cache_k.shape
    kv = pl.BlockSpec((None, gr, m * nh, hd), lambda i: (layer, i, 0, 0))
    qs = pl.BlockSpec((gr, nh, hd), lambda i: (i, 0, 0))
    return pl.pallas_call(
        functools.partial(_memattn_sample_body, scale),
        grid=(n // gr,),
        in_specs=[qs, kv, kv],
        out_specs=qs,
        out_shape=jax.ShapeDtypeStruct((n, nh, hd), BF16),
        compiler_params=_cparams("parallel"),
        name="memattn_sample",
    )(q.reshape(n, nh, hd), cache_k.reshape(depth, n, m * nh, hd), cache_v.reshape(depth, n, m * nh, hd)).reshape(n, md)


def _merge_body(nh, h_ref, og_ref, ctx_ref, om_ref, gates_ref, wo_ref, wuv_ref, mwo_ref, memwo_ref, wout_ref, post_ref,
                o_ref):
    d = h_ref.shape[1]
    o_rwkv = jnp.dot(og_ref[...], wo_ref[...], preferred_element_type=F32)
    vs = []
    for pr in range(nh // 2):
        vp = (jnp.dot(ctx_ref[0, 2 * pr], wuv_ref[2 * pr], preferred_element_type=F32)
              + jnp.dot(ctx_ref[0, 2 * pr + 1], wuv_ref[2 * pr + 1], preferred_element_type=F32))
        vs.append(vp.astype(BF16))
    o_mla = jnp.dot(jnp.concatenate(vs, axis=1), mwo_ref[...], preferred_element_type=F32)
    o_mem = jnp.dot(om_ref[...], memwo_ref[...], preferred_element_type=F32)
    merged = gates_ref[:, 0:d] * o_rwkv + gates_ref[:, d:2 * d] * o_mla + gates_ref[:, 2 * d:3 * d] * o_mem
    y = jnp.dot(merged.astype(BF16), wout_ref[...], preferred_element_type=F32)
    o_ref[...] = h_ref[...] + _rms(y, post_ref[...])


def _merge(h, nbatch, og, ctx, om, gates, wo, wuv, mwo, memwo, wout, post, nh, tm):
    n, d = h.shape
    nb = (n // nbatch) // tm
    row = lambda i: (i, 0)
    kl = ctx.shape[-1]
    return pl.pallas_call(
        functools.partial(_merge_body, nh),
        grid=(n // tm,),
        in_specs=[pl.BlockSpec((tm, d), row), pl.BlockSpec((tm, og.shape[1]), row),
                  pl.BlockSpec((1, nh, tm, kl), lambda i: (i // nb, 0, i % nb, 0)),
                  pl.BlockSpec((tm, om.shape[1]), row), pl.BlockSpec((tm, 3 * d), row),
                  _resident(wo.shape), _resident(wuv.shape), _resident(mwo.shape), _resident(memwo.shape),
                  _resident(wout.shape), _resident((1, d))],
        out_specs=pl.BlockSpec((tm, d), row),
        out_shape=jax.ShapeDtypeStruct((n, d), F32),
        compiler_params=_cparams("parallel"),
        name="merge",
    )(h, og, ctx, om, gates, wo, wuv, mwo, memwo, wout, post)


def _rope_tables(pos, rope, nh):
    half = rope // 2
    freqs = ROPE_BASE ** (-jnp.arange(half, dtype=F32) / half)
    ang = pos.astype(F32)[:, None] * freqs
    cos, sin = jnp.cos(ang), jnp.sin(ang)
    return jnp.tile(jnp.concatenate([cos, cos], axis=1), (1, nh)), jnp.tile(jnp.concatenate([-sin, sin], axis=1), (1, nh))


def _prep_weights(W, d):
    nh, hd = W["rwkv_r_k"].shape
    rd = nh * hd
    lora = W["rwkv_w2"].shape[0]
    glora = W["rwkv_g2"].shape[0]
    rp = 3 * rd + 2 * lora + glora
    ql = W["mla_q_norm"].shape[0]
    kl, mh, vh = W["mla_w_uv"].shape
    nope = W["mla_w_uk"].shape[2]
    rope = W["mla_w_qb"].shape[1] // mh - nope
    md = W["mem_w_k"].shape[1]
    half = rope // 2
    row = lambda x: x.reshape(1, -1)
    w_in = W["w_in"]
    o_cq, o_kv, o_pe, o_mem, o_g = rp, rp + ql, rp + ql + kl, rp + ql + kl + rope, rp + ql + kl + rope + md
    cols = [w_in[:, :o_pe], w_in[:, o_mem:], w_in[:, o_pe:o_mem],
            w_in[:, o_pe + half:o_mem], w_in[:, o_pe:o_pe + half]]
    width = sum(c.shape[1] for c in cols)
    pad = (-width) % LANES
    win = jnp.concatenate(cols + [jnp.zeros((d, pad), F32)], axis=1).astype(BF16)
    wqb = W["mla_w_qb"].reshape(ql, mh, nope + rope)
    wqb = jnp.concatenate([wqb[:, :, :nope].reshape(ql, mh * nope),
                           wqb[:, :, nope:].reshape(ql, mh * rope),
                           jnp.concatenate([wqb[:, :, nope + half:], wqb[:, :, nope:nope + half]], axis=2).reshape(ql, mh * rope)],
                          axis=1).astype(BF16)
    ukt = jnp.transpose(W["mla_w_uk"], (1, 2, 0))
    z = jnp.zeros_like(ukt[0])
    wuk = jnp.stack([jnp.concatenate([jnp.concatenate([ukt[2 * p], z], axis=1),
                                      jnp.concatenate([z, ukt[2 * p + 1]], axis=1)], axis=0)
                     for p in range(mh // 2)]).astype(BF16)
    uv = jnp.transpose(W["mla_w_uv"], (1, 0, 2))
    zv = jnp.zeros_like(uv[0])
    wuv = jnp.stack([jnp.concatenate([uv[h], zv] if h % 2 == 0 else [zv, uv[h]], axis=1)
                     for h in range(mh)]).astype(BF16)
    zl = jnp.zeros((lora, rd), F32)
    w2a2 = jnp.concatenate([jnp.concatenate([W["rwkv_w2"], zl], axis=1),
                            jnp.concatenate([zl, W["rwkv_a2"]], axis=1)], axis=0).astype(BF16)
    hid = jnp.arange(rd) // hd
    rw = dict(mu=row(W["rwkv_mu"]), w0=row(W["rwkv_w0"]), w2a2=w2a2, a0=row(W["rwkv_a0"]), g2=W["rwkv_g2"].astype(BF16),
              k_k=row(W["rwkv_k_k"]), k_a=row(W["rwkv_k_a"]), r_k=row(W["rwkv_r_k"]), lnx_g=row(W["rwkv_lnx_g"]),
              lnx_b=row(W["rwkv_lnx_b"]), e=(hid[:, None] == hid[None, :]).astype(BF16))
    dims = (rp, ql, kl, rope, md, 3 * d, mh, nope)
    return dict(
        dims=dims, nh=nh, hd=hd, lora=lora, rw=rw, win=win, wqb=wqb, wuk=wuk, wuv=wuv,
        ffn1=(row(W["ffn1_pre"]), row(W["ffn1_post"]), W["ffn1_gate"].astype(BF16), W["ffn1_up"].astype(BF16),
              W["ffn1_down"].astype(BF16)),
        ffn2=(row(W["ffn2_pre"]), row(W["ffn2_post"]), W["ffn2_gate"].astype(BF16), W["ffn2_up"].astype(BF16),
              W["ffn2_down"].astype(BF16)),
        mix_pre=row(W["mix_pre"]), mix_post=row(W["mix_post"]), q_norm=row(W["mla_q_norm"]), kv_norm=row(W["mla_kv_norm"]),
        mem_norm=row(W["mem_norm"]), mem_wkv=jnp.concatenate([W["mem_w_k"], W["mem_w_v"]], axis=1).astype(BF16),
        rwkv_wo=W["rwkv_w_o"].astype(BF16), mla_wo=W["mla_w_o"].astype(BF16), mem_wo=W["mem_w_o"].astype(BF16),
        w_out=W["w_out"].astype(BF16), qscale=float(nope + rope) ** -0.5 * LOG2E,
    )


def _tile(n, pref):
    t = min(pref, n)
    assert n % t == 0, (n, t)
    return t


def kernel(x_prompt, x_sample, cache_mla, state_rwkv, state_shift, cache_mem_k, cache_mem_v, page_table, mem_prompt, ffn1_pre, ffn1_post, ffn1_gate, ffn1_up, ffn1_down, mix_pre, mix_post, w_in, rwkv_mu, rwkv_w0, rwkv_w2, rwkv_a0, rwkv_a2, rwkv_g2, rwkv_k_k, rwkv_k_a, rwkv_r_k, rwkv_lnx_g, rwkv_lnx_b, rwkv_w_o, mla_q_norm, mla_w_qb, mla_kv_norm, mla_w_uk, mla_w_uv, mla_w_o, mem_norm, mem_w_k, mem_w_v, mem_w_o, w_out, ffn2_pre, ffn2_post, ffn2_gate, ffn2_up, ffn2_down):
    names = ("ffn1_pre", "ffn1_post", "ffn1_gate", "ffn1_up", "ffn1_down", "mix_pre", "mix_post", "w_in",
             "rwkv_mu", "rwkv_w0", "rwkv_w2", "rwkv_a0", "rwkv_a2", "rwkv_g2", "rwkv_k_k", "rwkv_k_a", "rwkv_r_k",
             "rwkv_lnx_g", "rwkv_lnx_b", "rwkv_w_o", "mla_q_norm", "mla_w_qb", "mla_kv_norm", "mla_w_uk", "mla_w_uv",
             "mla_w_o", "mem_norm", "mem_w_k", "mem_w_v", "mem_w_o", "w_out", "ffn2_pre", "ffn2_post", "ffn2_gate",
             "ffn2_up", "ffn2_down")
    stacked = (ffn1_pre, ffn1_post, ffn1_gate, ffn1_up, ffn1_down, mix_pre, mix_post, w_in,
               rwkv_mu, rwkv_w0, rwkv_w2, rwkv_a0, rwkv_a2, rwkv_g2, rwkv_k_k, rwkv_k_a, rwkv_r_k,
               rwkv_lnx_g, rwkv_lnx_b, rwkv_w_o, mla_q_norm, mla_w_qb, mla_kv_norm, mla_w_uk, mla_w_uv,
               mla_w_o, mem_norm, mem_w_k, mem_w_v, mem_w_o, w_out, ffn2_pre, ffn2_post, ffn2_gate,
               ffn2_up, ffn2_down)
    B, S, D = x_prompt.shape
    DB, T, _ = x_sample.shape
    assert T == 1, "decode groups carry one new token per request"
    depth = ffn1_pre.shape[0]
    page = cache_mla.shape[2]
    past_len = page_table.shape[1] * page
    mem_tokens, mem_heads, mem_hd = cache_mem_k.shape[2:]
    mem_scale = float(mem_hd) ** -0.5

    cache_t = jnp.swapaxes(cache_mla, 2, 3)
    state_t = jnp.transpose(state_rwkv, (0, 2, 3, 4, 1))
    xp = x_prompt.reshape(B * S, D)
    xs = x_sample.reshape(DB, D)
    outs = [[] for _ in range(8)]
    for l in range(depth):
        P = _prep_weights({n: w[l] for n, w in zip(names, stacked)}, D)
        rp, ql, kl, rope, md, gd, mh, nope = P["dims"]
        nh, hd, lora, rw = P["nh"], P["hd"], P["lora"], P["rw"]
        cos_p, sin_p = _rope_tables(jnp.arange(S), rope, mh)
        cos_s, sin_s = _rope_tables(jnp.full((DB,), past_len), rope, mh)
        tm_p, tm_w, tm_s = _tile(S, ROWS_PER_STEP), _tile(S, ROWS_PER_STEP_WIDE), DB

        h = _ffn(xs, *P["ffn1"], tm_s)
        prw_s, q, rows_s, kbf, qmem, gates = _inproj(h, 1, P["mix_pre"], P["win"], P["q_norm"], P["wqb"], P["wuk"],
                                                     P["kv_norm"], cos_s, sin_s, P["dims"], P["qscale"], tm_s)
        og, wkv_t = _rwkv_sample(prw_s, state_shift[l], state_t, l, rw, nh, hd, lora, _tile(nh, 2))
        wkv_s = jnp.transpose(wkv_t, (3, 0, 1, 2))
        ctx = _attn_sample(jnp.swapaxes(q[0], 0, 1), kbf.reshape(DB, 1, kl + rope), cache_t, l, page_table, kl)
        om = _memattn_sample(qmem, cache_mem_k, cache_mem_v, l, mem_scale, _tile(DB, 4))
        h = _merge(h, 1, og, jnp.swapaxes(ctx, 0, 1)[None], om, gates, P["rwkv_wo"], P["wuv"], P["mla_wo"], P["mem_wo"],
                   P["w_out"], P["mix_post"], mh, tm_s)
        xs = _ffn(h, *P["ffn2"], tm_s)

        mkv = _memkv(mem_prompt.reshape(B * mem_tokens, D), P["mem_norm"], P["mem_wkv"])
        mk_p, mv_p = mkv[:, :md].reshape(B, mem_tokens, md), mkv[:, md:].reshape(B, mem_tokens, md)
        h = _ffn(xp, *P["ffn1"], tm_w)
        prw, q, rows, kbf, qmem, gates = _inproj(h, B, P["mix_pre"], P["win"], P["q_norm"], P["wqb"], P["wuk"],
                                                 P["kv_norm"], cos_p, sin_p, P["dims"], P["qscale"], tm_p)
        og, wkv_p = _rwkv_prompt(prw.reshape(B, S, rp), rw, nh, hd, lora)
        ctx = _attn_prompt(q, kbf.reshape(B, S, kl + rope), kl)
        om = _memattn_prompt(qmem, mk_p, mv_p, mem_heads, mem_scale, tm_w)
        h = _merge(h, B, og.reshape(B * S, nh * hd), ctx, om, gates, P["rwkv_wo"], P["wuv"], P["mla_wo"], P["mem_wo"],
                   P["w_out"], P["mix_post"], mh, tm_w)
        xp = _ffn(h, *P["ffn2"], tm_w)
        rows_p, shift_p = rows.reshape(B, S, kl + rope), prw.reshape(B, S, rp)[:, -1]

        for lst, val in zip(outs, (rows_p, rows_s.reshape(DB, T, kl + rope), wkv_p, wkv_s, shift_p, prw_s,
                                   mk_p.reshape(B, mem_tokens, mem_heads, mem_hd),
                                   mv_p.reshape(B, mem_tokens, mem_heads, mem_hd))):
            lst.append(val)
    return (xp.reshape(B, S, D), xs.reshape(DB, T, D)) + tuple(jnp.stack(o) for o in outs)
```

```python
import functools

import jax
import jax.numpy as jnp
from jax import lax
from jax.experimental import pallas as pl
from jax.experimental.pallas import tpu as pltpu

F32, BF16 = jnp.float32, jnp.bfloat16
RMS_EPS = 1e-6
LNX_EPS = 64e-5
ROPE_BASE = 10000.0
LANES = 128
VMEM_LIMIT = 52 * 1024 * 1024
ROWS_PER_STEP = 256
ROWS_PER_STEP_WIDE = 512
RWKV_CHUNK = 64
RWKV_CHUNKS_PER_STEP = 2
ATT_TQ = 256
ATT_TK = 512
ATT_ROW_GROUP = 512
PAGES_PER_GROUP = 32
PAGE_SLOTS = 4
LOG2E = 1.4426950408889634


def _cparams(*sem):
    return pltpu.CompilerParams(dimension_semantics=sem, vmem_limit_bytes=VMEM_LIMIT)


def _resident(shape):
    nd = len(shape)
    return pl.BlockSpec(shape, lambda *_: (0,) * nd, pipeline_mode=pl.Buffered(1))


def _rms(x, g):
    return x * lax.rsqrt(jnp.mean(x * x, axis=-1, keepdims=True) + RMS_EPS) * g


def _sigmoid(x):
    return 1.0 / (1.0 + jnp.exp(-x))


def _mm(a, b):
    return jnp.dot(a.astype(BF16), b.astype(BF16), preferred_element_type=F32)


def _mm_nt(a, b):
    return lax.dot_general(a.astype(BF16), b.astype(BF16), (((1,), (1,)), ((), ())), preferred_element_type=F32)


def _ffn_body(x_ref, pre_ref, post_ref, wg_ref, wu_ref, wd_ref, o_ref):
    x = x_ref[...]
    h = _rms(x, pre_ref[...]).astype(BF16)
    g = jnp.dot(h, wg_ref[...], preferred_element_type=F32)
    u = jnp.dot(h, wu_ref[...], preferred_element_type=F32)
    act = (g * _sigmoid(g)) * u
    y = jnp.dot(act.astype(BF16), wd_ref[...], preferred_element_type=F32)
    o_ref[...] = x + 0.5 * _rms(y, post_ref[...])


def _ffn(x, pre, post, wg, wu, wd, tm):
    n, d = x.shape
    f = wg.shape[1]
    return pl.pallas_call(
        _ffn_body,
        grid=(n // tm,),
        in_specs=[pl.BlockSpec((tm, d), lambda i: (i, 0)), _resident((1, d)), _resident((1, d)),
                  _resident((d, f)), _resident((d, f)), _resident((f, d))],
        out_specs=pl.BlockSpec((tm, d), lambda i: (i, 0)),
        out_shape=jax.ShapeDtypeStruct((n, d), F32),
        compiler_params=_cparams("parallel"),
        name="ffn",
    )(x, pre, post, wg, wu, wd)


def _inproj_body(dims, qscale, h_ref, pre_ref, win_ref, qn_ref, wqb_ref, wuk_ref, kvn_ref, cos_ref, sin_ref,
                 prw_ref, q_ref, rows_ref, kbf_ref, qmem_ref, gates_ref):
    rp, ql, kl, rope, md, gd, nh, nope = dims
    u = _rms(h_ref[...], pre_ref[...]).astype(BF16)
    p = jnp.dot(u, win_ref[...], preferred_element_type=F32)
    o = 0
    prw_ref[...] = p[:, o:o + rp]; o += rp
    cq = p[:, o:o + ql]; o += ql
    ckv = p[:, o:o + kl]; o += kl
    qmem_ref[...] = p[:, o:o + md].astype(BF16); o += md
    gates_ref[...] = _sigmoid(p[:, o:o + gd]); o += gd
    kpe = p[:, o:o + rope]; o += rope
    kpe_sw = p[:, o:o + rope]
    cos = cos_ref[...]
    sin = sin_ref[...]
    q = jnp.dot(_rms(cq, qn_ref[...]).astype(BF16), wqb_ref[...], preferred_element_type=F32)
    nn = nh * nope
    nr = nh * rope
    qpe = ((q[:, nn:nn + nr] * cos + q[:, nn + nr:nn + 2 * nr] * sin) * qscale).astype(BF16)
    qn = q[:, :nn].astype(BF16)
    for pr in range(nh // 2):
        qlat = (jnp.dot(qn[:, LANES * pr:LANES * (pr + 1)], wuk_ref[pr], preferred_element_type=F32) * qscale).astype(BF16)
        for e in range(2):
            hh = 2 * pr + e
            q_ref[0, hh, :, 0:kl] = qlat[:, kl * e:kl * (e + 1)]
            q_ref[0, hh, :, kl:kl + rope] = qpe[:, rope * hh:rope * (hh + 1)]
    ckvn = _rms(ckv, kvn_ref[...])
    kper = kpe * cos[:, :rope] + kpe_sw * sin[:, :rope]
    rows_ref[:, 0:kl] = ckvn
    rows_ref[:, kl:kl + rope] = kper
    kbf_ref[:, 0:kl] = ckvn.astype(BF16)
    kbf_ref[:, kl:kl + rope] = kper.astype(BF16)


def _inproj(h, nbatch, pre, win, qn, wqb, wuk, kvn, cos, sin, dims, qscale, tm):
    n, d = h.shape
    rp, ql, kl, rope, md, gd, nh, nope = dims
    t = n // nbatch
    nb = t // tm
    cw = win.shape[1]
    row = lambda i: (i, 0)
    tab = lambda i: (i % nb, 0)
    return pl.pallas_call(
        functools.partial(_inproj_body, dims, qscale),
        grid=(n // tm,),
        in_specs=[pl.BlockSpec((tm, d), row), _resident((1, d)), _resident((d, cw)), _resident((1, ql)),
                  _resident(wqb.shape), _resident(wuk.shape), _resident((1, kl)),
                  pl.BlockSpec((tm, nh * rope), tab), pl.BlockSpec((tm, nh * rope), tab)],
        out_specs=[pl.BlockSpec((tm, rp), row),
                   pl.BlockSpec((1, nh, tm, kl + rope), lambda i: (i // nb, 0, i % nb, 0)),
                   pl.BlockSpec((tm, kl + rope), row), pl.BlockSpec((tm, kl + rope), row),
                   pl.BlockSpec((tm, md), row), pl.BlockSpec((tm, gd), row)],
        out_shape=[jax.ShapeDtypeStruct((n, rp), F32),
                   jax.ShapeDtypeStruct((nbatch, nh, t, kl + rope), BF16),
                   jax.ShapeDtypeStruct((n, kl + rope), F32),
                   jax.ShapeDtypeStruct((n, kl + rope), BF16),
                   jax.ShapeDtypeStruct((n, md), BF16),
                   jax.ShapeDtypeStruct((n, gd), F32)],
        compiler_params=_cparams("parallel"),
        name="inproj",
    )(h, pre, win, qn, wqb, wuk, kvn, cos, sin)


def _segsum(x, e):
    hi = x.astype(BF16)
    lo = (x - hi.astype(F32)).astype(BF16)
    return jnp.dot(hi, e, preferred_element_type=F32) + jnp.dot(lo, e, preferred_element_type=F32)


def _rwkv_prep(p, prev, mu, w0, w2a2, a0, g2, k_k, k_a, e, rd, lora):
    ps = p + (prev - p) * mu
    r = ps[:, 0:rd]
    k = ps[:, rd:2 * rd]
    v = ps[:, 2 * rd:3 * rd]
    wa = ps[:, 3 * rd:3 * rd + 2 * lora]
    gl = ps[:, 3 * rd + 2 * lora:]
    lane = lax.broadcasted_iota(jnp.int32, wa.shape, 1)
    wa = jnp.where(lane < lora, jnp.tanh(wa), wa)
    wa2 = _mm(wa, w2a2)
    x = -(w0 + wa2[:, :rd])
    softplus = jnp.maximum(x, 0.0) + jnp.log(1.0 + jnp.exp(-jnp.abs(x)))
    logdec = -jnp.exp(-softplus - 0.5)
    a = _sigmoid(a0 + wa2[:, rd:])
    g = _mm(_sigmoid(gl), g2)
    kk = k * k_k
    kk = kk / jnp.maximum(jnp.sqrt(_segsum(kk * kk, e)), 1e-12)
    k = k * (1.0 + (a - 1.0) * k_a)
    return r, k, v, logdec, -kk, kk * a, g


def _rwkv_post(o, r, k, v, g, r_k, lng, lnb, e, hd):
    mean = _segsum(o, e) * (1.0 / hd)
    oc = o - mean
    var = _segsum(oc * oc, e) * (1.0 / hd)
    o = oc * lax.rsqrt(var + LNX_EPS) * lng + lnb
    bonus = _segsum(r * k * r_k, e) * v
    return (o + bonus) * g


def _pair_rows(y):
    lo = (lax.broadcasted_iota(jnp.int32, y.shape, 1) % LANES) < (LANES // 2)
    z = jnp.zeros_like(y)
    return jnp.concatenate([jnp.where(lo, y, z), jnp.where(lo, z, y)], axis=0)


def _rwkv_chunks_local(items, strict, incl):
    L = items[0][0].shape[0]
    cs = []
    for r, k, v, ld, cum, a, b in items:
        cum_l = cum[L - 1:L, :]
        e_neg = jnp.exp(-cum)
        e_end = jnp.exp(cum_l - cum)
        kt, bt = (k * e_neg).astype(BF16), (b * e_neg).astype(BF16)
        at, rt = a * jnp.exp(cum - ld), (r * jnp.exp(cum)).astype(BF16)
        cs.append(dict(at=at, rt=rt, vb=_pair_rows(v.astype(BF16)), v=v, dl=jnp.exp(cum_l),
                       bkh=jnp.concatenate([b * e_end, k * e_end], axis=0).astype(BF16),
                       lhs=jnp.concatenate([at.astype(BF16), rt], axis=0),
                       rhs=jnp.concatenate([_pair_rows(bt), _pair_rows(kt)], axis=0)))
    for c in cs:
        mm = _mm_nt(c.pop("lhs"), c.pop("rhs"))
        c["pw"] = jnp.where(strict, mm[:L, :2 * L], 0.0)
        c["m_ka"] = jnp.where(strict, mm[:L, 2 * L:], 0.0)
        c["m_r"] = jnp.concatenate([jnp.where(incl, mm[L:, :2 * L], 0.0), jnp.where(incl, mm[L:, 2 * L:], 0.0)],
                                   axis=1).astype(BF16)
    for c in cs:
        c["x"] = jnp.concatenate([c.pop("at"), _mm(c.pop("m_ka"), c["vb"])], axis=1)
    span = 1
    while span < L:
        for c in cs:
            c["x"] = c["x"] + _mm(c["pw"], _pair_rows(c["x"].astype(BF16)))
        span *= 2
        if span < L:
            for c in cs:
                c["pw"] = _mm(c["pw"], _pair_rows(c["pw"].astype(BF16)))
    for c in cs:
        x = c.pop("x")
        c["w1"], c["uloc"] = x[:, :LANES].astype(BF16), x[:, LANES:]
    return cs


def _rwkv_chunks_apply(cs, states, diag):
    sbs = [s.astype(BF16) for s in states]
    urs = [_mm_nt(c["w1"], sb) + c["uloc"] for c, sb in zip(cs, sbs)]
    o1 = [_mm_nt(c["rt"], sb) for c, sb in zip(cs, sbs)]
    upds = [_mm(jnp.concatenate([ur, c["v"]], axis=0).T, c["bkh"]) for c, ur in zip(cs, urs)]
    o2 = [_mm(c["m_r"], jnp.concatenate([_pair_rows(ur.astype(BF16)), c["vb"]], axis=0)) for c, ur in zip(cs, urs)]
    return [(a + b, s * c["dl"] + jnp.where(diag, u, 0.0)) for a, b, s, c, u in zip(o1, o2, states, cs, upds)]


def _rwkv_prompt_body(rd, lora, hd, L, p_ref, mu_ref, w0_ref, w2a2_ref, a0_ref, g2_ref, kk_ref, ka_ref, rk_ref,
                      lng_ref, lnb_ref, e_ref, og_ref, st_ref, prev_ref, s_ref):
    step = pl.program_id(0)
    nb, rows, _ = p_ref.shape
    npair = rd // LANES

    @pl.when(step == 0)
    def _():
        prev_ref[...] = jnp.zeros_like(prev_ref)
        s_ref[...] = jnp.zeros_like(s_ref)

    e = e_ref[...]
    ti = lax.broadcasted_iota(jnp.int32, (L, 2 * L), 0)
    si = lax.broadcasted_iota(jnp.int32, (L, 2 * L), 1) % L
    strict, incl = si < ti, si <= ti
    half = LANES // 2
    diag = ((lax.broadcasted_iota(jnp.int32, (LANES, LANES), 0) < half)
            == (lax.broadcasted_iota(jnp.int32, (LANES, LANES), 1) < half))
    tr = lax.broadcasted_iota(jnp.int32, (rows, rows), 0)
    tc = lax.broadcasted_iota(jnp.int32, (rows, rows), 1)
    tri = ((tc <= tr) & (tc // L == tr // L)).astype(BF16)
    rowi = lax.broadcasted_iota(jnp.int32, (rows, p_ref.shape[2]), 0)

    keys, items, vecs = [], [], []
    for bi in range(nb):
        p = p_ref[bi]
        prev = jnp.where(rowi == 0, prev_ref[bi], pltpu.roll(p, 1, axis=0))
        prev_ref[bi] = p[rows - 1:rows, :]
        r, k, v, ld, a, b, g = _rwkv_prep(p, prev, mu_ref[...], w0_ref[...], w2a2_ref[...], a0_ref[...], g2_ref[...],
                                          kk_ref[...], ka_ref[...], e, rd, lora)
        hi = ld.astype(BF16)
        r1 = ld - hi.astype(F32)
        mid = r1.astype(BF16)
        lo = (r1 - mid.astype(F32)).astype(BF16)
        cum = (jnp.dot(tri, hi, preferred_element_type=F32) + jnp.dot(tri, mid, preferred_element_type=F32)
               + jnp.dot(tri, lo, preferred_element_type=F32))
        vecs.append((r, k, v, g))
        for cc in range(rows // L):
            for pr in range(npair):
                sl = (slice(L * cc, L * (cc + 1)), slice(LANES * pr, LANES * (pr + 1)))
                keys.append((bi, cc, pr))
                items.append((r[sl], k[sl], v[sl], ld[sl], cum[sl], a[sl], b[sl]))
    local = dict(zip(keys, _rwkv_chunks_local(items, strict, incl)))
    chains = [(bi, pr) for bi in range(nb) for pr in range(npair)]
    state = {ch: s_ref[ch[0] * npair + ch[1]] for ch in chains}
    outs = {}
    for cc in range(rows // L):
        new = _rwkv_chunks_apply([local[bi, cc, pr] for bi, pr in chains], [state[ch] for ch in chains], diag)
        for ch, (o, s) in zip(chains, new):
            outs[ch, cc], state[ch] = o, s
    for bi in range(nb):
        for pr in range(npair):
            s_ref[bi * npair + pr] = state[bi, pr]
        o = jnp.concatenate([jnp.concatenate([outs[(bi, pr), cc] for cc in range(rows // L)], axis=0)
                             for pr in range(npair)], axis=1)
        r, k, v, g = vecs[bi]
        og_ref[bi] = _rwkv_post(o, r, k, v, g, rk_ref[...], lng_ref[...], lnb_ref[...], e, hd).astype(BF16)

    @pl.when(step == pl.num_programs(0) - 1)
    def _():
        for bi in range(nb):
            for pr in range(npair):
                s = s_ref[bi * npair + pr]
                st_ref[bi, 2 * pr] = s[:hd, :hd]
                st_ref[bi, 2 * pr + 1] = s[hd:, hd:]


def _rwkv_prompt(prw, rw, nh, hd, lora):
    b, t, pw = prw.shape
    rd = nh * hd
    rows = _tile(t, RWKV_CHUNK * RWKV_CHUNKS_PER_STEP)
    names = ("mu", "w0", "w2a2", "a0", "g2", "k_k", "k_a", "r_k", "lnx_g", "lnx_b", "e")
    return pl.pallas_call(
        functools.partial(_rwkv_prompt_body, rd, lora, hd, RWKV_CHUNK),
        grid=(t // rows,),
        in_specs=[pl.BlockSpec((b, rows, pw), lambda c: (0, c, 0))] + [_resident(rw[k].shape) for k in names],
        out_specs=[pl.BlockSpec((b, rows, rd), lambda c: (0, c, 0)),
                   pl.BlockSpec((b, nh, hd, hd), lambda c: (0, 0, 0, 0))],
        out_shape=[jax.ShapeDtypeStruct((b, t, rd), BF16), jax.ShapeDtypeStruct((b, nh, hd, hd), F32)],
        scratch_shapes=[pltpu.VMEM((b, 1, pw), F32), pltpu.VMEM((b * (rd // LANES), LANES, LANES), F32)],
        compiler_params=_cparams("arbitrary"),
        name="rwkv_prompt",
    )(prw, *[rw[k] for k in names])


def _rwkv_prep_body(rd, lora, p_ref, prev_ref, mu_ref, w0_ref, w2a2_ref, a0_ref, g2_ref, kk_ref, ka_ref, e_ref,
                    r_ref, k_ref, v_ref, g_ref, *t_refs):
    r, k, v, ld, a, b, g = _rwkv_prep(p_ref[...], prev_ref[...], mu_ref[...], w0_ref[...], w2a2_ref[...], a0_ref[...],
                                      g2_ref[...], kk_ref[...], ka_ref[...], e_ref[...], rd, lora)
    r_ref[...] = r
    k_ref[...] = k
    v_ref[...] = v
    g_ref[...] = g
    for ref, x in zip(t_refs, (r, k, v, jnp.exp(ld), a, b)):
        ref[...] = x.T


def _rwkv_step_body(s_ref, r_ref, k_ref, v_ref, w_ref, a_ref, b_ref, so_ref, o_ref):
    for h in range(s_ref.shape[0]):
        s = s_ref[h]
        sa = jnp.sum(s * a_ref[h][None], axis=1)
        s = s * w_ref[h][None] + sa[:, None, :] * b_ref[h][None] + v_ref[h][:, None, :] * k_ref[h][None]
        so_ref[h] = s
        o_ref[h] = jnp.sum(s * r_ref[h][None], axis=1)


def _rwkv_post_body(hd, o_ref, r_ref, k_ref, v_ref, g_ref, rk_ref, lng_ref, lnb_ref, e_ref, og_ref):
    og_ref[...] = _rwkv_post(o_ref[...].T, r_ref[...], k_ref[...], v_ref[...], g_ref[...], rk_ref[...], lng_ref[...],
                             lnb_ref[...], e_ref[...], hd).astype(BF16)


def _rwkv_sample(prw, shift, state_t, layer, rw, nh, hd, lora, hb):
    n, pw = prw.shape
    rd = nh * hd
    names = ("mu", "w0", "w2a2", "a0", "g2", "k_k", "k_a", "e")
    full = lambda s: pl.BlockSpec(s, lambda: (0,) * len(s))
    vecs = pl.pallas_call(
        functools.partial(_rwkv_prep_body, rd, lora),
        in_specs=[full((n, pw)), full((n, pw))] + [full(rw[k].shape) for k in names],
        out_specs=[full((n, rd))] * 4 + [full((rd, n))] * 6,
        out_shape=[jax.ShapeDtypeStruct((n, rd), F32)] * 4 + [jax.ShapeDtypeStruct((rd, n), F32)] * 6,
        name="rwkv_prep",
    )(prw, shift, *[rw[k] for k in names])
    r, k, v, g = vecs[:4]
    vspec = pl.BlockSpec((hb, hd, n), lambda i: (i, 0, 0))
    sspec = pl.BlockSpec((hb, hd, hd, n), lambda i: (i, 0, 0, 0))
    s_new, o = pl.pallas_call(
        _rwkv_step_body,
        grid=(nh // hb,),
        in_specs=[pl.BlockSpec((None, hb, hd, hd, n), lambda i: (layer, i, 0, 0, 0))] + [vspec] * 6,
        out_specs=[sspec, vspec],
        out_shape=[jax.ShapeDtypeStruct(state_t.shape[1:], F32), jax.ShapeDtypeStruct((nh, hd, n), F32)],
        compiler_params=_cparams("parallel"),
        name="rwkv_step",
    )(state_t, *[x.reshape(nh, hd, n) for x in vecs[4:]])
    pnames = ("r_k", "lnx_g", "lnx_b", "e")
    og = pl.pallas_call(
        functools.partial(_rwkv_post_body, hd),
        in_specs=[full((rd, n))] + [full((n, rd))] * 4 + [full(rw[k].shape) for k in pnames],
        out_specs=full((n, rd)),
        out_shape=jax.ShapeDtypeStruct((n, rd), BF16),
        name="rwkv_post",
    )(o.reshape(rd, n), r, k, v, g, *[rw[k] for k in pnames])
    return og, s_new


def _lanes(x, n):
    return x if n == LANES else jnp.concatenate([x] * (n // LANES), axis=1)


def _attn_prompt_body(kl, tk, rg, q_ref, k_ref, o_ref, m_ref, l_ref, acc_ref, s_ref):
    i = pl.program_id(1)
    nh, tq, dk = q_ref.shape[1:]
    rows = nh * tq
    q = q_ref[0].reshape(rows, dk)
    groups = [slice(g * rg, (g + 1) * rg) for g in range(rows // rg)]
    m_ref[...] = jnp.full_like(m_ref, -jnp.inf)
    l_ref[...] = jnp.zeros_like(l_ref)
    acc_ref[...] = jnp.zeros_like(acc_ref)

    def keys(j):
        return k_ref[0, pl.ds(pl.multiple_of(j * tk, tk), tk), :]

    def scores(r, k):
        return lax.dot_general(q[r], k, (((1,), (1,)), ((), ())), preferred_element_type=F32)

    def update(r, s, v):
        m_old = m_ref[r]
        m_new = jnp.maximum(m_old, jnp.max(s, axis=-1, keepdims=True))
        alpha = jnp.exp2(m_old - m_new)
        p = jnp.exp2(s - _lanes(m_new, tk))
        l_ref[r] = alpha * l_ref[r] + jnp.sum(p, axis=-1, keepdims=True)
        acc_ref[r] = _lanes(alpha, kl) * acc_ref[r] + jnp.dot(p.astype(BF16), v, preferred_element_type=F32)
        m_ref[r] = m_new

    k0 = keys(0)
    for r in groups:
        s_ref[r] = scores(r, k0)

    def body(j, carry):
        v = keys(j)[:, :kl]
        k_next = keys(j + 1)
        for r in groups:
            s = s_ref[r]
            s_ref[r] = scores(r, k_next)
            update(r, s, v)
        return carry

    last = (i * tq) // tk
    lax.fori_loop(0, last, body, 0)
    v = keys(last)[:, :kl]
    for g, r in enumerate(groups):
        s = s_ref[r]
        qpos = i * tq + (g * rg + lax.broadcasted_iota(jnp.int32, s.shape, 0)) % tq
        kpos = last * tk + lax.broadcasted_iota(jnp.int32, s.shape, 1)
        update(r, jnp.where(kpos <= qpos, s, -jnp.inf), v)
    o_ref[0] = (acc_ref[...] / _lanes(l_ref[...], kl)).reshape(nh, tq, kl).astype(o_ref.dtype)


def _attn_prompt(q, kbf, kl):
    b, nh, t, dk = q.shape
    tq, tk = _tile(t, ATT_TQ), _tile(t, ATT_TK)
    rows = nh * tq
    rg = _tile(rows, ATT_ROW_GROUP)
    assert tk % tq == 0 and rg % tq == 0
    return pl.pallas_call(
        functools.partial(_attn_prompt_body, kl, tk, rg),
        grid=(b, t // tq),
        in_specs=[pl.BlockSpec((1, nh, tq, dk), lambda bi, i: (bi, 0, i, 0)),
                  pl.BlockSpec((1, t, dk), lambda bi, i: (bi, 0, 0))],
        out_specs=pl.BlockSpec((1, nh, tq, kl), lambda bi, i: (bi, 0, i, 0)),
        out_shape=jax.ShapeDtypeStruct((b, nh, t, kl), BF16),
        scratch_shapes=[pltpu.VMEM((rows, LANES), F32), pltpu.VMEM((rows, LANES), F32), pltpu.VMEM((rows, kl), F32),
                        pltpu.VMEM((rows, tk), F32)],
        compiler_params=_cparams("parallel", "arbitrary"),
        name="attn_prompt",
    )(q, kbf)


def _attn_sample_body(kl, npg, ngrp, layer, nreq, pt_ref, q_ref, kself_ref, cache_ref, o_ref, buf_ref, sem_ref):
    b = pl.program_id(0)
    nslot = buf_ref.shape[0]
    ahead = nslot - 1

    def copies(req, grp, slot):
        return [pltpu.make_async_copy(cache_ref.at[layer, pt_ref[req, grp * npg + n]], buf_ref.at[slot, n],
                                      sem_ref.at[slot]) for n in range(npg)]

    def slot_of(grp):
        return grp % nslot if ngrp % nslot == 0 else (b * ngrp + grp) % nslot

    def start(grp):
        off, g = divmod(grp, ngrp)

        @pl.when(b + off < nreq)
        def _():
            for c in copies(b + off, g, slot_of(grp)):
                c.start()

    @pl.when(b == 0)
    def _():
        for grp in range(ahead):
            start(grp)

    q = q_ref[0]
    ks = kself_ref[0]
    m = jnp.sum(q.astype(F32) * ks.astype(F32), axis=-1, keepdims=True)
    l = jnp.ones_like(m)
    acc = jnp.broadcast_to(ks[:, :kl].astype(F32), (q.shape[0], kl))
    for g in range(ngrp):
        slot = slot_of(g)
        start(g + ahead)
        for c in copies(b, g, slot):
            c.wait()
        pair = 2 if npg % 2 == 0 else 1
        kts = [jnp.concatenate([buf_ref[slot, n + e] for e in range(pair)], axis=1).astype(BF16)
               for n in range(0, npg, pair)]
        s = jnp.concatenate([jnp.dot(q, kt, preferred_element_type=F32) for kt in kts], axis=1)
        m_new = jnp.maximum(m, jnp.max(s, axis=-1, keepdims=True))
        alpha = jnp.exp2(m - m_new)
        p = jnp.exp2(s - m_new)
        l = alpha * l + jnp.sum(p, axis=-1, keepdims=True)
        pb = p.astype(BF16)
        w = kts[0].shape[1]
        pv = _mm_nt(pb[:, :w], kts[0][:kl, :])
        for n in range(1, len(kts)):
            pv = pv + _mm_nt(pb[:, n * w:(n + 1) * w], kts[n][:kl, :])
        acc = alpha * acc + pv
        m = m_new
    o_ref[0] = (acc / l).astype(o_ref.dtype)


def _attn_sample(q, kself, cache_t, layer, page_table, kl):
    n, nh, dk = q.shape
    ps = cache_t.shape[3]
    npages = page_table.shape[1]
    npg = _tile(npages, PAGES_PER_GROUP)
    grid_spec = pltpu.PrefetchScalarGridSpec(
        num_scalar_prefetch=1,
        grid=(n,),
        in_specs=[pl.BlockSpec((1, nh, dk), lambda bi, pt: (bi, 0, 0)),
                  pl.BlockSpec((1, 1, dk), lambda bi, pt: (bi, 0, 0)),
                  pl.BlockSpec(memory_space=pl.ANY)],
        out_specs=pl.BlockSpec((1, nh, kl), lambda bi, pt: (bi, 0, 0)),
        scratch_shapes=[pltpu.VMEM((PAGE_SLOTS, npg, dk, ps), F32), pltpu.SemaphoreType.DMA((PAGE_SLOTS,))],
    )
    return pl.pallas_call(
        functools.partial(_attn_sample_body, kl, npg, npages // npg, layer, n),
        grid_spec=grid_spec,
        out_shape=jax.ShapeDtypeStruct((n, nh, kl), BF16),
        compiler_params=_cparams("arbitrary"),
        name="attn_sample",
    )(page_table, q, kself, cache_t)


def _memkv_body(m_ref, g_ref, w_ref, o_ref):
    o_ref[...] = jnp.dot(_rms(m_ref[...], g_ref[...]).astype(BF16), w_ref[...], preferred_element_type=F32)


def _memkv(mem, g, wkv):
    n, d = mem.shape
    full = lambda s: pl.BlockSpec(s, lambda: (0,) * len(s))
    return pl.pallas_call(
        _memkv_body,
        in_specs=[full((n, d)), full((1, d)), full(wkv.shape)],
        out_specs=full((n, wkv.shape[1])),
        out_shape=jax.ShapeDtypeStruct((n, wkv.shape[1]), F32),
        name="mem_kv",
    )(mem, g, wkv)


def _memattn_prompt_body(nh, scale, q_ref, k_ref, v_ref, o_ref):
    hd = q_ref.shape[1] // nh
    q, k, v = q_ref[...], k_ref[0], v_ref[0]
    heads = [slice(h * hd, (h + 1) * hd) for h in range(nh)]
    ss = [_mm_nt(q[:, sl], k[:, sl]) * scale for sl in heads]
    ps = [jnp.exp(s - jnp.max(s, axis=-1, keepdims=True)) for s in ss]
    ps = [p / jnp.sum(p, axis=-1, keepdims=True) for p in ps]
    o_ref[...] = jnp.concatenate([_mm(p, v[:, sl]) for p, sl in zip(ps, heads)], axis=1).astype(o_ref.dtype)


def _memattn_prompt(q, mk, mv, nh, scale, tm):
    n, md = q.shape
    b, m, _ = mk.shape
    nb = (n // b) // tm
    kv = pl.BlockSpec((1, m, md), lambda i: (i // nb, 0, 0))
    return pl.pallas_call(
        functools.partial(_memattn_prompt_body, nh, scale),
        grid=(n // tm,),
        in_specs=[pl.BlockSpec((tm, md), lambda i: (i, 0)), kv, kv],
        out_specs=pl.BlockSpec((tm, md), lambda i: (i, 0)),
        out_shape=jax.ShapeDtypeStruct((n, md), BF16),
        compiler_params=_cparams("parallel"),
        name="memattn_prompt",
    )(q, mk, mv)


def _memattn_sample_body(scale, q_ref, k_ref, v_ref, o_ref):
    gr, rows, _ = k_ref.shape
    nh = q_ref.shape[1]
    own = (lax.broadcasted_iota(jnp.int32, (nh, rows), 1) % nh) == lax.broadcasted_iota(jnp.int32, (nh, rows), 0)
    ss = [jnp.where(own, _mm_nt(q_ref[g], k_ref[g]) * scale, -jnp.inf) for g in range(gr)]
    ps = [jnp.exp(s - jnp.max(s, axis=-1, keepdims=True)) for s in ss]
    ps = [p / jnp.sum(p, axis=-1, keepdims=True) for p in ps]
    for g in range(gr):
        o_ref[g] = _mm(ps[g], v_ref[g]).astype(o_ref.dtype)


def _memattn_sample(q, cache_k, cache_v, layer, scale, gr):
    n, md = q.shape
    depth, _, m, nh, hd = cache_k.shape
    kv = pl.BlockSpec((None, gr, m * nh, hd), lambda i: (layer, i, 0, 0))
    qs = pl.BlockSpec((gr, nh, hd), lambda i: (i, 0, 0))
    return pl.pallas_call(
        functools.partial(_memattn_sample_body, scale),
        grid=(n // gr,),
        in_specs=[qs, kv, kv],
        out_specs=qs,
        out_shape=jax.ShapeDtypeStruct((n, nh, hd), BF16),
        compiler_params=_cparams("parallel"),
        name="memattn_sample",
    )(q.reshape(n, nh, hd), cache_k.reshape(depth, n, m * nh, hd), cache_v.reshape(depth, n, m * nh, hd)).reshape(n, md)


def _merge_body(nh, h_ref, og_ref, ctx_ref, om_ref, gates_ref, wo_ref, wuv_ref, mwo_ref, memwo_ref, wout_ref, post_ref,
                o_ref):
    d = h_ref.shape[1]
    o_rwkv = jnp.dot(og_ref[...], wo_ref[...], preferred_element_type=F32)
    vs = []
    for pr in range(nh // 2):
        vp = (jnp.dot(ctx_ref[0, 2 * pr], wuv_ref[2 * pr], preferred_element_type=F32)
              + jnp.dot(ctx_ref[0, 2 * pr + 1], wuv_ref[2 * pr + 1], preferred_element_type=F32))
        vs.append(vp.astype(BF16))
    o_mla = jnp.dot(jnp.concatenate(vs, axis=1), mwo_ref[...], preferred_element_type=F32)
    o_mem = jnp.dot(om_ref[...], memwo_ref[...], preferred_element_type=F32)
    merged = gates_ref[:, 0:d] * o_rwkv + gates_ref[:, d:2 * d] * o_mla + gates_ref[:, 2 * d:3 * d] * o_mem
    y = jnp.dot(merged.astype(BF16), wout_ref[...], preferred_element_type=F32)
    o_ref[...] = h_ref[...] + _rms(y, post_ref[...])


def _merge(h, nbatch, og, ctx, om, gates, wo, wuv, mwo, memwo, wout, post, nh, tm):
    n, d = h.shape
    nb = (n // nbatch) // tm
    row = lambda i: (i, 0)
    kl = ctx.shape[-1]
    return pl.pallas_call(
        functools.partial(_merge_body, nh),
        grid=(n // tm,),
        in_specs=[pl.BlockSpec((tm, d), row), pl.BlockSpec((tm, og.shape[1]), row),
                  pl.BlockSpec((1, nh, tm, kl), lambda i: (i // nb, 0, i % nb, 0)),
                  pl.BlockSpec((tm, om.shape[1]), row), pl.BlockSpec((tm, 3 * d), row),
                  _resident(wo.shape), _resident(wuv.shape), _resident(mwo.shape), _resident(memwo.shape),
                  _resident(wout.shape), _resident((1, d))],
        out_specs=pl.BlockSpec((tm, d), row),
        out_shape=jax.ShapeDtypeStruct((n, d), F32),
        compiler_params=_cparams("parallel"),
        name="merge",
    )(h, og, ctx, om, gates, wo, wuv, mwo, memwo, wout, post)


def _rope_tables(pos, rope, nh):
    half = rope // 2
    freqs = ROPE_BASE ** (-jnp.arange(half, dtype=F32) / half)
    ang = pos.astype(F32)[:, None] * freqs
    cos, sin = jnp.cos(ang), jnp.sin(ang)
    return jnp.tile(jnp.concatenate([cos, cos], axis=1), (1, nh)), jnp.tile(jnp.concatenate([-sin, sin], axis=1), (1, nh))


def _prep_weights(W, d):
    nh, hd = W["rwkv_r_k"].shape
    rd = nh * hd
    lora = W["rwkv_w2"].shape[0]
    glora = W["rwkv_g2"].shape[0]
    rp = 3 * rd + 2 * lora + glora
    ql = W["mla_q_norm"].shape[0]
    kl, mh, vh = W["mla_w_uv"].shape
    nope = W["mla_w_uk"].shape[2]
    rope = W["mla_w_qb"].shape[1] // mh - nope
    md = W["mem_w_k"].shape[1]
    half = rope // 2
    row = lambda x: x.reshape(1, -1)
    w_in = W["w_in"]
    o_cq, o_kv, o_pe, o_mem, o_g = rp, rp + ql, rp + ql + kl, rp + ql + kl + rope, rp + ql + kl + rope + md
    cols = [w_in[:, :o_pe], w_in[:, o_mem:], w_in[:, o_pe:o_mem],
            w_in[:, o_pe + half:o_mem], w_in[:, o_pe:o_pe + half]]
    width = sum(c.shape[1] for c in cols)
    pad = (-width) % LANES
    win = jnp.concatenate(cols + [jnp.zeros((d, pad), F32)], axis=1).astype(BF16)
    wqb = W["mla_w_qb"].reshape(ql, mh, nope + rope)
    wqb = jnp.concatenate([wqb[:, :, :nope].reshape(ql, mh * nope),
                           wqb[:, :, nope:].reshape(ql, mh * rope),
                           jnp.concatenate([wqb[:, :, nope + half:], wqb[:, :, nope:nope + half]], axis=2).reshape(ql, mh * rope)],
                          axis=1).astype(BF16)
    ukt = jnp.transpose(W["mla_w_uk"], (1, 2, 0))
    z = jnp.zeros_like(ukt[0])
    wuk = jnp.stack([jnp.concatenate([jnp.concatenate([ukt[2 * p], z], axis=1),
                                      jnp.concatenate([z, ukt[2 * p + 1]], axis=1)], axis=0)
                     for p in range(mh // 2)]).astype(BF16)
    uv = jnp.transpose(W["mla_w_uv"], (1, 0, 2))
    zv = jnp.zeros_like(uv[0])
    wuv = jnp.stack([jnp.concatenate([uv[h], zv] if h % 2 == 0 else [zv, uv[h]], axis=1)
                     for h in range(mh)]).astype(BF16)
    zl = jnp.zeros((lora, rd), F32)
    w2a2 = jnp.concatenate([jnp.concatenate([W["rwkv_w2"], zl], axis=1),
                            jnp.concatenate([zl, W["rwkv_a2"]], axis=1)], axis=0).astype(BF16)
    hid = jnp.arange(rd) // hd
    rw = dict(mu=row(W["rwkv_mu"]), w0=row(W["rwkv_w0"]), w2a2=w2a2, a0=row(W["rwkv_a0"]), g2=W["rwkv_g2"].astype(BF16),
              k_k=row(W["rwkv_k_k"]), k_a=row(W["rwkv_k_a"]), r_k=row(W["rwkv_r_k"]), lnx_g=row(W["rwkv_lnx_g"]),
              lnx_b=row(W["rwkv_lnx_b"]), e=(hid[:, None] == hid[None, :]).astype(BF16))
    dims = (rp, ql, kl, rope, md, 3 * d, mh, nope)
    return dict(
        dims=dims, nh=nh, hd=hd, lora=lora, rw=rw, win=win, wqb=wqb, wuk=wuk, wuv=wuv,
        ffn1=(row(W["ffn1_pre"]), row(W["ffn1_post"]), W["ffn1_gate"].astype(BF16), W["ffn1_up"].astype(BF16),
              W["ffn1_down"].astype(BF16)),
        ffn2=(row(W["ffn2_pre"]), row(W["ffn2_post"]), W["ffn2_gate"].astype(BF16), W["ffn2_up"].astype(BF16),
              W["ffn2_down"].astype(BF16)),
        mix_pre=row(W["mix_pre"]), mix_post=row(W["mix_post"]), q_norm=row(W["mla_q_norm"]), kv_norm=row(W["mla_kv_norm"]),
        mem_norm=row(W["mem_norm"]), mem_wkv=jnp.concatenate([W["mem_w_k"], W["mem_w_v"]], axis=1).astype(BF16),
        rwkv_wo=W["rwkv_w_o"].astype(BF16), mla_wo=W["mla_w_o"].astype(BF16), mem_wo=W["mem_w_o"].astype(BF16),
        w_out=W["w_out"].astype(BF16), qscale=float(nope + rope) ** -0.5 * LOG2E,
    )


def _tile(n, pref):
    t = min(pref, n)
    assert n % t == 0, (n, t)
    return t


def kernel(x_prompt, x_sample, cache_mla, state_rwkv, state_shift, cache_mem_k, cache_mem_v, page_table, mem_prompt, ffn1_pre, ffn1_post, ffn1_gate, ffn1_up, ffn1_down, mix_pre, mix_post, w_in, rwkv_mu, rwkv_w0, rwkv_w2, rwkv_a0, rwkv_a2, rwkv_g2, rwkv_k_k, rwkv_k_a, rwkv_r_k, rwkv_lnx_g, rwkv_lnx_b, rwkv_w_o, mla_q_norm, mla_w_qb, mla_kv_norm, mla_w_uk, mla_w_uv, mla_w_o, mem_norm, mem_w_k, mem_w_v, mem_w_o, w_out, ffn2_pre, ffn2_post, ffn2_gate, ffn2_up, ffn2_down):
    names = ("ffn1_pre", "ffn1_post", "ffn1_gate", "ffn1_up", "ffn1_down", "mix_pre", "mix_post", "w_in",
             "rwkv_mu", "rwkv_w0", "rwkv_w2", "rwkv_a0", "rwkv_a2", "rwkv_g2", "rwkv_k_k", "rwkv_k_a", "rwkv_r_k",
             "rwkv_lnx_g", "rwkv_lnx_b", "rwkv_w_o", "mla_q_norm", "mla_w_qb", "mla_kv_norm", "mla_w_uk", "mla_w_uv",
             "mla_w_o", "mem_norm", "mem_w_k", "mem_w_v", "mem_w_o", "w_out", "ffn2_pre", "ffn2_post", "ffn2_gate",
             "ffn2_up", "ffn2_down")
    stacked = (ffn1_pre, ffn1_post, ffn1_gate, ffn1_up, ffn1_down, mix_pre, mix_post, w_in,
               rwkv_mu, rwkv_w0, rwkv_w2, rwkv_a0, rwkv_a2, rwkv_g2, rwkv_k_k, rwkv_k_a, rwkv_r_k,
               rwkv_lnx_g, rwkv_lnx_b, rwkv_w_o, mla_q_norm, mla_w_qb, mla_kv_norm, mla_w_uk, mla_w_uv,
               mla_w_o, mem_norm, mem_w_k, mem_w_v, mem_w_o, w_out, ffn2_pre, ffn2_post, ffn2_gate,
               ffn2_up, ffn2_down)
    B, S, D = x_prompt.shape
    DB, T, _ = x_sample.shape
    assert T == 1, "decode groups carry one new token per request"
    depth = ffn1_pre.shape[0]
    page = cache_mla.shape[2]
    past_len = page_table.shape[1] * page
    mem_tokens, mem_heads, mem_hd = cache_mem_k.shape[2:]
    mem_scale = float(mem_hd) ** -0.5

    cache_t = jnp.swapaxes(cache_mla, 2, 3)
    state_t = jnp.transpose(state_rwkv, (0, 2, 3, 4, 1))
    xp = x_prompt.reshape(B * S, D)
    xs = x_sample.reshape(DB, D)
    outs = [[] for _ in range(8)]
    for l in range(depth):
        P = _prep_weights({n: w[l] for n, w in zip(names, stacked)}, D)
        rp, ql, kl, rope, md, gd, mh, nope = P["dims"]
        nh, hd, lora, rw = P["nh"], P["hd"], P["lora"], P["rw"]
        cos_p, sin_p = _rope_tables(jnp.arange(S), rope, mh)
        cos_s, sin_s = _rope_tables(jnp.full((DB,), past_len), rope, mh)
        tm_p, tm_w, tm_s = _tile(S, ROWS_PER_STEP), _tile(S, ROWS_PER_STEP_WIDE), DB

        h = _ffn(xs, *P["ffn1"], tm_s)
        prw_s, q, rows_s, kbf, qmem, gates = _inproj(h, 1, P["mix_pre"], P["win"], P["q_norm"], P["wqb"], P["wuk"],
                                                     P["kv_norm"], cos_s, sin_s, P["dims"], P["qscale"], tm_s)
        og, wkv_t = _rwkv_sample(prw_s, state_shift[l], state_t, l, rw, nh, hd, lora, _tile(nh, 2))
        wkv_s = jnp.transpose(wkv_t, (3, 0, 1, 2))
        ctx = _attn_sample(jnp.swapaxes(q[0], 0, 1), kbf.reshape(DB, 1, kl + rope), cache_t, l, page_table, kl)
        om = _memattn_sample(qmem, cache_mem_k, cache_mem_v, l, mem_scale, _tile(DB, 4))
        h = _merge(h, 1, og, jnp.swapaxes(ctx, 0, 1)[None], om, gates, P["rwkv_wo"], P["wuv"], P["mla_wo"], P["mem_wo"],
                   P["w_out"], P["mix_post"], mh, tm_s)
        xs = _ffn(h, *P["ffn2"], tm_s)

        mkv = _memkv(mem_prompt.reshape(B * mem_tokens, D), P["mem_norm"], P["mem_wkv"])
        mk_p, mv_p = mkv[:, :md].reshape(B, mem_tokens, md), mkv[:, md:].reshape(B, mem_tokens, md)
        h = _ffn(xp, *P["ffn1"], tm_w)
        prw, q, rows, kbf, qmem, gates = _inproj(h, B, P["mix_pre"], P["win"], P["q_norm"], P["wqb"], P["wuk"],
                                                 P["kv_norm"], cos_p, sin_p, P["dims"], P["qscale"], tm_p)
        og, wkv_p = _rwkv_prompt(prw.reshape(B, S, rp), rw, nh, hd, lora)
        ctx = _attn_prompt(q, kbf.reshape(B, S, kl + rope), kl)
        om = _memattn_prompt(qmem, mk_p, mv_p, mem_heads, mem_scale, tm_w)
        h = _merge(h, B, og.reshape(B * S, nh * hd), ctx, om, gates, P["rwkv_wo"], P["wuv"], P["mla_wo"], P["mem_wo"],
                   P["w_out"], P["mix_post"], mh, tm_w)
        xp = _ffn(h, *P["ffn2"], tm_w)
        rows_p, shift_p = rows.reshape(B, S, kl + rope), prw.reshape(B, S, rp)[:, -1]

        for lst, val in zip(outs, (rows_p, rows_s.reshape(DB, T, kl + rope), wkv_p, wkv_s, shift_p, prw_s,
                                   mk_p.reshape(B, mem_tokens, mem_heads, mem_hd),
                                   mv_p.reshape(B, mem_tokens, mem_heads, mem_hd))):
            lst.append(val)
    return (xp.reshape(B, S, D), xs.reshape(DB, T, D)) + tuple(jnp.stack(o) for o in outs)
```

```python
import functools

import jax
import jax.numpy as jnp
from jax import lax
from jax.experimental import pallas as pl
from jax.experimental.pallas import tpu as pltpu

F32, BF16 = jnp.float32, jnp.bfloat16
RMS_EPS = 1e-6
LNX_EPS = 64e-5
ROPE_BASE = 10000.0
LANES = 128
VMEM_LIMIT = 52 * 1024 * 1024
ROWS_PER_STEP = 256
ROWS_PER_STEP_WIDE = 512
RWKV_CHUNK = 64
RWKV_CHUNKS_PER_STEP = 2
ATT_TQ = 256
ATT_TK = 512
ATT_ROW_GROUP = 512
PAGES_PER_GROUP = 32
PAGE_SLOTS = 4
LOG2E = 1.4426950408889634


def _cparams(*sem):
    return pltpu.CompilerParams(dimension_semantics=sem, vmem_limit_bytes=VMEM_LIMIT)


def _resident(shape):
    nd = len(shape)
    return pl.BlockSpec(shape, lambda *_: (0,) * nd, pipeline_mode=pl.Buffered(1))


def _rms(x, g):
    return x * lax.rsqrt(jnp.mean(x * x, axis=-1, keepdims=True) + RMS_EPS) * g


def _sigmoid(x):
    return 1.0 / (1.0 + jnp.exp(-x))


def _mm(a, b):
    return jnp.dot(a.astype(BF16), b.astype(BF16), preferred_element_type=F32)


def _mm_nt(a, b):
    return lax.dot_general(a.astype(BF16), b.astype(BF16), (((1,), (1,)), ((), ())), preferred_element_type=F32)


def _ffn_body(x_ref, pre_ref, post_ref, wg_ref, wu_ref, wd_ref, o_ref):
    x = x_ref[...]
    h = _rms(x, pre_ref[...]).astype(BF16)
    g = jnp.dot(h, wg_ref[...], preferred_element_type=F32)
    u = jnp.dot(h, wu_ref[...], preferred_element_type=F32)
    act = (g * _sigmoid(g)) * u
    y = jnp.dot(act.astype(BF16), wd_ref[...], preferred_element_type=F32)
    o_ref[...] = x + 0.5 * _rms(y, post_ref[...])


def _ffn(x, pre, post, wg, wu, wd, tm):
    n, d = x.shape
    f = wg.shape[1]
    return pl.pallas_call(
        _ffn_body,
        grid=(n // tm,),
        in_specs=[pl.BlockSpec((tm, d), lambda i: (i, 0)), _resident((1, d)), _resident((1, d)),
                  _resident((d, f)), _resident((d, f)), _resident((f, d))],
        out_specs=pl.BlockSpec((tm, d), lambda i: (i, 0)),
        out_shape=jax.ShapeDtypeStruct((n, d), F32),
        compiler_params=_cparams("parallel"),
        name="ffn",
    )(x, pre, post, wg, wu, wd)


def _inproj_body(dims, qscale, h_ref, pre_ref, win_ref, qn_ref, wqb_ref, wuk_ref, kvn_ref, cos_ref, sin_ref,
                 prw_ref, q_ref, rows_ref, kbf_ref, qmem_ref, gates_ref):
    rp, ql, kl, rope, md, gd, nh, nope = dims
    u = _rms(h_ref[...], pre_ref[...]).astype(BF16)
    p = jnp.dot(u, win_ref[...], preferred_element_type=F32)
    o = 0
    prw_ref[...] = p[:, o:o + rp]; o += rp
    cq = p[:, o:o + ql]; o += ql
    ckv = p[:, o:o + kl]; o += kl
    qmem_ref[...] = p[:, o:o + md].astype(BF16); o += md
    gates_ref[...] = _sigmoid(p[:, o:o + gd]); o += gd
    kpe = p[:, o:o + rope]; o += rope
    kpe_sw = p[:, o:o + rope]
    cos = cos_ref[...]
    sin = sin_ref[...]
    q = jnp.dot(_rms(cq, qn_ref[...]).astype(BF16), wqb_ref[...], preferred_element_type=F32)
    nn = nh * nope
    nr = nh * rope
    qpe = ((q[:, nn:nn + nr] * cos + q[:, nn + nr:nn + 2 * nr] * sin) * qscale).astype(BF16)
    qn = q[:, :nn].astype(BF16)
    for pr in range(nh // 2):
        qlat = (jnp.dot(qn[:, LANES * pr:LANES * (pr + 1)], wuk_ref[pr], preferred_element_type=F32) * qscale).astype(BF16)
        for e in range(2):
            hh = 2 * pr + e
            q_ref[0, hh, :, 0:kl] = qlat[:, kl * e:kl * (e + 1)]
            q_ref[0, hh, :, kl:kl + rope] = qpe[:, rope * hh:rope * (hh + 1)]
    ckvn = _rms(ckv, kvn_ref[...])
    kper = kpe * cos[:, :rope] + kpe_sw * sin[:, :rope]
    rows_ref[:, 0:kl] = ckvn
    rows_ref[:, kl:kl + rope] = kper
    kbf_ref[:, 0:kl] = ckvn.astype(BF16)
    kbf_ref[:, kl:kl + rope] = kper.astype(BF16)


def _inproj(h, nbatch, pre, win, qn, wqb, wuk, kvn, cos, sin, dims, qscale, tm):
    n, d = h.shape
    rp, ql, kl, rope, md, gd, nh, nope = dims
    t = n // nbatch
    nb = t // tm
    cw = win.shape[1]
    row = lambda i: (i, 0)
    tab = lambda i: (i % nb, 0)
    return pl.pallas_call(
        functools.partial(_inproj_body, dims, qscale),
        grid=(n // tm,),
        in_specs=[pl.BlockSpec((tm, d), row), _resident((1, d)), _resident((d, cw)), _resident((1, ql)),
                  _resident(wqb.shape), _resident(wuk.shape), _resident((1, kl)),
                  pl.BlockSpec((tm, nh * rope), tab), pl.BlockSpec((tm, nh * rope), tab)],
        out_specs=[pl.BlockSpec((tm, rp), row),
                   pl.BlockSpec((1, nh, tm, kl + rope), lambda i: (i // nb, 0, i % nb, 0)),
                   pl.BlockSpec((tm, kl + rope), row), pl.BlockSpec((tm, kl + rope), row),
                   pl.BlockSpec((tm, md), row), pl.BlockSpec((tm, gd), row)],
        out_shape=[jax.ShapeDtypeStruct((n, rp), F32),
                   jax.ShapeDtypeStruct((nbatch, nh, t, kl + rope), BF16),
                   jax.ShapeDtypeStruct((n, kl + rope), F32),
                   jax.ShapeDtypeStruct((n, kl + rope), BF16),
                   jax.ShapeDtypeStruct((n, md), BF16),
                   jax.ShapeDtypeStruct((n, gd), F32)],
        compiler_params=_cparams("parallel"),
        name="inproj",
    )(h, pre, win, qn, wqb, wuk, kvn, cos, sin)


def _segsum(x, e):
    hi = x.astype(BF16)
    lo = (x - hi.astype(F32)).astype(BF16)
    return jnp.dot(hi, e, preferred_element_type=F32) + jnp.dot(lo, e, preferred_element_type=F32)


def _rwkv_prep(p, prev, mu, w0, w2a2, a0, g2, k_k, k_a, e, rd, lora):
    ps = p + (prev - p) * mu
    r = ps[:, 0:rd]
    k = ps[:, rd:2 * rd]
    v = ps[:, 2 * rd:3 * rd]
    wa = ps[:, 3 * rd:3 * rd + 2 * lora]
    gl = ps[:, 3 * rd + 2 * lora:]
    lane = lax.broadcasted_iota(jnp.int32, wa.shape, 1)
    wa = jnp.where(lane < lora, jnp.tanh(wa), wa)
    wa2 = _mm(wa, w2a2)
    x = -(w0 + wa2[:, :rd])
    softplus = jnp.maximum(x, 0.0) + jnp.log(1.0 + jnp.exp(-jnp.abs(x)))
    logdec = -jnp.exp(-softplus - 0.5)
    a = _sigmoid(a0 + wa2[:, rd:])
    g = _mm(_sigmoid(gl), g2)
    kk = k * k_k
    kk = kk / jnp.maximum(jnp.sqrt(_segsum(kk * kk, e)), 1e-12)
    k = k * (1.0 + (a - 1.0) * k_a)
    return r, k, v, logdec, -kk, kk * a, g


def _rwkv_post(o, r, k, v, g, r_k, lng, lnb, e, hd):
    mean = _segsum(o, e) * (1.0 / hd)
    oc = o - mean
    var = _segsum(oc * oc, e) * (1.0 / hd)
    o = oc * lax.rsqrt(var + LNX_EPS) * lng + lnb
    bonus = _segsum(r * k * r_k, e) * v
    return (o + bonus) * g


def _pair_rows(y):
    lo = (lax.broadcasted_iota(jnp.int32, y.shape, 1) % LANES) < (LANES // 2)
    z = jnp.zeros_like(y)
    return jnp.concatenate([jnp.where(lo, y, z), jnp.where(lo, z, y)], axis=0)


def _rwkv_chunks_local(items, strict, incl):
    L = items[0][0].shape[0]
    cs = []
    for r, k, v, ld, cum, a, b in items:
        cum_l = cum[L - 1:L, :]
        e_neg = jnp.exp(-cum)
        e_end = jnp.exp(cum_l - cum)
        kt, bt = (k * e_neg).astype(BF16), (b * e_neg).astype(BF16)
        at, rt = a * jnp.exp(cum - ld), (r * jnp.exp(cum)).astype(BF16)
        cs.append(dict(at=at, rt=rt, vb=_pair_rows(v.astype(BF16)), v=v, dl=jnp.exp(cum_l),
                       bkh=jnp.concatenate([b * e_end, k * e_end], axis=0).astype(BF16),
                       lhs=jnp.concatenate([at.astype(BF16), rt], axis=0),
                       rhs=jnp.concatenate([_pair_rows(bt), _pair_rows(kt)], axis=0)))
    for c in cs:
        mm = _mm_nt(c.pop("lhs"), c.pop("rhs"))
        c["pw"] = jnp.where(strict, mm[:L, :2 * L], 0.0)
        c["m_ka"] = jnp.where(strict, mm[:L, 2 * L:], 0.0)
        c["m_r"] = jnp.concatenate([jnp.where(incl, mm[L:, :2 * L], 0.0), jnp.where(incl, mm[L:, 2 * L:], 0.0)],
                                   axis=1).astype(BF16)
    for c in cs:
        c["x"] = jnp.concatenate([c.pop("at"), _mm(c.pop("m_ka"), c["vb"])], axis=1)
    span = 1
    while span < L:
        for c in cs:
            c["x"] = c["x"] + _mm(c["pw"], _pair_rows(c["x"].astype(BF16)))
        span *= 2
        if span < L:
            for c in cs:
                c["pw"] = _mm(c["pw"], _pair_rows(c["pw"].astype(BF16)))
    for c in cs:
        x = c.pop("x")
        c["w1"], c["uloc"] = x[:, :LANES].astype(BF16), x[:, LANES:]
    return cs


def _rwkv_chunks_apply(cs, states, diag):
    sbs = [s.astype(BF16) for s in states]
    urs = [_mm_nt(c["w1"], sb) + c["uloc"] for c, sb in zip(cs, sbs)]
    o1 = [_mm_nt(c["rt"], sb) for c, sb in zip(cs, sbs)]
    upds = [_mm(jnp.concatenate([ur, c["v"]], axis=0).T, c["bkh"]) for c, ur in zip(cs, urs)]
    o2 = [_mm(c["m_r"], jnp.concatenate([_pair_rows(ur.astype(BF16)), c["vb"]], axis=0)) for c, ur in zip(cs, urs)]
    return [(a + b, s * c["dl"] + jnp.where(diag, u, 0.0)) for a, b, s, c, u in zip(o1, o2, states, cs, upds)]


def _rwkv_prompt_body(rd, lora, hd, L, p_ref, mu_ref, w0_ref, w2a2_ref, a0_ref, g2_ref, kk_ref, ka_ref, rk_ref,
                      lng_ref, lnb_ref, e_ref, og_ref, st_ref, prev_ref, s_ref):
    step = pl.program_id(0)
    nb, rows, _ = p_ref.shape
    npair = rd // LANES

    @pl.when(step == 0)
    def _():
        prev_ref[...] = jnp.zeros_like(prev_ref)
        s_ref[...] = jnp.zeros_like(s_ref)

    e = e_ref[...]
    ti = lax.broadcasted_iota(jnp.int32, (L, 2 * L), 0)
    si = lax.broadcasted_iota(jnp.int32, (L, 2 * L), 1) % L
    strict, incl = si < ti, si <= ti
    half = LANES // 2
    diag = ((lax.broadcasted_iota(jnp.int32, (LANES, LANES), 0) < half)
            == (lax.broadcasted_iota(jnp.int32, (LANES, LANES), 1) < half))
    tr = lax.broadcasted_iota(jnp.int32, (rows, rows), 0)
    tc = lax.broadcasted_iota(jnp.int32, (rows, rows), 1)
    tri = ((tc <= tr) & (tc // L == tr // L)).astype(BF16)
    rowi = lax.broadcasted_iota(jnp.int32, (rows, p_ref.shape[2]), 0)

    keys, items, vecs = [], [], []
    for bi in range(nb):
        p = p_ref[bi]
        prev = jnp.where(rowi == 0, prev_ref[bi], pltpu.roll(p, 1, axis=0))
        prev_ref[bi] = p[rows - 1:rows, :]
        r, k, v, ld, a, b, g = _rwkv_prep(p, prev, mu_ref[...], w0_ref[...], w2a2_ref[...], a0_ref[...], g2_ref[...],
                                          kk_ref[...], ka_ref[...], e, rd, lora)
        hi = ld.astype(BF16)
        r1 = ld - hi.astype(F32)
        mid = r1.astype(BF16)
        lo = (r1 - mid.astype(F32)).astype(BF16)
        cum = (jnp.dot(tri, hi, preferred_element_type=F32) + jnp.dot(tri, mid, preferred_element_type=F32)
               + jnp.dot(tri, lo, preferred_element_type=F32))
        vecs.append((r, k, v, g))
        for cc in range(rows // L):
            for pr in range(npair):
                sl = (slice(L * cc, L * (cc + 1)), slice(LANES * pr, LANES * (pr + 1)))
                keys.append((bi, cc, pr))
                items.append((r[sl], k[sl], v[sl], ld[sl], cum[sl], a[sl], b[sl]))
    local = dict(zip(keys, _rwkv_chunks_local(items, strict, incl)))
    chains = [(bi, pr) for bi in range(nb) for pr in range(npair)]
    state = {ch: s_ref[ch[0] * npair + ch[1]] for ch in chains}
    outs = {}
    for cc in range(rows // L):
        new = _rwkv_chunks_apply([local[bi, cc, pr] for bi, pr in chains], [state[ch] for ch in chains], diag)
        for ch, (o, s) in zip(chains, new):
            outs[ch, cc], state[ch] = o, s
    for bi in range(nb):
        for pr in range(npair):
            s_ref[bi * npair + pr] = state[bi, pr]
        o = jnp.concatenate([jnp.concatenate([outs[(bi, pr), cc] for cc in range(rows // L)], axis=0)
                             for pr in range(npair)], axis=1)
        r, k, v, g = vecs[bi]
        og_ref[bi] = _rwkv_post(o, r, k, v, g, rk_ref[...], lng_ref[...], lnb_ref[...], e, hd).astype(BF16)

    @pl.when(step == pl.num_programs(0) - 1)
    def _():
        for bi in range(nb):
            for pr in range(npair):
                s = s_ref[bi * npair + pr]
                st_ref[bi, 2 * pr] = s[:hd, :hd]
                st_ref[bi, 2 * pr + 1] = s[hd:, hd:]


def _rwkv_prompt(prw, rw, nh, hd, lora):
    b, t, pw = prw.shape
    rd = nh * hd
    rows = _tile(t, RWKV_CHUNK * RWKV_CHUNKS_PER_STEP)
    names = ("mu", "w0", "w2a2", "a0", "g2", "k_k", "k_a", "r_k", "lnx_g", "lnx_b", "e")
    return pl.pallas_call(
        functools.partial(_rwkv_prompt_body, rd, lora, hd, RWKV_CHUNK),
        grid=(t // rows,),
        in_specs=[pl.BlockSpec((b, rows, pw), lambda c: (0, c, 0))] + [_resident(rw[k].shape) for k in names],
        out_specs=[pl.BlockSpec((b, rows, rd), lambda c: (0, c, 0)),
                   pl.BlockSpec((b, nh, hd, hd), lambda c: (0, 0, 0, 0))],
        out_shape=[jax.ShapeDtypeStruct((b, t, rd), BF16), jax.ShapeDtypeStruct((b, nh, hd, hd), F32)],
        scratch_shapes=[pltpu.VMEM((b, 1, pw), F32), pltpu.VMEM((b * (rd // LANES), LANES, LANES), F32)],
        compiler_params=_cparams("arbitrary"),
        name="rwkv_prompt",
    )(prw, *[rw[k] for k in names])


def _rwkv_prep_body(rd, lora, p_ref, prev_ref, mu_ref, w0_ref, w2a2_ref, a0_ref, g2_ref, kk_ref, ka_ref, e_ref,
                    r_ref, k_ref, v_ref, g_ref, *t_refs):
    r, k, v, ld, a, b, g = _rwkv_prep(p_ref[...], prev_ref[...], mu_ref[...], w0_ref[...], w2a2_ref[...], a0_ref[...],
                                      g2_ref[...], kk_ref[...], ka_ref[...], e_ref[...], rd, lora)
    r_ref[...] = r
    k_ref[...] = k
    v_ref[...] = v
    g_ref[...] = g
    for ref, x in zip(t_refs, (r, k, v, jnp.exp(ld), a, b)):
        ref[...] = x.T


def _rwkv_step_body(s_ref, r_ref, k_ref, v_ref, w_ref, a_ref, b_ref, so_ref, o_ref):
    for h in range(s_ref.shape[0]):
        s = s_ref[h]
        sa = jnp.sum(s * a_ref[h][None], axis=1)
        s = s * w_ref[h][None] + sa[:, None, :] * b_ref[h][None] + v_ref[h][:, None, :] * k_ref[h][None]
        so_ref[h] = s
        o_ref[h] = jnp.sum(s * r_ref[h][None], axis=1)


def _rwkv_post_body(hd, o_ref, r_ref, k_ref, v_ref, g_ref, rk_ref, lng_ref, lnb_ref, e_ref, og_ref):
    og_ref[...] = _rwkv_post(o_ref[...].T, r_ref[...], k_ref[...], v_ref[...], g_ref[...], rk_ref[...], lng_ref[...],
                             lnb_ref[...], e_ref[...], hd).astype(BF16)


def _rwkv_sample(prw, shift, state_t, layer, rw, nh, hd, lora, hb):
    n, pw = prw.shape
    rd = nh * hd
    names = ("mu", "w0", "w2a2", "a0", "g2", "k_k", "k_a", "e")
    full = lambda s: pl.BlockSpec(s, lambda: (0,) * len(s))
    vecs = pl.pallas_call(
        functools.partial(_rwkv_prep_body, rd, lora),
        in_specs=[full((n, pw)), full((n, pw))] + [full(rw[k].shape) for k in names],
        out_specs=[full((n, rd))] * 4 + [full((rd, n))] * 6,
        out_shape=[jax.ShapeDtypeStruct((n, rd), F32)] * 4 + [jax.ShapeDtypeStruct((rd, n), F32)] * 6,
        name="rwkv_prep",
    )(prw, shift, *[rw[k] for k in names])
    r, k, v, g = vecs[:4]
    vspec = pl.BlockSpec((hb, hd, n), lambda i: (i, 0, 0))
    sspec = pl.BlockSpec((hb, hd, hd, n), lambda i: (i, 0, 0, 0))
    s_new, o = pl.pallas_call(
        _rwkv_step_body,
        grid=(nh // hb,),
        in_specs=[pl.BlockSpec((None, hb, hd, hd, n), lambda i: (layer, i, 0, 0, 0))] + [vspec] * 6,
        out_specs=[sspec, vspec],
        out_shape=[jax.ShapeDtypeStruct(state_t.shape[1:], F32), jax.ShapeDtypeStruct((nh, hd, n), F32)],
        compiler_params=_cparams("parallel"),
        name="rwkv_step",
    )(state_t, *[x.reshape(nh, hd, n) for x in vecs[4:]])
    pnames = ("r_k", "lnx_g", "lnx_b", "e")
    og = pl.pallas_call(
        functools.partial(_rwkv_post_body, hd),
        in_specs=[full((rd, n))] + [full((n, rd))] * 4 + [full(rw[k].shape) for k in pnames],
        out_specs=full((n, rd)),
        out_shape=jax.ShapeDtypeStruct((n, rd), BF16),
        name="rwkv_post",
    )(o.reshape(rd, n), r, k, v, g, *[rw[k] for k in pnames])
    return og, s_new


def _lanes(x, n):
    return x if n == LANES else jnp.concatenate([x] * (n // LANES), axis=1)


def _attn_prompt_body(kl, tk, rg, q_ref, k_ref, o_ref, m_ref, l_ref, acc_ref, s_ref):
    i = pl.program_id(1)
    nh, tq, dk = q_ref.shape[1:]
    rows = nh * tq
    q = q_ref[0].reshape(rows, dk)
    groups = [slice(g * rg, (g + 1) * rg) for g in range(rows // rg)]
    m_ref[...] = jnp.full_like(m_ref, -jnp.inf)
    l_ref[...] = jnp.zeros_like(l_ref)
    acc_ref[...] = jnp.zeros_like(acc_ref)

    def keys(j):
        return k_ref[0, pl.ds(pl.multiple_of(j * tk, tk), tk), :]

    def scores(r, k):
        return lax.dot_general(q[r], k, (((1,), (1,)), ((), ())), preferred_element_type=F32)

    def update(r, s, v):
        m_old = m_ref[r]
        m_new = jnp.maximum(m_old, jnp.max(s, axis=-1, keepdims=True))
        alpha = jnp.exp2(m_old - m_new)
        p = jnp.exp2(s - _lanes(m_new, tk))
        l_ref[r] = alpha * l_ref[r] + jnp.sum(p, axis=-1, keepdims=True)
        acc_ref[r] = _lanes(alpha, kl) * acc_ref[r] + jnp.dot(p.astype(BF16), v, preferred_element_type=F32)
        m_ref[r] = m_new

    k0 = keys(0)
    for r in groups:
        s_ref[r] = scores(r, k0)

    def body(j, carry):
        v = keys(j)[:, :kl]
        k_next = keys(j + 1)
        for r in groups:
            s = s_ref[r]
            s_ref[r] = scores(r, k_next)
            update(r, s, v)
        return carry

    last = (i * tq) // tk
    lax.fori_loop(0, last, body, 0)
    v = keys(last)[:, :kl]
    for g, r in enumerate(groups):
        s = s_ref[r]
        qpos = i * tq + (g * rg + lax.broadcasted_iota(jnp.int32, s.shape, 0)) % tq
        kpos = last * tk + lax.broadcasted_iota(jnp.int32, s.shape, 1)
        update(r, jnp.where(kpos <= qpos, s, -jnp.inf), v)
    o_ref[0] = (acc_ref[...] / _lanes(l_ref[...], kl)).reshape(nh, tq, kl).astype(o_ref.dtype)


def _attn_prompt(q, kbf, kl):
    b, nh, t, dk = q.shape
    tq, tk = _tile(t, ATT_TQ), _tile(t, ATT_TK)
    rows = nh * tq
    rg = _tile(rows, ATT_ROW_GROUP)
    assert tk % tq == 0 and rg % tq == 0
    return pl.pallas_call(
        functools.partial(_attn_prompt_body, kl, tk, rg),
        grid=(b, t // tq),
        in_specs=[pl.BlockSpec((1, nh, tq, dk), lambda bi, i: (bi, 0, i, 0)),
                  pl.BlockSpec((1, t, dk), lambda bi, i: (bi, 0, 0))],
        out_specs=pl.BlockSpec((1, nh, tq, kl), lambda bi, i: (bi, 0, i, 0)),
        out_shape=jax.ShapeDtypeStruct((b, nh, t, kl), BF16),
        scratch_shapes=[pltpu.VMEM((rows, LANES), F32), pltpu.VMEM((rows, LANES), F32), pltpu.VMEM((rows, kl), F32),
                        pltpu.VMEM((rows, tk), F32)],
        compiler_params=_cparams("parallel", "arbitrary"),
        name="attn_prompt",
    )(q, kbf)


def _attn_sample_body(kl, npg, ngrp, layer, nreq, pt_ref, q_ref, kself_ref, cache_ref, o_ref, buf_ref, sem_ref):
    b = pl.program_id(0)
    nslot = buf_ref.shape[0]
    ahead = nslot - 1

    def copies(req, grp, slot):
        return [pltpu.make_async_copy(cache_ref.at[layer, pt_ref[req, grp * npg + n]], buf_ref.at[slot, n],
                                      sem_ref.at[slot]) for n in range(npg)]

    def slot_of(grp):
        return grp % nslot if ngrp % nslot == 0 else (b * ngrp + grp) % nslot

    def start(grp):
        off, g = divmod(grp, ngrp)

        @pl.when(b + off < nreq)
        def _():
            for c in copies(b + off, g, slot_of(grp)):
                c.start()

    @pl.when(b == 0)
    def _():
        for grp in range(ahead):
            start(grp)

    q = q_ref[0]
    ks = kself_ref[0]
    m = jnp.sum(q.astype(F32) * ks.astype(F32), axis=-1, keepdims=True)
    l = jnp.ones_like(m)
    acc = jnp.broadcast_to(ks[:, :kl].astype(F32), (q.shape[0], kl))
    for g in range(ngrp):
        slot = slot_of(g)
        start(g + ahead)
        for c in copies(b, g, slot):
            c.wait()
        pair = 2 if npg % 2 == 0 else 1
        kts = [jnp.concatenate([buf_ref[slot, n + e] for e in range(pair)], axis=1).astype(BF16)
               for n in range(0, npg, pair)]
        s = jnp.concatenate([jnp.dot(q, kt, preferred_element_type=F32) for kt in kts], axis=1)
        m_new = jnp.maximum(m, jnp.max(s, axis=-1, keepdims=True))
        alpha = jnp.exp2(m - m_new)
        p = jnp.exp2(s - m_new)
        l = alpha * l + jnp.sum(p, axis=-1, keepdims=True)
        pb = p.astype(BF16)
        w = kts[0].shape[1]
        pv = _mm_nt(pb[:, :w], kts[0][:kl, :])
        for n in range(1, len(kts)):
            pv = pv + _mm_nt(pb[:, n * w:(n + 1) * w], kts[n][:kl, :])
        acc = alpha * acc + pv
        m = m_new
    o_ref[0] = (acc / l).astype(o_ref.dtype)


def _attn_sample(q, kself, cache_t, layer, page_table, kl):
    n, nh, dk = q.shape
    ps = cache_t.shape[3]
    npages = page_table.shape[1]
    npg = _tile(npages, PAGES_PER_GROUP)
    grid_spec = pltpu.PrefetchScalarGridSpec(
        num_scalar_prefetch=1,
        grid=(n,),
        in_specs=[pl.BlockSpec((1, nh, dk), lambda bi, pt: (bi, 0, 0)),
                  pl.BlockSpec((1, 1, dk), lambda bi, pt: (bi, 0, 0)),
                  pl.BlockSpec(memory_space=pl.ANY)],
        out_specs=pl.BlockSpec((1, nh, kl), lambda bi, pt: (bi, 0, 0)),
        scratch_shapes=[pltpu.VMEM((PAGE_SLOTS, npg, dk, ps), F32), pltpu.SemaphoreType.DMA((PAGE_SLOTS,))],
    )
    return pl.pallas_call(
        functools.partial(_attn_sample_body, kl, npg, npages // npg, layer, n),
        grid_spec=grid_spec,
        out_shape=jax.ShapeDtypeStruct((n, nh, kl), BF16),
        compiler_params=_cparams("arbitrary"),
        name="attn_sample",
    )(page_table, q, kself, cache_t)


def _memkv_body(m_ref, g_ref, w_ref, o_ref):
    o_ref[...] = jnp.dot(_rms(m_ref[...], g_ref[...]).astype(BF16), w_ref[...], preferred_element_type=F32)


def _memkv(mem, g, wkv):
    n, d = mem.shape
    full = lambda s: pl.BlockSpec(s, lambda: (0,) * len(s))
    return pl.pallas_call(
        _memkv_body,
        in_specs=[full((n, d)), full((1, d)), full(wkv.shape)],
        out_specs=full((n, wkv.shape[1])),
        out_shape=jax.ShapeDtypeStruct((n, wkv.shape[1]), F32),
        name="mem_kv",
    )(mem, g, wkv)


def _memattn_sample_body(scale, q_ref, k_ref, v_ref, o_ref):
    gr, rows, _ = k_ref.shape
    nh = q_ref.shape[1]
    own = (lax.broadcasted_iota(jnp.int32, (nh, rows), 1) % nh) == lax.broadcasted_iota(jnp.int32, (nh, rows), 0)
    ss = [jnp.where(own, _mm_nt(q_ref[g], k_ref[g]) * scale, -jnp.inf) for g in range(gr)]
    ps = [jnp.exp(s - jnp.max(s, axis=-1, keepdims=True)) for s in ss]
    ps = [p / jnp.sum(p, axis=-1, keepdims=True) for p in ps]
    for g in range(gr):
        o_ref[g] = _mm(ps[g], v_ref[g]).astype(o_ref.dtype)


def _memattn_sample(q, cache_k, cache_v, layer, scale, gr):
    n, md = q.shape
    depth, _, m, nh, hd = cache_k.shape
    kv = pl.BlockSpec((None, gr, m * nh, hd), lambda i: (layer, i, 0, 0))
    qs = pl.BlockSpec((gr, nh, hd), lambda i: (i, 0, 0))
    return pl.pallas_call(
        functools.partial(_memattn_sample_body, scale),
        grid=(n // gr,),
        in_specs=[qs, kv, kv],
        out_specs=qs,
        out_shape=jax.ShapeDtypeStruct((n, nh, hd), BF16),
        compiler_params=_cparams("parallel"),
        name="memattn_sample",
    )(q.reshape(n, nh, hd), cache_k.reshape(depth, n, m * nh, hd), cache_v.reshape(depth, n, m * nh, hd)).reshape(n, md)


def _merge_body(nh, mem, h_ref, og_ref, ctx_ref, om_ref, *rest):
    if mem is not None:
        k_ref, v_ref, *rest = rest
    gates_ref, wo_ref, wuv_ref, mwo_ref, memwo_ref, wout_ref, post_ref, o_ref = rest
    d = h_ref.shape[1]
    if mem is None:
        om = om_ref[...]
    else:
        mh, scale = mem
        hd = om_ref.shape[1] // mh
        q, k, v = om_ref[...], k_ref[0], v_ref[0]
        heads = [slice(h * hd, (h + 1) * hd) for h in range(mh)]
        ss = [_mm_nt(q[:, sl], k[:, sl]) * scale for sl in heads]
        ps = [jnp.exp(s - jnp.max(s, axis=-1, keepdims=True)) for s in ss]
        ps = [p / jnp.sum(p, axis=-1, keepdims=True) for p in ps]
        om = jnp.concatenate([_mm(p, v[:, sl]) for p, sl in zip(ps, heads)], axis=1).astype(BF16)
    o_rwkv = jnp.dot(og_ref[...], wo_ref[...], preferred_element_type=F32)
    vs = []
    for pr in range(nh // 2):
        vp = (jnp.dot(ctx_ref[0, 2 * pr], wuv_ref[2 * pr], preferred_element_type=F32)
              + jnp.dot(ctx_ref[0, 2 * pr + 1], wuv_ref[2 * pr + 1], preferred_element_type=F32))
        vs.append(vp.astype(BF16))
    o_mla = jnp.dot(jnp.concatenate(vs, axis=1), mwo_ref[...], preferred_element_type=F32)
    o_mem = jnp.dot(om, memwo_ref[...], preferred_element_type=F32)
    merged = gates_ref[:, 0:d] * o_rwkv + gates_ref[:, d:2 * d] * o_mla + gates_ref[:, 2 * d:3 * d] * o_mem
    y = jnp.dot(merged.astype(BF16), wout_ref[...], preferred_element_type=F32)
    o_ref[...] = h_ref[...] + _rms(y, post_ref[...])


def _merge(h, nbatch, og, ctx, om, gates, wo, wuv, mwo, memwo, wout, post, nh, tm, mem=None):
    n, d = h.shape
    nb = (n // nbatch) // tm
    row = lambda i: (i, 0)
    kl = ctx.shape[-1]
    kv_specs, kv_args, mem_cfg = [], [], None
    if mem is not None:
        mk, mv, mh, scale = mem
        kv_specs = [pl.BlockSpec((1,) + mk.shape[1:], lambda i: (i // nb, 0, 0))] * 2
        kv_args, mem_cfg = [mk, mv], (mh, scale)
    return pl.pallas_call(
        functools.partial(_merge_body, nh, mem_cfg),
        grid=(n // tm,),
        in_specs=[pl.BlockSpec((tm, d), row), pl.BlockSpec((tm, og.shape[1]), row),
                  pl.BlockSpec((1, nh, tm, kl), lambda i: (i // nb, 0, i % nb, 0)),
                  pl.BlockSpec((tm, om.shape[1]), row)] + kv_specs + [pl.BlockSpec((tm, 3 * d), row),
                  _resident(wo.shape), _resident(wuv.shape), _resident(mwo.shape), _resident(memwo.shape),
                  _resident(wout.shape), _resident((1, d))],
        out_specs=pl.BlockSpec((tm, d), row),
        out_shape=jax.ShapeDtypeStruct((n, d), F32),
        compiler_params=_cparams("parallel"),
        name="merge",
    )(h, og, ctx, om, *kv_args, gates, wo, wuv, mwo, memwo, wout, post)


def _rope_tables(pos, rope, nh):
    half = rope // 2
    freqs = ROPE_BASE ** (-jnp.arange(half, dtype=F32) / half)
    ang = pos.astype(F32)[:, None] * freqs
    cos, sin = jnp.cos(ang), jnp.sin(ang)
    return jnp.tile(jnp.concatenate([cos, cos], axis=1), (1, nh)), jnp.tile(jnp.concatenate([-sin, sin], axis=1), (1, nh))


def _prep_weights(W, d):
    nh, hd = W["rwkv_r_k"].shape
    rd = nh * hd
    lora = W["rwkv_w2"].shape[0]
    glora = W["rwkv_g2"].shape[0]
    rp = 3 * rd + 2 * lora + glora
    ql = W["mla_q_norm"].shape[0]
    kl, mh, vh = W["mla_w_uv"].shape
    nope = W["mla_w_uk"].shape[2]
    rope = W["mla_w_qb"].shape[1] // mh - nope
    md = W["mem_w_k"].shape[1]
    half = rope // 2
    row = lambda x: x.reshape(1, -1)
    w_in = W["w_in"]
    o_cq, o_kv, o_pe, o_mem, o_g = rp, rp + ql, rp + ql + kl, rp + ql + kl + rope, rp + ql + kl + rope + md
    cols = [w_in[:, :o_pe], w_in[:, o_mem:], w_in[:, o_pe:o_mem],
            w_in[:, o_pe + half:o_mem], w_in[:, o_pe:o_pe + half]]
    width = sum(c.shape[1] for c in cols)
    pad = (-width) % LANES
    win = jnp.concatenate(cols + [jnp.zeros((d, pad), F32)], axis=1).astype(BF16)
    wqb = W["mla_w_qb"].reshape(ql, mh, nope + rope)
    wqb = jnp.concatenate([wqb[:, :, :nope].reshape(ql, mh * nope),
                           wqb[:, :, nope:].reshape(ql, mh * rope),
                           jnp.concatenate([wqb[:, :, nope + half:], wqb[:, :, nope:nope + half]], axis=2).reshape(ql, mh * rope)],
                          axis=1).astype(BF16)
    ukt = jnp.transpose(W["mla_w_uk"], (1, 2, 0))
    z = jnp.zeros_like(ukt[0])
    wuk = jnp.stack([jnp.concatenate([jnp.concatenate([ukt[2 * p], z], axis=1),
                                      jnp.concatenate([z, ukt[2 * p + 1]], axis=1)], axis=0)
                     for p in range(mh // 2)]).astype(BF16)
    uv = jnp.transpose(W["mla_w_uv"], (1, 0, 2))
    zv = jnp.zeros_like(uv[0])
    wuv = jnp.stack([jnp.concatenate([uv[h], zv] if h % 2 == 0 else [zv, uv[h]], axis=1)
                     for h in range(mh)]).astype(BF16)
    zl = jnp.zeros((lora, rd), F32)
    w2a2 = jnp.concatenate([jnp.concatenate([W["rwkv_w2"], zl], axis=1),
                            jnp.concatenate([zl, W["rwkv_a2"]], axis=1)], axis=0).astype(BF16)
    hid = jnp.arange(rd) // hd
    rw = dict(mu=row(W["rwkv_mu"]), w0=row(W["rwkv_w0"]), w2a2=w2a2, a0=row(W["rwkv_a0"]), g2=W["rwkv_g2"].astype(BF16),
              k_k=row(W["rwkv_k_k"]), k_a=row(W["rwkv_k_a"]), r_k=row(W["rwkv_r_k"]), lnx_g=row(W["rwkv_lnx_g"]),
              lnx_b=row(W["rwkv_lnx_b"]), e=(hid[:, None] == hid[None, :]).astype(BF16))
    dims = (rp, ql, kl, rope, md, 3 * d, mh, nope)
    return dict(
        dims=dims, nh=nh, hd=hd, lora=lora, rw=rw, win=win, wqb=wqb, wuk=wuk, wuv=wuv,
        ffn1=(row(W["ffn1_pre"]), row(W["ffn1_post"]), W["ffn1_gate"].astype(BF16), W["ffn1_up"].astype(BF16),
              W["ffn1_down"].astype(BF16)),
        ffn2=(row(W["ffn2_pre"]), row(W["ffn2_post"]), W["ffn2_gate"].astype(BF16), W["ffn2_up"].astype(BF16),
              W["ffn2_down"].astype(BF16)),
        mix_pre=row(W["mix_pre"]), mix_post=row(W["mix_post"]), q_norm=row(W["mla_q_norm"]), kv_norm=row(W["mla_kv_norm"]),
        mem_norm=row(W["mem_norm"]), mem_wkv=jnp.concatenate([W["mem_w_k"], W["mem_w_v"]], axis=1).astype(BF16),
        rwkv_wo=W["rwkv_w_o"].astype(BF16), mla_wo=W["mla_w_o"].astype(BF16), mem_wo=W["mem_w_o"].astype(BF16),
        w_out=W["w_out"].astype(BF16), qscale=float(nope + rope) ** -0.5 * LOG2E,
    )


def _tile(n, pref):
    t = min(pref, n)
    assert n % t == 0, (n, t)
    return t


def kernel(x_prompt, x_sample, cache_mla, state_rwkv, state_shift, cache_mem_k, cache_mem_v, page_table, mem_prompt, ffn1_pre, ffn1_post, ffn1_gate, ffn1_up, ffn1_down, mix_pre, mix_post, w_in, rwkv_mu, rwkv_w0, rwkv_w2, rwkv_a0, rwkv_a2, rwkv_g2, rwkv_k_k, rwkv_k_a, rwkv_r_k, rwkv_lnx_g, rwkv_lnx_b, rwkv_w_o, mla_q_norm, mla_w_qb, mla_kv_norm, mla_w_uk, mla_w_uv, mla_w_o, mem_norm, mem_w_k, mem_w_v, mem_w_o, w_out, ffn2_pre, ffn2_post, ffn2_gate, ffn2_up, ffn2_down):
    names = ("ffn1_pre", "ffn1_post", "ffn1_gate", "ffn1_up", "ffn1_down", "mix_pre", "mix_post", "w_in",
             "rwkv_mu", "rwkv_w0", "rwkv_w2", "rwkv_a0", "rwkv_a2", "rwkv_g2", "rwkv_k_k", "rwkv_k_a", "rwkv_r_k",
             "rwkv_lnx_g", "rwkv_lnx_b", "rwkv_w_o", "mla_q_norm", "mla_w_qb", "mla_kv_norm", "mla_w_uk", "mla_w_uv",
             "mla_w_o", "mem_norm", "mem_w_k", "mem_w_v", "mem_w_o", "w_out", "ffn2_pre", "ffn2_post", "ffn2_gate",
             "ffn2_up", "ffn2_down")
    stacked = (ffn1_pre, ffn1_post, ffn1_gate, ffn1_up, ffn1_down, mix_pre, mix_post, w_in,
               rwkv_mu, rwkv_w0, rwkv_w2, rwkv_a0, rwkv_a2, rwkv_g2, rwkv_k_k, rwkv_k_a, rwkv_r_k,
               rwkv_lnx_g, rwkv_lnx_b, rwkv_w_o, mla_q_norm, mla_w_qb, mla_kv_norm, mla_w_uk, mla_w_uv,
               mla_w_o, mem_norm, mem_w_k, mem_w_v, mem_w_o, w_out, ffn2_pre, ffn2_post, ffn2_gate,
               ffn2_up, ffn2_down)
    B, S, D = x_prompt.shape
    DB, T, _ = x_sample.shape
    assert T == 1, "decode groups carry one new token per request"
    depth = ffn1_pre.shape[0]
    page = cache_mla.shape[2]
    past_len = page_table.shape[1] * page
    mem_tokens, mem_heads, mem_hd = cache_mem_k.shape[2:]
    mem_scale = float(mem_hd) ** -0.5

    cache_t = jnp.swapaxes(cache_mla, 2, 3)
    state_t = jnp.transpose(state_rwkv, (0, 2, 3, 4, 1))
    xp = x_prompt.reshape(B * S, D)
    xs = x_sample.reshape(DB, D)
    outs = [[] for _ in range(8)]
    for l in range(depth):
        P = _prep_weights({n: w[l] for n, w in zip(names, stacked)}, D)
        rp, ql, kl, rope, md, gd, mh, nope = P["dims"]
        nh, hd, lora, rw = P["nh"], P["hd"], P["lora"], P["rw"]
        cos_p, sin_p = _rope_tables(jnp.arange(S), rope, mh)
        cos_s, sin_s = _rope_tables(jnp.full((DB,), past_len), rope, mh)
        tm_p, tm_w, tm_s = _tile(S, ROWS_PER_STEP), _tile(S, ROWS_PER_STEP_WIDE), DB

        h = _ffn(xs, *P["ffn1"], tm_s)
        prw_s, q, rows_s, kbf, qmem, gates = _inproj(h, 1, P["mix_pre"], P["win"], P["q_norm"], P["wqb"], P["wuk"],
                                                     P["kv_norm"], cos_s, sin_s, P["dims"], P["qscale"], tm_s)
        og, wkv_t = _rwkv_sample(prw_s, state_shift[l], state_t, l, rw, nh, hd, lora, _tile(nh, 2))
        wkv_s = jnp.transpose(wkv_t, (3, 0, 1, 2))
        ctx = _attn_sample(jnp.swapaxes(q[0], 0, 1), kbf.reshape(DB, 1, kl + rope), cache_t, l, page_table, kl)
        om = _memattn_sample(qmem, cache_mem_k, cache_mem_v, l, mem_scale, _tile(DB, 4))
        h = _merge(h, 1, og, jnp.swapaxes(ctx, 0, 1)[None], om, gates, P["rwkv_wo"], P["wuv"], P["mla_wo"], P["mem_wo"],
                   P["w_out"], P["mix_post"], mh, tm_s)
        xs = _ffn(h, *P["ffn2"], tm_s)

        mkv = _memkv(mem_prompt.reshape(B * mem_tokens, D), P["mem_norm"], P["mem_wkv"])
        mk_p, mv_p = mkv[:, :md].reshape(B, mem_tokens, md), mkv[:, md:].reshape(B, mem_tokens, md)
        h = _ffn(xp, *P["ffn1"], tm_w)
        prw, q, rows, kbf, qmem, gates = _inproj(h, B, P["mix_pre"], P["win"], P["q_norm"], P["wqb"], P["wuk"],
                                                 P["kv_norm"], cos_p, sin_p, P["dims"], P["qscale"], tm_p)
        og, wkv_p = _rwkv_prompt(prw.reshape(B, S, rp), rw, nh, hd, lora)
        ctx = _attn_prompt(q, kbf.reshape(B, S, kl + rope), kl)
        h = _merge(h, B, og.reshape(B * S, nh * hd), ctx, qmem, gates, P["rwkv_wo"], P["wuv"], P["mla_wo"], P["mem_wo"],
                   P["w_out"], P["mix_post"], mh, tm_w, mem=(mk_p, mv_p, mem_heads, mem_scale))
        xp = _ffn(h, *P["ffn2"], tm_w)
        rows_p, shift_p = rows.reshape(B, S, kl + rope), prw.reshape(B, S, rp)[:, -1]

        for lst, val in zip(outs, (rows_p, rows_s.reshape(DB, T, kl + rope), wkv_p, wkv_s, shift_p, prw_s,
                                   mk_p.reshape(B, mem_tokens, mem_heads, mem_hd),
                                   mv_p.reshape(B, mem_tokens, mem_heads, mem_hd))):
            lst.append(val)
    return (xp.reshape(B, S, D), xs.reshape(DB, T, D)) + tuple(jnp.stack(o) for o in outs)
```
